```python
import jax, jax.numpy as jnp
from jax import lax
import numpy as np

D_MODEL = 2048
BATCH = 8
SEQ = 2048
DEPTH = 4

HEAD_DIM = 64
N_ATT_HEADS = 12
N_RWKV_HEADS = 12
D_ATT = N_ATT_HEADS * HEAD_DIM
D_RWKV = N_RWKV_HEADS * HEAD_DIM
D_POOL = D_MODEL - D_ATT - D_RWKV
N_POOL_GROUPS = 4
POOL_GROUP = D_POOL // N_POOL_GROUPS
POOL_WINDOWS = (2, 4, 8, 16)
DILATED_BRANCHES = ((128, 1), (512, 4), (2048, 16))
ATT_BLOCK = 128
ROPE_THETA = 10000.0
RWKV_DECAY_LORA = 96
RWKV_AAA_LORA = 96
RWKV_GATE_LORA = 256
D_RWKV_IN = 3 * D_RWKV + RWKV_DECAY_LORA + RWKV_AAA_LORA + RWKV_GATE_LORA
D_IN = 3 * D_ATT + D_RWKV_IN + D_POOL
D_FF = 256 * ((8 * D_MODEL // 3 + 255) // 256)
NORM_EPS = 1e-6
RWKV_GN_EPS = 64e-5

kernel_name = 'hybrid_dilated_rwkv7_pool_macaron'


def rms_norm(x, g):
    xf = x.astype(jnp.float32)
    y = xf * lax.rsqrt(jnp.mean(xf * xf, axis=-1, keepdims=True) + NORM_EPS)
    return (y * g.astype(jnp.float32)).astype(x.dtype)


def swiglu(h, w_gate, w_up, w_down):
    return (jax.nn.silu(h @ w_gate) * (h @ w_up)) @ w_down


def rope_tables(seq_len, dtype):
    inv_freq = ROPE_THETA ** (-jnp.arange(0, HEAD_DIM, 2, dtype=jnp.float32) / HEAD_DIM)
    ang = jnp.arange(seq_len, dtype=jnp.float32)[:, None] * inv_freq[None, :]
    return jnp.cos(ang).astype(dtype), jnp.sin(ang).astype(dtype)


def rotary(x, cos, sin):
    x1, x2 = jnp.split(x, 2, axis=-1)
    c = cos[None, :, None, :]
    s = sin[None, :, None, :]
    return jnp.concatenate([x1 * c - x2 * s, x2 * c + x1 * s], axis=-1)


def to_strided(x, d):
    b, s = x.shape[:2]
    rest = x.shape[2:]
    x = jnp.moveaxis(x.reshape((b, s // d, d) + rest), 2, 1)
    return x.reshape((b * d, s // d) + rest)


def from_strided(x, b, d):
    l = x.shape[1]
    rest = x.shape[2:]
    x = jnp.moveaxis(x.reshape((b, d, l) + rest), 1, 2)
    return x.reshape((b, l * d) + rest)


def dilated_branch(q, k, v, window, dilation):
    b, s, h, dh = q.shape
    n_back = window // dilation
    qs, ks, vs = (to_strided(t, dilation) for t in (q, k, v))
    n, l = qs.shape[:2]
    nblk = -(-l // ATT_BLOCK)
    lp = nblk * ATT_BLOCK
    pad = ((0, 0), (0, lp - l), (0, 0), (0, 0))
    qs, ks, vs = (jnp.pad(t, pad).reshape(n, nblk, ATT_BLOCK, h, dh) for t in (qs, ks, vs))

    def with_prev(t):
        prev = jnp.pad(t[:, :-1], ((0, 0), (1, 0), (0, 0), (0, 0), (0, 0)))
        return jnp.concatenate([prev, t], axis=2)

    kb, vb = with_prev(ks), with_prev(vs)
    scores = jnp.einsum('nbqhd,nbkhd->nbhqk', qs, kb).astype(jnp.float32) * (dh ** -0.5)
    qi = jnp.arange(ATT_BLOCK)[:, None]
    kj = jnp.arange(2 * ATT_BLOCK)[None, :]
    dist = ATT_BLOCK + qi - kj
    key_pos = jnp.arange(nblk)[:, None, None] * ATT_BLOCK + kj - ATT_BLOCK
    valid = (dist >= 0) & (dist <= n_back) & (key_pos >= 0)
    scores = jnp.where(valid[None, :, None], scores, -jnp.inf)
    m = jnp.max(scores, axis=-1, keepdims=True)
    p = jnp.exp(scores - m)
    den = jnp.sum(p, axis=-1, keepdims=True)
    o = jnp.einsum('nbhqk,nbkhd->nbqhd', (p / den).astype(v.dtype), vb)
    lse = jnp.moveaxis((m + jnp.log(den))[..., 0], 2, 3)
    o = o.reshape(n, lp, h, dh)[:, :l]
    lse = lse.reshape(n, lp, h)[:, :l]
    return from_strided(o, b, dilation), from_strided(lse, b, dilation)


def dilated_attention(q, k, v):
    outs, lses = [], []
    for window, dilation in DILATED_BRANCHES:
        o, lse = dilated_branch(q, k, v, window, dilation)
        outs.append(o)
        lses.append(lse)
    alpha = jax.nn.softmax(jnp.stack(lses, axis=0), axis=0)
    out = jnp.einsum('gbsh,gbshd->bshd', alpha, jnp.stack(outs, axis=0).astype(jnp.float32))
    return out.astype(q.dtype)


def token_shift(x):
    return jnp.pad(x[:, :-1], ((0, 0), (1, 0), (0, 0)))


def rwkv7_mixer(z, mu, w_up, w0, a_up, a0, g_up, k_k, k_a, r_k, gn_w, gn_b):
    out_dtype = z.dtype
    z = z.astype(jnp.float32)
    b, s, _ = z.shape
    z = z + (token_shift(z) - z) * mu
    splits = [D_RWKV, 2 * D_RWKV, 3 * D_RWKV, 3 * D_RWKV + RWKV_DECAY_LORA,
              3 * D_RWKV + RWKV_DECAY_LORA + RWKV_AAA_LORA]
    r, k, v, zw, za, zg = jnp.split(z, splits, axis=-1)
    log_w = -jax.nn.softplus(-(w0 + jnp.tanh(zw) @ w_up)) - 0.5
    decay = jnp.exp(-jnp.exp(log_w))
    a = jax.nn.sigmoid(a0 + za @ a_up)
    g = jax.nn.sigmoid(zg) @ g_up
    kk = k * k_k
    k = k * (1.0 + (a - 1.0) * k_a)

    def heads(t):
        return t.reshape(b, s, N_RWKV_HEADS, HEAD_DIM).astype(jnp.float32)

    r, k, v, kk, a, decay = (heads(t) for t in (r, k, v, kk, a, decay))
    kk = kk / jnp.maximum(jnp.sqrt(jnp.sum(kk * kk, axis=-1, keepdims=True)), 1e-12)

    def step(state, inp):
        r_t, w_t, k_t, v_t, kk_t, a_t = inp
        sa = jnp.einsum('bhvk,bhk->bhv', state, kk_t)
        state = (state * w_t[:, :, None, :]
                 - sa[..., None] * (kk_t * a_t)[:, :, None, :]
                 + v_t[..., None] * k_t[:, :, None, :])
        return state, jnp.einsum('bhvk,bhk->bhv', state, r_t)

    state0 = jnp.zeros((b, N_RWKV_HEADS, HEAD_DIM, HEAD_DIM), jnp.float32)
    xs = tuple(jnp.moveaxis(t, 1, 0) for t in (r, decay, k, v, kk, a))
    _, y = lax.scan(step, state0, xs)
    y = jnp.moveaxis(y, 0, 1)
    mean = jnp.mean(y, axis=-1, keepdims=True)
    var = jnp.mean(jnp.square(y - mean), axis=-1, keepdims=True)
    y = ((y - mean) * lax.rsqrt(var + RWKV_GN_EPS)).reshape(b, s, D_RWKV) * gn_w + gn_b
    bonus = jnp.sum(r * k * r_k, axis=-1, keepdims=True) * v
    y = (y + bonus.reshape(b, s, D_RWKV)) * g
    return y.astype(out_dtype)


def pool_mixer(u, pool_w, pool_scale):
    b, s, _ = u.shape
    uf = u.astype(jnp.float32).reshape(b, s, N_POOL_GROUPS, POOL_GROUP)
    c = jnp.cumsum(uf, axis=1)
    t = jnp.arange(1, s + 1, dtype=jnp.float32)
    outs = []
    for gi, w in enumerate(POOL_WINDOWS):
        cg = c[:, :, gi]
        prev = jnp.pad(cg[:, :-w], ((0, 0), (w, 0), (0, 0)))
        mean = (cg - prev) / jnp.minimum(t, float(w))[None, :, None]
        outs.append(mean - uf[:, :, gi])
    p = jnp.stack(outs, axis=2).astype(u.dtype)
    y = jnp.einsum('bsgc,gcd->bsgd', p, pool_w).reshape(b, s, D_POOL)
    return y * pool_scale


def setup_inputs(seed: int = 0) -> dict:
    key = jax.random.key(seed)
    ks = iter(jax.random.split(key, 32))
    f32 = jnp.float32

    def nrm(shape, scale):
        return jax.random.normal(next(ks), shape, f32) * scale

    def gain(shape):
        return 1.0 + 0.05 * jax.random.normal(next(ks), shape, f32)

    def unif(shape, lo, hi):
        return jax.random.uniform(next(ks), shape, f32, lo, hi)

    L, D, F = DEPTH, D_MODEL, D_FF
    return {
        'x': nrm((BATCH, SEQ, D), 1.0),
        'ffn1_norm_pre': gain((L, D)),
        'ffn1_norm_post': gain((L, D)),
        'ffn1_w_gate': nrm((L, D, F), D ** -0.5),
        'ffn1_w_up': nrm((L, D, F), D ** -0.5),
        'ffn1_w_down': nrm((L, F, D), F ** -0.5),
        'mix_norm_pre': gain((L, D)),
        'mix_norm_post': gain((L, D)),
        'w_in': nrm((L, D, D_IN), D ** -0.5),
        'w_out': nrm((L, D, D), D ** -0.5),
        'rwkv_mu': unif((L, D_RWKV_IN), 0.0, 1.0),
        'rwkv_w_up': nrm((L, RWKV_DECAY_LORA, D_RWKV), 0.5 * RWKV_DECAY_LORA ** -0.5),
        'rwkv_w0': unif((L, D_RWKV), -6.0, -1.0),
        'rwkv_a_up': nrm((L, RWKV_AAA_LORA, D_RWKV), RWKV_AAA_LORA ** -0.5),
        'rwkv_a0': nrm((L, D_RWKV), 0.1),
        'rwkv_g_up': nrm((L, RWKV_GATE_LORA, D_RWKV), RWKV_GATE_LORA ** -0.5),
        'rwkv_k_k': 0.85 + nrm((L, D_RWKV), 0.05),
        'rwkv_k_a': gain((L, D_RWKV)),
        'rwkv_r_k': nrm((L, N_RWKV_HEADS, HEAD_DIM), 0.1),
        'rwkv_gn_w': gain((L, D_RWKV)),
        'rwkv_gn_b': nrm((L, D_RWKV), 0.01),
        'pool_w': nrm((L, N_POOL_GROUPS, POOL_GROUP, POOL_GROUP), POOL_GROUP ** -0.5),
        'pool_scale': gain((L, D_POOL)),
        'ffn2_norm_pre': gain((L, D)),
        'ffn2_norm_post': gain((L, D)),
        'ffn2_w_gate': nrm((L, D, F), D ** -0.5),
        'ffn2_w_up': nrm((L, D, F), D ** -0.5),
        'ffn2_w_down': nrm((L, F, D), F ** -0.5),
    }


def reference(x, ffn1_norm_pre, ffn1_norm_post, ffn1_w_gate, ffn1_w_up, ffn1_w_down,
              mix_norm_pre, mix_norm_post, w_in, w_out, rwkv_mu, rwkv_w_up, rwkv_w0,
              rwkv_a_up, rwkv_a0, rwkv_g_up, rwkv_k_k, rwkv_k_a, rwkv_r_k, rwkv_gn_w,
              rwkv_gn_b, pool_w, pool_scale, ffn2_norm_pre, ffn2_norm_post, ffn2_w_gate,
              ffn2_w_up, ffn2_w_down):
    b, s, _ = x.shape
    cos, sin = rope_tables(s, x.dtype)
    for i in range(DEPTH):
        h = rms_norm(x, ffn1_norm_pre[i])
        x = x + 0.5 * rms_norm(swiglu(h, ffn1_w_gate[i], ffn1_w_up[i], ffn1_w_down[i]),
                               ffn1_norm_post[i])
        h = rms_norm(x, mix_norm_pre[i])
        z = h @ w_in[i]
        zq, zk, zv, zr, zp = jnp.split(
            z, [D_ATT, 2 * D_ATT, 3 * D_ATT, 3 * D_ATT + D_RWKV_IN], axis=-1)
        q = rotary(zq.reshape(b, s, N_ATT_HEADS, HEAD_DIM), cos, sin)
        k = rotary(zk.reshape(b, s, N_ATT_HEADS, HEAD_DIM), cos, sin)
        v = zv.reshape(b, s, N_ATT_HEADS, HEAD_DIM)
        y_att = dilated_attention(q, k, v).reshape(b, s, D_ATT)
        y_rwkv = rwkv7_mixer(zr, rwkv_mu[i], rwkv_w_up[i], rwkv_w0[i], rwkv_a_up[i],
                             rwkv_a0[i], rwkv_g_up[i], rwkv_k_k[i], rwkv_k_a[i],
                             rwkv_r_k[i], rwkv_gn_w[i], rwkv_gn_b[i])
        y_pool = pool_mixer(zp, pool_w[i], pool_scale[i])
        mix = jnp.concatenate([y_att, y_rwkv, y_pool], axis=-1) @ w_out[i]
        x = x + rms_norm(mix, mix_norm_post[i])
        h = rms_norm(x, ffn2_norm_pre[i])
        x = x + 0.5 * rms_norm(swiglu(h, ffn2_w_gate[i], ffn2_w_up[i], ffn2_w_down[i]),
                               ffn2_norm_post[i])
    return x
```

```python
import functools
import math

import jax
import jax.numpy as jnp
from jax import lax
from jax.experimental import pallas as pl
from jax.experimental.pallas import tpu as pltpu

F32 = jnp.float32
BF16 = jnp.bfloat16

HEAD_DIM = 64
N_ATT_HEADS = 12
N_RWKV_HEADS = 12
D_ATT = N_ATT_HEADS * HEAD_DIM
D_RWKV = N_RWKV_HEADS * HEAD_DIM
D_POOL = 512
N_POOL_GROUPS = 4
POOL_GROUP = D_POOL // N_POOL_GROUPS
POOL_WINDOWS = (2, 4, 8, 16)
DILATED_BRANCHES = ((128, 1), (512, 4), (2048, 16))
ROPE_THETA = 10000.0
RWKV_DECAY_LORA = 96
RWKV_AAA_LORA = 96
RWKV_GATE_LORA = 256
NORM_EPS = 1e-6
RWKV_GN_EPS = 64e-5

LANES = 128
LORA_PAD = 128
COL_ATT = 0
COL_RKV = 3 * D_ATT
COL_LORA = COL_RKV + 3 * D_RWKV
D_LORA_PACK = 2 * LORA_PAD + RWKV_GATE_LORA
COL_POOL = COL_LORA + D_LORA_PACK
D_PACK = COL_POOL + D_POOL

RWKV_CHUNK = 64
ATT_TILE = 256
VMEM_LIMIT = 56 * 1024 * 1024

NEG_BIG = -1e30


def _rms(xf, g):
    return xf * lax.rsqrt(jnp.mean(xf * xf, axis=-1, keepdims=True) + NORM_EPS) * g


def _dot(a, b):
    return jnp.dot(a, b, preferred_element_type=F32)


def _dot_nt(a, b):
    return lax.dot_general(a, b, (((1,), (1,)), ((), ())), preferred_element_type=F32)


def _ffn_kernel(x_ref, gpre_ref, gpost_ref, wg_ref, wu_ref, wd_ref, o_ref, h_ref, acc_ref):
    j = pl.program_id(1)

    @pl.when(j == 0)
    def _():
        h_ref[...] = _rms(x_ref[...], gpre_ref[...]).astype(BF16)
        acc_ref[...] = jnp.zeros_like(acc_ref)

    h = h_ref[...]
    g = _dot(h, wg_ref[...])
    u = _dot(h, wu_ref[...])
    a = (g * jax.nn.sigmoid(g) * u).astype(BF16)
    acc_ref[...] += _dot(a, wd_ref[...])

    @pl.when(j == pl.num_programs(1) - 1)
    def _():
        o_ref[...] = x_ref[...] + 0.5 * _rms(acc_ref[...], gpost_ref[...])


def _ffn(x, g_pre, g_post, wg, wu, wd, *, tm, tf):
    t, d = x.shape
    f = wg.shape[1]
    return pl.pallas_call(
        _ffn_kernel,
        grid=(t // tm, f // tf),
        in_specs=[
            pl.BlockSpec((tm, d), lambda i, j: (i, 0)),
            pl.BlockSpec((1, d), lambda i, j: (0, 0)),
            pl.BlockSpec((1, d), lambda i, j: (0, 0)),
            pl.BlockSpec((d, tf), lambda i, j: (0, j)),
            pl.BlockSpec((d, tf), lambda i, j: (0, j)),
            pl.BlockSpec((tf, d), lambda i, j: (j, 0)),
        ],
        out_specs=pl.BlockSpec((tm, d), lambda i, j: (i, 0)),
        out_shape=jax.ShapeDtypeStruct((t, d), F32),
        scratch_shapes=[pltpu.VMEM((tm, d), BF16), pltpu.VMEM((tm, d), F32)],
        compiler_params=pltpu.CompilerParams(
            dimension_semantics=("parallel", "arbitrary"), vmem_limit_bytes=VMEM_LIMIT),
        name="ffn",
    )(x, g_pre, g_post, wg, wu, wd)


def _proj_kernel(x_ref, g_ref, w_ref, o_ref, h_ref):
    @pl.when(pl.program_id(1) == 0)
    def _():
        h_ref[...] = _rms(x_ref[...], g_ref[...]).astype(BF16)

    o_ref[...] = _dot(h_ref[...], w_ref[...])


def _proj(x, g, w, *, tm, tn):
    t, d = x.shape
    n = w.shape[1]
    return pl.pallas_call(
        _proj_kernel,
        grid=(t // tm, n // tn),
        in_specs=[
            pl.BlockSpec((tm, d), lambda i, j: (i, 0)),
            pl.BlockSpec((1, d), lambda i, j: (0, 0)),
            pl.BlockSpec((d, tn), lambda i, j: (0, j)),
        ],
        out_specs=pl.BlockSpec((tm, tn), lambda i, j: (i, j)),
        out_shape=jax.ShapeDtypeStruct((t, n), F32),
        scratch_shapes=[pltpu.VMEM((tm, d), BF16)],
        compiler_params=pltpu.CompilerParams(
            dimension_semantics=("parallel", "arbitrary"), vmem_limit_bytes=VMEM_LIMIT),
        name="proj_in",
    )(x, g, w)


def _mixout_kernel(x_ref, ya_ref, yr_ref, yp_ref, wa_ref, wr_ref, wp_ref, g_ref, o_ref):
    mix = _dot(ya_ref[...], wa_ref[...]) + _dot(yr_ref[...], wr_ref[...]) + _dot(yp_ref[...], wp_ref[...])
    o_ref[...] = x_ref[...] + _rms(mix, g_ref[...])


def _mixout(x, ya, yr, yp, wa, wr, wp, g, *, tm):
    t, d = x.shape
    return pl.pallas_call(
        _mixout_kernel,
        grid=(t // tm,),
        in_specs=[
            pl.BlockSpec((tm, d), lambda i: (i, 0)),
            pl.BlockSpec((tm, ya.shape[1]), lambda i: (i, 0)),
            pl.BlockSpec((tm, yr.shape[1]), lambda i: (i, 0)),
            pl.BlockSpec((tm, yp.shape[1]), lambda i: (i, 0)),
            pl.BlockSpec(wa.shape, lambda i: (0, 0)),
            pl.BlockSpec(wr.shape, lambda i: (0, 0)),
            pl.BlockSpec(wp.shape, lambda i: (0, 0)),
            pl.BlockSpec((1, d), lambda i: (0, 0)),
        ],
        out_specs=pl.BlockSpec((tm, d), lambda i: (i, 0)),
        out_shape=jax.ShapeDtypeStruct((t, d), F32),
        compiler_params=pltpu.CompilerParams(
            dimension_semantics=("parallel",), vmem_limit_bytes=VMEM_LIMIT),
        name="proj_out",
    )(x, ya, yr, yp, wa, wr, wp, g)


def _att_bias_tile(off, tile):
    row = lax.broadcasted_iota(jnp.int32, (tile, tile), 0)
    col = lax.broadcasted_iota(jnp.int32, (tile, tile), 1)
    delta = off * tile + row - col
    cnt = jnp.zeros((tile, tile), jnp.int32)
    for window, dil in DILATED_BRANCHES:
        hit = (delta >= 0) & (delta <= window) & ((delta & (dil - 1)) == 0)
        cnt = cnt + hit.astype(jnp.int32)
    return jnp.where(cnt == 3, math.log(3.0),
                     jnp.where(cnt == 2, math.log(2.0), jnp.where(cnt == 1, 0.0, NEG_BIG))).astype(F32)


def _att_kernel(q_ref, k_ref, v_ref, cos_ref, sin_ref, o_ref,
                qa_s, qb_s, k_s, va_s, vb_s, bias_s, *, seq, tile):
    nblk = seq // tile

    @pl.when(pl.program_id(1) == 0)
    def _():
        for off in range(nblk):
            bias_s[off] = _att_bias_tile(off, tile)

    lane = lax.broadcasted_iota(jnp.int32, (tile, LANES), 1)
    first_half = (lane & (HEAD_DIM - 1)) < (HEAD_DIM // 2)
    head_a = lane < HEAD_DIM

    def rope(x, cos, sin):
        swapped = jnp.where(first_half, pltpu.roll(x, LANES - HEAD_DIM // 2, 1), pltpu.roll(x, HEAD_DIM // 2, 1))
        return x * cos + swapped * sin

    def prep(i, carry):
        rows = pl.ds(pl.multiple_of(i * tile, tile), tile)
        cos = cos_ref[rows, :]
        sin = sin_ref[rows, :]
        q = rope(q_ref[0, rows, :], cos, sin) * (HEAD_DIM ** -0.5)
        k = rope(k_ref[0, rows, :], cos, sin)
        v = v_ref[0, rows, :]
        qa_s[rows, :] = jnp.where(head_a, q, 0.0).astype(BF16)
        qb_s[rows, :] = jnp.where(head_a, 0.0, q).astype(BF16)
        k_s[rows, :] = k.astype(BF16)
        va_s[rows, :] = jnp.where(head_a, v, 1.0).astype(BF16)
        vb_s[rows, :] = jnp.where(head_a, 1.0, v).astype(BF16)
        return carry

    lax.fori_loop(0, nblk, prep, 0)

    def q_block(qi, carry):
        qrows = pl.ds(pl.multiple_of(qi * tile, tile), tile)
        qa = qa_s[qrows, :]
        qb = qb_s[qrows, :]

        def kv_step(kj, c):
            m_a, acc_a, m_b, acc_b = c
            krows = pl.ds(pl.multiple_of(kj * tile, tile), tile)
            kt = k_s[krows, :]
            bias = bias_s[qi - kj]

            def one(q, v_s, m, acc):
                s = _dot_nt(q, kt) + bias
                m_new = jnp.maximum(m, jnp.max(s, axis=-1, keepdims=True))
                p = jnp.exp(s - m_new)
                acc = jnp.exp(m - m_new) * acc + _dot(p.astype(BF16), v_s[krows, :])
                return m_new, acc

            m_a, acc_a = one(qa, va_s, m_a, acc_a)
            m_b, acc_b = one(qb, vb_s, m_b, acc_b)
            return m_a, acc_a, m_b, acc_b

        m0 = jnp.full((tile, 1), NEG_BIG, F32)
        a0 = jnp.zeros((tile, LANES), F32)
        _, acc_a, _, acc_b = lax.fori_loop(0, qi + 1, kv_step, (m0, a0, m0, a0))
        o_a = acc_a / pltpu.roll(acc_a, HEAD_DIM, 1)
        o_b = acc_b / pltpu.roll(acc_b, HEAD_DIM, 1)
        o_ref[0, qrows, :] = jnp.where(head_a, o_a, o_b).astype(o_ref.dtype)
        return carry

    lax.fori_loop(0, nblk, q_block, 0)


def _attention(z3, cos_t, sin_t):
    b, s, _ = z3.shape
    tile = min(ATT_TILE, s)
    npair = D_ATT // LANES
    kern = functools.partial(_att_kernel, seq=s, tile=tile)
    return pl.pallas_call(
        kern,
        grid=(b, npair),
        in_specs=[
            pl.BlockSpec((1, s, LANES), lambda i, p: (i, 0, COL_ATT // LANES + p)),
            pl.BlockSpec((1, s, LANES), lambda i, p: (i, 0, (COL_ATT + D_ATT) // LANES + p)),
            pl.BlockSpec((1, s, LANES), lambda i, p: (i, 0, (COL_ATT + 2 * D_ATT) // LANES + p)),
            pl.BlockSpec((s, LANES), lambda i, p: (0, 0)),
            pl.BlockSpec((s, LANES), lambda i, p: (0, 0)),
        ],
        out_specs=pl.BlockSpec((1, s, LANES), lambda i, p: (i, 0, p)),
        out_shape=jax.ShapeDtypeStruct((b, s, D_ATT), BF16),
        scratch_shapes=[pltpu.VMEM((s, LANES), BF16)] * 5 + [pltpu.VMEM((s // tile, tile, tile), F32)],
        compiler_params=pltpu.CompilerParams(
            dimension_semantics=("parallel", "arbitrary"), vmem_limit_bytes=VMEM_LIMIT),
        name="dilated_attention",
    )(z3, z3, z3, cos_t, sin_t)


POOL_PAD = 16


def _pool_kernel(u_ref, w_ref, sc_ref, o_ref, pad_s, *, seq):
    t1 = (lax.broadcasted_iota(jnp.int32, (seq, POOL_GROUP), 0) + 1).astype(F32)
    pad_s[0:POOL_PAD, :] = jnp.zeros((POOL_PAD, POOL_GROUP), F32)
    for gi, w in enumerate(POOL_WINDOWS):
        cols = slice(gi * POOL_GROUP, (gi + 1) * POOL_GROUP)
        u = u_ref[0, :, cols]
        s = u
        k = 1
        while k < w:
            pad_s[POOL_PAD:POOL_PAD + seq, :] = s
            s = s + pad_s[POOL_PAD - k:POOL_PAD - k + seq, :]
            k *= 2
        p = s / jnp.minimum(t1, float(w)) - u
        y = _dot(p.astype(BF16), w_ref[gi])
        o_ref[0, :, cols] = (y * sc_ref[:, cols]).astype(o_ref.dtype)


def _pool(z3, pool_w, pool_scale):
    b, s, _ = z3.shape
    kern = functools.partial(_pool_kernel, seq=s)
    return pl.pallas_call(
        kern,
        grid=(b,),
        in_specs=[
            pl.BlockSpec((1, s, D_POOL), lambda i: (i, 0, COL_POOL // D_POOL)),
            pl.BlockSpec(pool_w.shape, lambda i: (0, 0, 0)),
            pl.BlockSpec((1, D_POOL), lambda i: (0, 0)),
        ],
        out_specs=pl.BlockSpec((1, s, D_POOL), lambda i: (i, 0, 0)),
        out_shape=jax.ShapeDtypeStruct((b, s, D_POOL), BF16),
        scratch_shapes=[pltpu.VMEM((s + POOL_PAD, POOL_GROUP), F32)],
        compiler_params=pltpu.CompilerParams(
            dimension_semantics=("parallel",), vmem_limit_bytes=VMEM_LIMIT),
        name="pool_mixer",
    )(z3, pool_w, pool_scale)


def _split_dot(x, ones_b, terms):
    acc = None
    rem = x
    for _ in range(terms):
        piece = rem.astype(BF16)
        part = _dot(piece, ones_b)
        acc = part if acc is None else acc + part
        rem = rem - piece.astype(F32)
    return acc


def _rwkv_kernel(zr_ref, zrp_ref, zl_ref, zlp_ref, mur_ref, mul_ref, wup_ref, w0_ref, aup_ref, a0_ref,
                 gup_ref, kk_ref, ka_ref, rk_ref, gnw_ref, gnb_ref, o_ref, h_s, sh_s, y_s, *, chunk):
    c = pl.program_id(1)
    npair = D_RWKV // LANES
    halo = 8

    @pl.when(c == 0)
    def _():
        h_s[...] = jnp.zeros_like(h_s)

    first = c == 0
    zr = zr_ref[0]
    zl = zl_ref[0]
    wr = 3 * D_RWKV
    sh_s[0:halo, 0:wr] = jnp.where(first, 0.0, zrp_ref[0])
    sh_s[0:halo, wr:wr + D_LORA_PACK] = jnp.where(first, 0.0, zlp_ref[0])
    sh_s[halo:halo + chunk, 0:wr] = zr
    sh_s[halo:halo + chunk, wr:wr + D_LORA_PACK] = zl
    zr_prev = sh_s[halo - 1:halo - 1 + chunk, 0:wr]
    zl_prev = sh_s[halo - 1:halo - 1 + chunk, wr:wr + D_LORA_PACK]
    zr = zr + (zr_prev - zr) * mur_ref[...]
    zl = zl + (zl_prev - zl) * mul_ref[...]
    r = zr[:, 0:D_RWKV]
    k = zr[:, D_RWKV:2 * D_RWKV]
    v = zr[:, 2 * D_RWKV:3 * D_RWKV]
    zw = zl[:, 0:LORA_PAD]
    za = zl[:, LORA_PAD:2 * LORA_PAD]
    zg = zl[:, 2 * LORA_PAD:D_LORA_PACK]

    lw = w0_ref[...] + _dot(jnp.tanh(zw).astype(BF16), wup_ref[...])
    nlw = -lw
    softplus = jnp.maximum(nlw, 0.0) + jnp.log(1.0 + jnp.exp(-jnp.abs(nlw)))
    ld = -jnp.exp(-softplus - 0.5)
    a = jax.nn.sigmoid(a0_ref[...] + _dot(za.astype(BF16), aup_ref[...]))
    g = _dot(jax.nn.sigmoid(zg).astype(BF16), gup_ref[...])
    kk = k * kk_ref[...]
    k2 = k * (1.0 + (a - 1.0) * ka_ref[...])

    row = lax.broadcasted_iota(jnp.int32, (LANES, LANES), 0)
    col = lax.broadcasted_iota(jnp.int32, (LANES, LANES), 1)
    head_ones = ((row < HEAD_DIM) == (col < HEAD_DIM)).astype(BF16)

    def seg_sum(x):
        return jnp.concatenate(
            [_split_dot(x[:, p * LANES:(p + 1) * LANES], head_ones, 2) for p in range(npair)], axis=1)

    kk = kk / jnp.maximum(jnp.sqrt(seg_sum(kk * kk)), 1e-12)
    bonus = seg_sum(r * k2 * rk_ref[...]) * v
    bb = kk * a

    trow = lax.broadcasted_iota(jnp.int32, (chunk, chunk), 0)
    tcol = lax.broadcasted_iota(jnp.int32, (chunk, chunk), 1)
    cum = _split_dot_left((trow >= tcol).astype(BF16), ld, 3)
    cum_end = cum[chunk - 1:chunk, :]
    r_t = r * jnp.exp(cum)
    kk_t = kk * jnp.exp(cum - ld)
    e_inv = jnp.exp(-cum)
    b_h = bb * e_inv
    k_h = k2 * e_inv
    e_end = jnp.exp(cum_end - cum)
    b_a = bb * e_end
    k_a = k2 * e_end
    a_end = jnp.exp(cum_end)

    lane = lax.broadcasted_iota(jnp.int32, (chunk, LANES), 1)
    head_a = lane < HEAD_DIM
    strict_lower = row > col
    lower = row >= col
    eye = (row == col).astype(F32)

    def stack2(x):
        return jnp.concatenate([jnp.where(head_a, x, 0.0), jnp.where(head_a, 0.0, x)], axis=0)

    for p in range(npair):
        sl = slice(p * LANES, (p + 1) * LANES)
        kks = stack2(kk_t[:, sl]).astype(BF16)
        bs = stack2(b_h[:, sl]).astype(BF16)
        ks = stack2(k_h[:, sl]).astype(BF16)
        rs = stack2(r_t[:, sl]).astype(BF16)
        vs = stack2(v[:, sl]).astype(BF16)
        bat = stack2(b_a[:, sl]).T.astype(BF16)
        kat = stack2(k_a[:, sl]).T.astype(BF16)
        a_col = jnp.broadcast_to(a_end[:, sl], (LANES, LANES)).T

        l_b = jnp.where(strict_lower, _dot_nt(kks, bs), 0.0)
        l_k = jnp.where(strict_lower, _dot_nt(kks, ks), 0.0).astype(BF16)
        m_b = jnp.where(lower, _dot_nt(rs, bs), 0.0).astype(BF16)
        m_k = jnp.where(lower, _dot_nt(rs, ks), 0.0).astype(BF16)

        t_inv = eye - jnp.where(((row ^ col) < 2) & ((row & 1) != 0) & ((col & 1) == 0), l_b, 0.0)
        s = 2
        while s < HEAD_DIM:
            sel = ((row ^ col) < 2 * s) & ((row & s) != 0) & ((col & s) == 0)
            l_s = jnp.where(sel, l_b, 0.0).astype(BF16)
            t_b = t_inv.astype(BF16)
            t_inv = t_inv - _dot(t_b, _dot(l_s, t_b).astype(BF16))
            s *= 2
        t_b = t_inv.astype(BF16)

        w_m = _dot(t_b, kks).astype(BF16)
        u_t = _dot(t_b, _dot(l_k, vs).astype(BF16))
        y0 = _dot(m_k, vs)
        g_m = _dot(kat, vs)

        h = h_s[p]
        h_b = h.astype(BF16)
        u = -(_dot(w_m, h_b) + u_t)
        u_b = u.astype(BF16)
        y = _dot(rs, h_b) + _dot(m_b, u_b) + y0
        h_s[p] = a_col * h + _dot(bat, u_b) + g_m
        y_s[:, sl] = y[0:chunk, :] + y[chunk:2 * chunk, :]

    y = y_s[...]
    mean = seg_sum(y) * (1.0 / HEAD_DIM)
    d = y - mean
    var = seg_sum(d * d) * (1.0 / HEAD_DIM)
    yn = d * lax.rsqrt(var + RWKV_GN_EPS) * gnw_ref[...] + gnb_ref[...]
    o_ref[0] = ((yn + bonus) * g).astype(o_ref.dtype)


def _split_dot_left(tri_b, x, terms):
    acc = None
    rem = x
    for _ in range(terms):
        piece = rem.astype(BF16)
        part = _dot(tri_b, piece)
        acc = part if acc is None else acc + part
        rem = rem - piece.astype(F32)
    return acc


def _rwkv(z3, prm):
    b, s, _ = z3.shape
    chunk = RWKV_CHUNK
    nc = s // chunk
    wr = 3 * D_RWKV
    kern = functools.partial(_rwkv_kernel, chunk=chunk)
    rows8 = chunk // 8

    def vec(n):
        return pl.BlockSpec((1, n), lambda i, c: (0, 0))

    def full(a):
        return pl.BlockSpec(a.shape, lambda i, c: (0, 0))

    return pl.pallas_call(
        kern,
        grid=(b, nc),
        in_specs=[
            pl.BlockSpec((1, chunk, wr), lambda i, c: (i, c, COL_RKV // wr)),
            pl.BlockSpec((1, 8, wr), lambda i, c: (i, jnp.maximum(c * rows8 - 1, 0), COL_RKV // wr)),
            pl.BlockSpec((1, chunk, D_LORA_PACK), lambda i, c: (i, c, COL_LORA // D_LORA_PACK)),
            pl.BlockSpec((1, 8, D_LORA_PACK),
                         lambda i, c: (i, jnp.maximum(c * rows8 - 1, 0), COL_LORA // D_LORA_PACK)),
            vec(wr), vec(D_LORA_PACK),
            full(prm["w_up"]), vec(D_RWKV), full(prm["a_up"]), vec(D_RWKV), full(prm["g_up"]),
            vec(D_RWKV), vec(D_RWKV), vec(D_RWKV), vec(D_RWKV), vec(D_RWKV),
        ],
        out_specs=pl.BlockSpec((1, chunk, D_RWKV), lambda i, c: (i, c, 0)),
        out_shape=jax.ShapeDtypeStruct((b, s, D_RWKV), BF16),
        scratch_shapes=[
            pltpu.VMEM((D_RWKV // LANES, LANES, LANES), F32),
            pltpu.VMEM((chunk + 8, wr + D_LORA_PACK), F32),
            pltpu.VMEM((chunk, D_RWKV), F32),
        ],
        compiler_params=pltpu.CompilerParams(
            dimension_semantics=("parallel", "arbitrary"), vmem_limit_bytes=VMEM_LIMIT),
        name="rwkv7",
    )(z3, z3, z3, z3, prm["mu_r"], prm["mu_l"], prm["w_up"], prm["w0"], prm["a_up"], prm["a0"],
      prm["g_up"], prm["k_k"], prm["k_a"], prm["r_k"], prm["gn_w"], prm["gn_b"])


def _pad_cols(w, n):
    return jnp.pad(w, ((0, 0), (0, n - w.shape[1])))


def _pack_w_in(w_in):
    d_att3 = 3 * D_ATT
    o = d_att3 + 3 * D_RWKV
    parts = [
        w_in[:, :o],
        _pad_cols(w_in[:, o:o + RWKV_DECAY_LORA], LORA_PAD),
        _pad_cols(w_in[:, o + RWKV_DECAY_LORA:o + RWKV_DECAY_LORA + RWKV_AAA_LORA], LORA_PAD),
        w_in[:, o + RWKV_DECAY_LORA + RWKV_AAA_LORA:],
    ]
    return jnp.concatenate(parts, axis=1).astype(BF16)


def _pack_mu(mu):
    o = 3 * D_RWKV
    mu_l = jnp.concatenate([
        jnp.pad(mu[o:o + RWKV_DECAY_LORA], (0, LORA_PAD - RWKV_DECAY_LORA)),
        jnp.pad(mu[o + RWKV_DECAY_LORA:o + RWKV_DECAY_LORA + RWKV_AAA_LORA], (0, LORA_PAD - RWKV_AAA_LORA)),
        mu[o + RWKV_DECAY_LORA + RWKV_AAA_LORA:],
    ])
    return mu[None, :o], mu_l[None, :]


def _pad_rows(w, n):
    return jnp.pad(w, ((0, n - w.shape[0]), (0, 0)))


def _rope_tables(seq):
    inv_freq = ROPE_THETA ** (-jnp.arange(0, HEAD_DIM, 2, dtype=F32) / HEAD_DIM)
    ang = jnp.arange(seq, dtype=F32)[:, None] * inv_freq[None, :]
    cos, sin = jnp.cos(ang), jnp.sin(ang)
    reps = LANES // HEAD_DIM
    cos_t = jnp.tile(jnp.concatenate([cos, cos], axis=1), (1, reps))
    sin_t = jnp.tile(jnp.concatenate([-sin, sin], axis=1), (1, reps))
    return cos_t, sin_t


def _tile(n, pref):
    t = min(n, pref)
    assert n % t == 0
    return t


def kernel(x, ffn1_norm_pre, ffn1_norm_post, ffn1_w_gate, ffn1_w_up, ffn1_w_down, mix_norm_pre, mix_norm_post, w_in, w_out, rwkv_mu, rwkv_w_up, rwkv_w0, rwkv_a_up, rwkv_a0, rwkv_g_up, rwkv_k_k, rwkv_k_a, rwkv_r_k, rwkv_gn_w, rwkv_gn_b, pool_w, pool_scale, ffn2_norm_pre, ffn2_norm_post, ffn2_w_gate, ffn2_w_up, ffn2_w_down):
    b, s, d = x.shape
    depth = w_in.shape[0]
    t = b * s
    d_ff = ffn1_w_gate.shape[2]
    assert s % RWKV_CHUNK == 0 and d == D_ATT + D_RWKV + D_POOL
    tm = _tile(t, 512)
    tf = _tile(d_ff, 512)
    tn = _tile(D_PACK, 512)
    cos_t, sin_t = _rope_tables(s)
    x2 = x.reshape(t, d).astype(F32)

    for i in range(depth):
        x2 = _ffn(x2, ffn1_norm_pre[i][None], ffn1_norm_post[i][None], ffn1_w_gate[i].astype(BF16),
                  ffn1_w_up[i].astype(BF16), ffn1_w_down[i].astype(BF16), tm=tm, tf=tf)

        z = _proj(x2, mix_norm_pre[i][None], _pack_w_in(w_in[i]), tm=tm, tn=tn)
        z3 = z.reshape(b, s, D_PACK)
        y_att = _attention(z3, cos_t, sin_t)
        mu_r, mu_l = _pack_mu(rwkv_mu[i])
        prm = dict(
            mu_r=mu_r, mu_l=mu_l,
            w_up=_pad_rows(rwkv_w_up[i], LORA_PAD).astype(BF16), w0=rwkv_w0[i][None],
            a_up=_pad_rows(rwkv_a_up[i], LORA_PAD).astype(BF16), a0=rwkv_a0[i][None],
            g_up=rwkv_g_up[i].astype(BF16), k_k=rwkv_k_k[i][None], k_a=rwkv_k_a[i][None],
            r_k=rwkv_r_k[i].reshape(1, D_RWKV), gn_w=rwkv_gn_w[i][None], gn_b=rwkv_gn_b[i][None])
        y_rwkv = _rwkv(z3, prm)
        y_pool = _pool(z3, pool_w[i].astype(BF16), pool_scale[i][None])

        wo = w_out[i].astype(BF16)
        x2 = _mixout(x2, y_att.reshape(t, D_ATT), y_rwkv.reshape(t, D_RWKV), y_pool.reshape(t, D_POOL),
                     wo[:D_ATT], wo[D_ATT:D_ATT + D_RWKV], wo[D_ATT + D_RWKV:], mix_norm_post[i][None], tm=tm)

        x2 = _ffn(x2, ffn2_norm_pre[i][None], ffn2_norm_post[i][None], ffn2_w_gate[i].astype(BF16),
                  ffn2_w_up[i].astype(BF16), ffn2_w_down[i].astype(BF16), tm=tm, tf=tf)
    return x2.reshape(b, s, d).astype(x.dtype)
```

```python
import functools
import math

import jax
import jax.numpy as jnp
from jax import lax
from jax.experimental import pallas as pl
from jax.experimental.pallas import tpu as pltpu

F32 = jnp.float32
BF16 = jnp.bfloat16

HEAD_DIM = 64
N_ATT_HEADS = 12
N_RWKV_HEADS = 12
D_ATT = N_ATT_HEADS * HEAD_DIM
D_RWKV = N_RWKV_HEADS * HEAD_DIM
D_POOL = 512
N_POOL_GROUPS = 4
POOL_GROUP = D_POOL // N_POOL_GROUPS
POOL_WINDOWS = (2, 4, 8, 16)
DILATED_BRANCHES = ((128, 1), (512, 4), (2048, 16))
ROPE_THETA = 10000.0
RWKV_DECAY_LORA = 96
RWKV_AAA_LORA = 96
RWKV_GATE_LORA = 256
NORM_EPS = 1e-6
RWKV_GN_EPS = 64e-5

LANES = 128
LORA_PAD = 128
COL_ATT = 0
COL_RKV = 3 * D_ATT
COL_LORA = COL_RKV + 3 * D_RWKV
D_LORA_PACK = 2 * LORA_PAD + RWKV_GATE_LORA
COL_POOL = COL_LORA + D_LORA_PACK
D_PACK = COL_POOL + D_POOL

RWKV_ROWS = 2
RWKV_CHUNK = 64
ATT_KV_TILE = 512
VMEM_LIMIT = 56 * 1024 * 1024

NEG_BIG = -1e30


def _rms(xf, g):
    return xf * lax.rsqrt(jnp.mean(xf * xf, axis=-1, keepdims=True) + NORM_EPS) * g


def _dot(a, b):
    return jnp.dot(a, b, preferred_element_type=F32)


def _dot_nt(a, b):
    return lax.dot_general(a, b, (((1,), (1,)), ((), ())), preferred_element_type=F32)


def _ffn_kernel(x_ref, gpre_ref, gpost_ref, wg_ref, wu_ref, wd_ref, o_ref, h_ref, acc_ref):
    j = pl.program_id(1)

    @pl.when(j == 0)
    def _():
        h_ref[...] = _rms(x_ref[...], gpre_ref[...]).astype(BF16)
        acc_ref[...] = jnp.zeros_like(acc_ref)

    h = h_ref[...]
    g = _dot(h, wg_ref[...])
    u = _dot(h, wu_ref[...])
    a = (g * jax.nn.sigmoid(g) * u).astype(BF16)
    acc_ref[...] += _dot(a, wd_ref[...])

    @pl.when(j == pl.num_programs(1) - 1)
    def _():
        o_ref[...] = x_ref[...] + 0.5 * _rms(acc_ref[...], gpost_ref[...])


def _ffn(x, g_pre, g_post, wg, wu, wd, *, tm, tf):
    t, d = x.shape
    f = wg.shape[1]
    return pl.pallas_call(
        _ffn_kernel,
        grid=(t // tm, f // tf),
        in_specs=[
            pl.BlockSpec((tm, d), lambda i, j: (i, 0)),
            pl.BlockSpec((1, d), lambda i, j: (0, 0)),
            pl.BlockSpec((1, d), lambda i, j: (0, 0)),
            pl.BlockSpec((d, tf), lambda i, j: (0, j)),
            pl.BlockSpec((d, tf), lambda i, j: (0, j)),
            pl.BlockSpec((tf, d), lambda i, j: (j, 0)),
        ],
        out_specs=pl.BlockSpec((tm, d), lambda i, j: (i, 0)),
        out_shape=jax.ShapeDtypeStruct((t, d), F32),
        scratch_shapes=[pltpu.VMEM((tm, d), BF16), pltpu.VMEM((tm, d), F32)],
        compiler_params=pltpu.CompilerParams(
            dimension_semantics=("parallel", "arbitrary"), vmem_limit_bytes=VMEM_LIMIT),
        name="ffn",
    )(x, g_pre, g_post, wg, wu, wd)


def _proj_kernel(x_ref, g_ref, w_ref, o_ref, h_ref):
    @pl.when(pl.program_id(1) == 0)
    def _():
        h_ref[...] = _rms(x_ref[...], g_ref[...]).astype(BF16)

    o_ref[...] = _dot(h_ref[...], w_ref[...])


def _proj(x, g, w, *, tm, tn):
    t, d = x.shape
    n = w.shape[1]
    return pl.pallas_call(
        _proj_kernel,
        grid=(t // tm, n // tn),
        in_specs=[
            pl.BlockSpec((tm, d), lambda i, j: (i, 0)),
            pl.BlockSpec((1, d), lambda i, j: (0, 0)),
            pl.BlockSpec((d, tn), lambda i, j: (0, j)),
        ],
        out_specs=pl.BlockSpec((tm, tn), lambda i, j: (i, j)),
        out_shape=jax.ShapeDtypeStruct((t, n), F32),
        scratch_shapes=[pltpu.VMEM((tm, d), BF16)],
        compiler_params=pltpu.CompilerParams(
            dimension_semantics=("parallel", "arbitrary"), vmem_limit_bytes=VMEM_LIMIT),
        name="proj_in",
    )(x, g, w)


def _mixout_kernel(x_ref, ya_ref, yr_ref, yp_ref, wa_ref, wr_ref, wp_ref, g_ref, o_ref):
    mix = _dot(ya_ref[...], wa_ref[...]) + _dot(yr_ref[...], wr_ref[...]) + _dot(yp_ref[...], wp_ref[...])
    o_ref[...] = x_ref[...] + _rms(mix, g_ref[...])


def _mixout(x, ya, yr, yp, wa, wr, wp, g, *, tm):
    t, d = x.shape
    return pl.pallas_call(
        _mixout_kernel,
        grid=(t // tm,),
        in_specs=[
            pl.BlockSpec((tm, d), lambda i: (i, 0)),
            pl.BlockSpec((tm, ya.shape[1]), lambda i: (i, 0)),
            pl.BlockSpec((tm, yr.shape[1]), lambda i: (i, 0)),
            pl.BlockSpec((tm, yp.shape[1]), lambda i: (i, 0)),
            pl.BlockSpec(wa.shape, lambda i: (0, 0)),
            pl.BlockSpec(wr.shape, lambda i: (0, 0)),
            pl.BlockSpec(wp.shape, lambda i: (0, 0)),
            pl.BlockSpec((1, d), lambda i: (0, 0)),
        ],
        out_specs=pl.BlockSpec((tm, d), lambda i: (i, 0)),
        out_shape=jax.ShapeDtypeStruct((t, d), F32),
        compiler_params=pltpu.CompilerParams(
            dimension_semantics=("parallel",), vmem_limit_bytes=VMEM_LIMIT),
        name="proj_out",
    )(x, ya, yr, yp, wa, wr, wp, g)


def _att_bias_tile(base, tq, tk):
    row = lax.broadcasted_iota(jnp.int32, (tq, tk), 0)
    col = lax.broadcasted_iota(jnp.int32, (tq, tk), 1)
    delta = base + row - col
    cnt = jnp.zeros((tq, tk), jnp.int32)
    for window, dil in DILATED_BRANCHES:
        hit = (delta >= 0) & (delta <= window) & ((delta & (dil - 1)) == 0)
        cnt = cnt + hit.astype(jnp.int32)
    return jnp.where(cnt == 3, math.log(3.0),
                     jnp.where(cnt == 2, math.log(2.0), jnp.where(cnt == 1, 0.0, NEG_BIG))).astype(F32)


def _att_kernel(q_ref, k_ref, v_ref, cos_ref, sin_ref, o_ref,
                qa_s, qb_s, k_s, va_s, vb_s, bias_s, *, seq, tq, tk):
    nq = seq // tq
    ratio = tk // tq

    @pl.when(pl.program_id(1) == 0)
    def _():
        for off in range(nq):
            bias_s[off] = _att_bias_tile(off * tq, tq, tk)

    lane = lax.broadcasted_iota(jnp.int32, (tq, LANES), 1)
    first_half = (lane & (HEAD_DIM - 1)) < (HEAD_DIM // 2)
    head_a = lane < HEAD_DIM

    def rope(x, cos, sin):
        swapped = jnp.where(first_half, pltpu.roll(x, LANES - HEAD_DIM // 2, 1), pltpu.roll(x, HEAD_DIM // 2, 1))
        return x * cos + swapped * sin

    def prep(i, carry):
        rows = pl.ds(pl.multiple_of(i * tq, tq), tq)
        cos = cos_ref[rows, :]
        sin = sin_ref[rows, :]
        q = rope(q_ref[0, rows, :], cos, sin) * (HEAD_DIM ** -0.5)
        k = rope(k_ref[0, rows, :], cos, sin)
        v = v_ref[0, rows, :]
        qa_s[rows, :] = jnp.where(head_a, q, 0.0).astype(BF16)
        qb_s[rows, :] = jnp.where(head_a, 0.0, q).astype(BF16)
        k_s[rows, :] = k.astype(BF16)
        va_s[rows, :] = jnp.where(head_a, v, 1.0).astype(BF16)
        vb_s[rows, :] = jnp.where(head_a, 1.0, v).astype(BF16)
        return carry

    lax.fori_loop(0, nq, prep, 0)

    def tile_max(s):
        m = s[:, 0:LANES]
        for t in range(1, tk // LANES):
            m = jnp.maximum(m, s[:, t * LANES:(t + 1) * LANES])
        return m

    def q_block(qi, carry):
        qrows = pl.ds(pl.multiple_of(qi * tq, tq), tq)
        qa = qa_s[qrows, :]
        qb = qb_s[qrows, :]
        nkv = qi // ratio + 1

        def scores(kj):
            krows = pl.ds(pl.multiple_of(kj * tk, tk), tk)
            kt = k_s[krows, :]
            bias = bias_s[qi - ratio * kj]
            return _dot_nt(qa, kt) + bias, _dot_nt(qb, kt) + bias, krows

        def max_step(kj, c):
            s_a, s_b, _ = scores(kj)
            return jnp.maximum(c[0], tile_max(s_a)), jnp.maximum(c[1], tile_max(s_b))

        neg = jnp.full((tq, LANES), NEG_BIG, F32)
        mt_a, mt_b = lax.fori_loop(0, nkv, max_step, (neg, neg))
        m_a = jnp.max(mt_a, axis=-1, keepdims=True)
        m_b = jnp.max(mt_b, axis=-1, keepdims=True)

        def acc_step(kj, c):
            s_a, s_b, krows = scores(kj)
            p_a = jnp.exp(s_a - m_a).astype(BF16)
            p_b = jnp.exp(s_b - m_b).astype(BF16)
            return c[0] + _dot(p_a, va_s[krows, :]), c[1] + _dot(p_b, vb_s[krows, :])

        zero = jnp.zeros((tq, LANES), F32)
        acc_a, acc_b = lax.fori_loop(0, nkv, acc_step, (zero, zero))
        o_a = acc_a / pltpu.roll(acc_a, HEAD_DIM, 1)
        o_b = acc_b / pltpu.roll(acc_b, HEAD_DIM, 1)
        o_ref[0, qrows, :] = jnp.where(head_a, o_a, o_b).astype(o_ref.dtype)
        return carry

    lax.fori_loop(0, nq, q_block, 0)


def _attention(z3, cos_t, sin_t):
    b, s, _ = z3.shape
    tk = min(ATT_KV_TILE, s)
    tq = tk // 2
    npair = D_ATT // LANES
    kern = functools.partial(_att_kernel, seq=s, tq=tq, tk=tk)
    return pl.pallas_call(
        kern,
        grid=(b, npair),
        in_specs=[
            pl.BlockSpec((1, s, LANES), lambda i, p: (i, 0, COL_ATT // LANES + p)),
            pl.BlockSpec((1, s, LANES), lambda i, p: (i, 0, (COL_ATT + D_ATT) // LANES + p)),
            pl.BlockSpec((1, s, LANES), lambda i, p: (i, 0, (COL_ATT + 2 * D_ATT) // LANES + p)),
            pl.BlockSpec((s, LANES), lambda i, p: (0, 0)),
            pl.BlockSpec((s, LANES), lambda i, p: (0, 0)),
        ],
        out_specs=pl.BlockSpec((1, s, LANES), lambda i, p: (i, 0, p)),
        out_shape=jax.ShapeDtypeStruct((b, s, D_ATT), BF16),
        scratch_shapes=[pltpu.VMEM((s, LANES), BF16)] * 5 + [pltpu.VMEM((s // tq, tq, tk), F32)],
        compiler_params=pltpu.CompilerParams(
            dimension_semantics=("parallel", "arbitrary"), vmem_limit_bytes=VMEM_LIMIT),
        name="dilated_attention",
    )(z3, z3, z3, cos_t, sin_t)


POOL_PAD = 16


def _pool_kernel(u_ref, w_ref, sc_ref, o_ref, pad_s, *, seq):
    t1 = (lax.broadcasted_iota(jnp.int32, (seq, POOL_GROUP), 0) + 1).astype(F32)
    pad_s[0:POOL_PAD, :] = jnp.zeros((POOL_PAD, POOL_GROUP), F32)
    for gi, w in enumerate(POOL_WINDOWS):
        cols = slice(gi * POOL_GROUP, (gi + 1) * POOL_GROUP)
        u = u_ref[0, :, cols]
        s = u
        k = 1
        while k < w:
            pad_s[POOL_PAD:POOL_PAD + seq, :] = s
            s = s + pad_s[POOL_PAD - k:POOL_PAD - k + seq, :]
            k *= 2
        p = s / jnp.minimum(t1, float(w)) - u
        y = _dot(p.astype(BF16), w_ref[gi])
        o_ref[0, :, cols] = (y * sc_ref[:, cols]).astype(o_ref.dtype)


def _pool(z3, pool_w, pool_scale):
    b, s, _ = z3.shape
    kern = functools.partial(_pool_kernel, seq=s)
    return pl.pallas_call(
        kern,
        grid=(b,),
        in_specs=[
            pl.BlockSpec((1, s, D_POOL), lambda i: (i, 0, COL_POOL // D_POOL)),
            pl.BlockSpec(pool_w.shape, lambda i: (0, 0, 0)),
            pl.BlockSpec((1, D_POOL), lambda i: (0, 0)),
        ],
        out_specs=pl.BlockSpec((1, s, D_POOL), lambda i: (i, 0, 0)),
        out_shape=jax.ShapeDtypeStruct((b, s, D_POOL), BF16),
        scratch_shapes=[pltpu.VMEM((s + POOL_PAD, POOL_GROUP), F32)],
        compiler_params=pltpu.CompilerParams(
            dimension_semantics=("parallel",), vmem_limit_bytes=VMEM_LIMIT),
        name="pool_mixer",
    )(z3, pool_w, pool_scale)


def _split_dot(x, ones_b, terms):
    acc = None
    rem = x
    for _ in range(terms):
        piece = rem.astype(BF16)
        part = _dot(piece, ones_b)
        acc = part if acc is None else acc + part
        rem = rem - piece.astype(F32)
    return acc


def _rwkv_kernel(zr_ref, zrp_ref, zl_ref, zlp_ref, mur_ref, mul_ref, wup_ref, w0_ref, aup_ref, a0_ref,
                 gup_ref, kk_ref, ka_ref, rk_ref, gnw_ref, gnb_ref, o_ref, h_s, sh_s, *, chunk, rows):
    c = pl.program_id(1)
    npair = D_RWKV // LANES
    halo = 8
    wr = 3 * D_RWKV

    @pl.when(c == 0)
    def _():
        h_s[...] = jnp.zeros_like(h_s)

    row = lax.broadcasted_iota(jnp.int32, (LANES, LANES), 0)
    col = lax.broadcasted_iota(jnp.int32, (LANES, LANES), 1)
    head_ones = ((row < HEAD_DIM) == (col < HEAD_DIM)).astype(BF16)
    trow = lax.broadcasted_iota(jnp.int32, (chunk, chunk), 0)
    tcol = lax.broadcasted_iota(jnp.int32, (chunk, chunk), 1)
    tri = (trow >= tcol).astype(BF16)
    lane = lax.broadcasted_iota(jnp.int32, (chunk, LANES), 1)
    head_a = lane < HEAD_DIM
    strict_lower = row > col
    lower = row >= col
    eye = (row == col).astype(F32)
    first = c == 0

    def seg_sum(x):
        return jnp.concatenate(
            [_split_dot(x[:, p * LANES:(p + 1) * LANES], head_ones, 2) for p in range(npair)], axis=1)

    def stack2(x):
        return jnp.concatenate([jnp.where(head_a, x, 0.0), jnp.where(head_a, 0.0, x)], axis=0)

    tok = []
    for bi in range(rows):
        zr = zr_ref[bi]
        zl = zl_ref[bi]
        sh_s[bi, 0:halo, 0:wr] = jnp.where(first, 0.0, zrp_ref[bi])
        sh_s[bi, 0:halo, wr:wr + D_LORA_PACK] = jnp.where(first, 0.0, zlp_ref[bi])
        sh_s[bi, halo:halo + chunk, 0:wr] = zr
        sh_s[bi, halo:halo + chunk, wr:wr + D_LORA_PACK] = zl
        zr_prev = sh_s[bi, halo - 1:halo - 1 + chunk, 0:wr]
        zl_prev = sh_s[bi, halo - 1:halo - 1 + chunk, wr:wr + D_LORA_PACK]
        zr = zr + (zr_prev - zr) * mur_ref[...]
        zl = zl + (zl_prev - zl) * mul_ref[...]
        r = zr[:, 0:D_RWKV]
        k = zr[:, D_RWKV:2 * D_RWKV]
        v = zr[:, 2 * D_RWKV:3 * D_RWKV]
        zw = zl[:, 0:LORA_PAD]
        za = zl[:, LORA_PAD:2 * LORA_PAD]
        zg = zl[:, 2 * LORA_PAD:D_LORA_PACK]

        lw = w0_ref[...] + _dot(jnp.tanh(zw).astype(BF16), wup_ref[...])
        nlw = -lw
        softplus = jnp.maximum(nlw, 0.0) + jnp.log(1.0 + jnp.exp(-jnp.abs(nlw)))
        ld = -jnp.exp(-softplus - 0.5)
        a = jax.nn.sigmoid(a0_ref[...] + _dot(za.astype(BF16), aup_ref[...]))
        g = _dot(jax.nn.sigmoid(zg).astype(BF16), gup_ref[...])
        kk = k * kk_ref[...]
        k2 = k * (1.0 + (a - 1.0) * ka_ref[...])
        kk = kk / jnp.maximum(jnp.sqrt(seg_sum(kk * kk)), 1e-12)
        bonus = seg_sum(r * k2 * rk_ref[...]) * v
        bb = kk * a

        cum = _split_dot_left(tri, ld, 3)
        cum_end = cum[chunk - 1:chunk, :]
        e_inv = jnp.exp(-cum)
        e_end = jnp.exp(cum_end - cum)
        tok.append(dict(
            r_t=r * jnp.exp(cum), kk_t=kk * jnp.exp(cum - ld), b_h=bb * e_inv, k_h=k2 * e_inv,
            b_a=bb * e_end, k_a=k2 * e_end, a_end=jnp.exp(cum_end), v=v, bonus=bonus, g=g))

    units = [(bi, p) for bi in range(rows) for p in range(npair)]

    def pair(name):
        return [tok[bi][name][:, p * LANES:(p + 1) * LANES] for bi, p in units]

    kks = [stack2(x).astype(BF16) for x in pair("kk_t")]
    bs = [stack2(x).astype(BF16) for x in pair("b_h")]
    ks = [stack2(x).astype(BF16) for x in pair("k_h")]
    rs = [stack2(x).astype(BF16) for x in pair("r_t")]
    vs = [stack2(x).astype(BF16) for x in pair("v")]
    bat = [stack2(x).T.astype(BF16) for x in pair("b_a")]
    kat = [stack2(x).T.astype(BF16) for x in pair("k_a")]
    a_col = [jnp.broadcast_to(x, (LANES, LANES)).T for x in pair("a_end")]
    ids = range(len(units))

    l_b = [jnp.where(strict_lower, _dot_nt(kks[i], bs[i]), 0.0) for i in ids]
    l_k = [jnp.where(strict_lower, _dot_nt(kks[i], ks[i]), 0.0).astype(BF16) for i in ids]
    m_b = [jnp.where(lower, _dot_nt(rs[i], bs[i]), 0.0).astype(BF16) for i in ids]
    m_k = [jnp.where(lower, _dot_nt(rs[i], ks[i]), 0.0).astype(BF16) for i in ids]

    sel = ((row ^ col) < 2) & ((row & 1) != 0) & ((col & 1) == 0)
    t_inv = [eye - jnp.where(sel, l_b[i], 0.0) for i in ids]
    s = 2
    while s < HEAD_DIM:
        sel = ((row ^ col) < 2 * s) & ((row & s) != 0) & ((col & s) == 0)
        t_b = [t_inv[i].astype(BF16) for i in ids]
        inner = [_dot(jnp.where(sel, l_b[i], 0.0).astype(BF16), t_b[i]).astype(BF16) for i in ids]
        t_inv = [t_inv[i] - _dot(t_b[i], inner[i]) for i in ids]
        s *= 2
    t_b = [t_inv[i].astype(BF16) for i in ids]

    lkv = [_dot(l_k[i], vs[i]).astype(BF16) for i in ids]
    w_m = [_dot(t_b[i], kks[i]).astype(BF16) for i in ids]
    u_t = [_dot(t_b[i], lkv[i]) for i in ids]
    y0 = [_dot(m_k[i], vs[i]) for i in ids]
    g_m = [_dot(kat[i], vs[i]) for i in ids]

    h = [h_s[i] for i in ids]
    h_b = [h[i].astype(BF16) for i in ids]
    u_b = [(-(_dot(w_m[i], h_b[i]) + u_t[i])).astype(BF16) for i in ids]
    y = [_dot(rs[i], h_b[i]) + _dot(m_b[i], u_b[i]) + y0[i] for i in ids]
    for i in ids:
        h_s[i] = a_col[i] * h[i] + _dot(bat[i], u_b[i]) + g_m[i]

    for bi in range(rows):
        yb = jnp.concatenate(
            [y[bi * npair + p][0:chunk, :] + y[bi * npair + p][chunk:2 * chunk, :] for p in range(npair)], axis=1)
        mean = seg_sum(yb) * (1.0 / HEAD_DIM)
        d = yb - mean
        var = seg_sum(d * d) * (1.0 / HEAD_DIM)
        yn = d * lax.rsqrt(var + RWKV_GN_EPS) * gnw_ref[...] + gnb_ref[...]
        o_ref[bi] = ((yn + tok[bi]["bonus"]) * tok[bi]["g"]).astype(o_ref.dtype)


def _split_dot_left(tri_b, x, terms):
    acc = None
    rem = x
    for _ in range(terms):
        piece = rem.astype(BF16)
        part = _dot(tri_b, piece)
        acc = part if acc is None else acc + part
        rem = rem - piece.astype(F32)
    return acc


def _rwkv(z3, prm, *, rows):
    b, s, _ = z3.shape
    chunk = RWKV_CHUNK
    nc = s // chunk
    wr = 3 * D_RWKV
    kern = functools.partial(_rwkv_kernel, chunk=chunk, rows=rows)
    rows8 = chunk // 8
    npair = D_RWKV // LANES

    def vec(n):
        return pl.BlockSpec((1, n), lambda i, c: (0, 0))

    def full(a):
        return pl.BlockSpec(a.shape, lambda i, c: (0, 0))

    return pl.pallas_call(
        kern,
        grid=(b // rows, nc),
        in_specs=[
            pl.BlockSpec((rows, chunk, wr), lambda i, c: (i, c, COL_RKV // wr)),
            pl.BlockSpec((rows, 8, wr), lambda i, c: (i, jnp.maximum(c * rows8 - 1, 0), COL_RKV // wr)),
            pl.BlockSpec((rows, chunk, D_LORA_PACK), lambda i, c: (i, c, COL_LORA // D_LORA_PACK)),
            pl.BlockSpec((rows, 8, D_LORA_PACK),
                         lambda i, c: (i, jnp.maximum(c * rows8 - 1, 0), COL_LORA // D_LORA_PACK)),
            vec(wr), vec(D_LORA_PACK),
            full(prm["w_up"]), vec(D_RWKV), full(prm["a_up"]), vec(D_RWKV), full(prm["g_up"]),
            vec(D_RWKV), vec(D_RWKV), vec(D_RWKV), vec(D_RWKV), vec(D_RWKV),
        ],
        out_specs=pl.BlockSpec((rows, chunk, D_RWKV), lambda i, c: (i, c, 0)),
        out_shape=jax.ShapeDtypeStruct((b, s, D_RWKV), BF16),
        scratch_shapes=[
            pltpu.VMEM((rows * npair, LANES, LANES), F32),
            pltpu.VMEM((rows, chunk + 8, wr + D_LORA_PACK), F32),
        ],
        compiler_params=pltpu.CompilerParams(
            dimension_semantics=("parallel", "arbitrary"), vmem_limit_bytes=VMEM_LIMIT),
        name="rwkv7",
    )(z3, z3, z3, z3, prm["mu_r"], prm["mu_l"], prm["w_up"], prm["w0"], prm["a_up"], prm["a0"],
      prm["g_up"], prm["k_k"], prm["k_a"], prm["r_k"], prm["gn_w"], prm["gn_b"])


def _pad_cols(w, n):
    return jnp.pad(w, ((0, 0), (0, n - w.shape[1])))


def _pack_w_in(w_in):
    d_att3 = 3 * D_ATT
    o = d_att3 + 3 * D_RWKV
    parts = [
        w_in[:, :o],
        _pad_cols(w_in[:, o:o + RWKV_DECAY_LORA], LORA_PAD),
        _pad_cols(w_in[:, o + RWKV_DECAY_LORA:o + RWKV_DECAY_LORA + RWKV_AAA_LORA], LORA_PAD),
        w_in[:, o + RWKV_DECAY_LORA + RWKV_AAA_LORA:],
    ]
    return jnp.concatenate(parts, axis=1).astype(BF16)


def _pack_mu(mu):
    o = 3 * D_RWKV
    mu_l = jnp.concatenate([
        jnp.pad(mu[o:o + RWKV_DECAY_LORA], (0, LORA_PAD - RWKV_DECAY_LORA)),
        jnp.pad(mu[o + RWKV_DECAY_LORA:o + RWKV_DECAY_LORA + RWKV_AAA_LORA], (0, LORA_PAD - RWKV_AAA_LORA)),
        mu[o + RWKV_DECAY_LORA + RWKV_AAA_LORA:],
    ])
    return mu[None, :o], mu_l[None, :]


def _pad_rows(w, n):
    return jnp.pad(w, ((0, n - w.shape[0]), (0, 0)))


def _rope_tables(seq):
    inv_freq = ROPE_THETA ** (-jnp.arange(0, HEAD_DIM, 2, dtype=F32) / HEAD_DIM)
    ang = jnp.arange(seq, dtype=F32)[:, None] * inv_freq[None, :]
    cos, sin = jnp.cos(ang), jnp.sin(ang)
    reps = LANES // HEAD_DIM
    cos_t = jnp.tile(jnp.concatenate([cos, cos], axis=1), (1, reps))
    sin_t = jnp.tile(jnp.concatenate([-sin, sin], axis=1), (1, reps))
    return cos_t, sin_t


def _tile(n, pref):
    t = min(n, pref)
    assert n % t == 0
    return t


def kernel(x, ffn1_norm_pre, ffn1_norm_post, ffn1_w_gate, ffn1_w_up, ffn1_w_down, mix_norm_pre, mix_norm_post, w_in, w_out, rwkv_mu, rwkv_w_up, rwkv_w0, rwkv_a_up, rwkv_a0, rwkv_g_up, rwkv_k_k, rwkv_k_a, rwkv_r_k, rwkv_gn_w, rwkv_gn_b, pool_w, pool_scale, ffn2_norm_pre, ffn2_norm_post, ffn2_w_gate, ffn2_w_up, ffn2_w_down):
    b, s, d = x.shape
    depth = w_in.shape[0]
    t = b * s
    d_ff = ffn1_w_gate.shape[2]
    assert s % RWKV_CHUNK == 0 and b % RWKV_ROWS == 0 and d == D_ATT + D_RWKV + D_POOL
    tm = _tile(t, 512)
    tf = _tile(d_ff, 512)
    tn = _tile(D_PACK, 512)
    cos_t, sin_t = _rope_tables(s)
    x2 = x.reshape(t, d).astype(F32)

    for i in range(depth):
        x2 = _ffn(x2, ffn1_norm_pre[i][None], ffn1_norm_post[i][None], ffn1_w_gate[i].astype(BF16),
                  ffn1_w_up[i].astype(BF16), ffn1_w_down[i].astype(BF16), tm=tm, tf=tf)

        z = _proj(x2, mix_norm_pre[i][None], _pack_w_in(w_in[i]), tm=tm, tn=tn)
        z3 = z.reshape(b, s, D_PACK)
        y_att = _attention(z3, cos_t, sin_t)
        mu_r, mu_l = _pack_mu(rwkv_mu[i])
        prm = dict(
            mu_r=mu_r, mu_l=mu_l,
            w_up=_pad_rows(rwkv_w_up[i], LORA_PAD).astype(BF16), w0=rwkv_w0[i][None],
            a_up=_pad_rows(rwkv_a_up[i], LORA_PAD).astype(BF16), a0=rwkv_a0[i][None],
            g_up=rwkv_g_up[i].astype(BF16), k_k=rwkv_k_k[i][None], k_a=rwkv_k_a[i][None],
            r_k=rwkv_r_k[i].reshape(1, D_RWKV), gn_w=rwkv_gn_w[i][None], gn_b=rwkv_gn_b[i][None])
        y_rwkv = _rwkv(z3, prm, rows=RWKV_ROWS)
        y_pool = _pool(z3, pool_w[i].astype(BF16), pool_scale[i][None])

        wo = w_out[i].astype(BF16)
        x2 = _mixout(x2, y_att.reshape(t, D_ATT), y_rwkv.reshape(t, D_RWKV), y_pool.reshape(t, D_POOL),
                     wo[:D_ATT], wo[D_ATT:D_ATT + D_RWKV], wo[D_ATT + D_RWKV:], mix_norm_post[i][None], tm=tm)

        x2 = _ffn(x2, ffn2_norm_pre[i][None], ffn2_norm_post[i][None], ffn2_w_gate[i].astype(BF16),
                  ffn2_w_up[i].astype(BF16), ffn2_w_down[i].astype(BF16), tm=tm, tf=tf)
    return x2.reshape(b, s, d).astype(x.dtype)
```

```python
import functools
import math

import jax
import jax.numpy as jnp
from jax import lax
from jax.experimental import pallas as pl
from jax.experimental.pallas import tpu as pltpu

F32 = jnp.float32
BF16 = jnp.bfloat16

HEAD_DIM = 64
N_ATT_HEADS = 12
N_RWKV_HEADS = 12
D_ATT = N_ATT_HEADS * HEAD_DIM
D_RWKV = N_RWKV_HEADS * HEAD_DIM
D_POOL = 512
N_POOL_GROUPS = 4
POOL_GROUP = D_POOL // N_POOL_GROUPS
POOL_WINDOWS = (2, 4, 8, 16)
DILATED_BRANCHES = ((128, 1), (512, 4), (2048, 16))
ROPE_THETA = 10000.0
RWKV_DECAY_LORA = 96
RWKV_AAA_LORA = 96
RWKV_GATE_LORA = 256
NORM_EPS = 1e-6
RWKV_GN_EPS = 64e-5

LANES = 128
LORA_PAD = 128
COL_ATT = 0
COL_RKV = 3 * D_ATT
COL_LORA = COL_RKV + 3 * D_RWKV
D_LORA_PACK = 2 * LORA_PAD + RWKV_GATE_LORA
COL_POOL = COL_LORA + D_LORA_PACK
D_PACK = COL_POOL + D_POOL

RWKV_ROWS = 2
RWKV_CHUNK = 64
ATT_KV_TILE = 512
ATT_GROUP = 2
VMEM_LIMIT = 56 * 1024 * 1024

NEG_BIG = -1e30
LOG2_E = math.log2(math.e)


def _rms(xf, g):
    return xf * lax.rsqrt(jnp.mean(xf * xf, axis=-1, keepdims=True) + NORM_EPS) * g


def _dot(a, b):
    return jnp.dot(a, b, preferred_element_type=F32)


def _dot_nt(a, b):
    return lax.dot_general(a, b, (((1,), (1,)), ((), ())), preferred_element_type=F32)


def _ffn_kernel(x_ref, gpre_ref, gpost_ref, wg_ref, wu_ref, wd_ref, o_ref, h_ref, acc_ref):
    j = pl.program_id(1)

    @pl.when(j == 0)
    def _():
        h_ref[...] = _rms(x_ref[...], gpre_ref[...]).astype(BF16)
        acc_ref[...] = jnp.zeros_like(acc_ref)

    h = h_ref[...]
    g = _dot(h, wg_ref[...])
    u = _dot(h, wu_ref[...])
    a = (g * jax.nn.sigmoid(g) * u).astype(BF16)
    acc_ref[...] += _dot(a, wd_ref[...])

    @pl.when(j == pl.num_programs(1) - 1)
    def _():
        o_ref[...] = x_ref[...] + 0.5 * _rms(acc_ref[...], gpost_ref[...])


def _ffn(x, g_pre, g_post, wg, wu, wd, *, layer, tm, tf):
    t, d = x.shape
    f = wg.shape[2]
    return pl.pallas_call(
        _ffn_kernel,
        grid=(t // tm, f // tf),
        in_specs=[
            pl.BlockSpec((tm, d), lambda i, j: (i, 0)),
            pl.BlockSpec((None, 1, d), lambda i, j: (layer, 0, 0)),
            pl.BlockSpec((None, 1, d), lambda i, j: (layer, 0, 0)),
            pl.BlockSpec((None, d, tf), lambda i, j: (layer, 0, j)),
            pl.BlockSpec((None, d, tf), lambda i, j: (layer, 0, j)),
            pl.BlockSpec((None, tf, d), lambda i, j: (layer, j, 0)),
        ],
        out_specs=pl.BlockSpec((tm, d), lambda i, j: (i, 0)),
        out_shape=jax.ShapeDtypeStruct((t, d), F32),
        scratch_shapes=[pltpu.VMEM((tm, d), BF16), pltpu.VMEM((tm, d), F32)],
        compiler_params=pltpu.CompilerParams(
            dimension_semantics=("parallel", "arbitrary"), vmem_limit_bytes=VMEM_LIMIT),
        name="ffn",
    )(x, g_pre, g_post, wg, wu, wd)


def _proj_kernel(x_ref, g_ref, w_ref, o_ref, h_ref):
    @pl.when(pl.program_id(1) == 0)
    def _():
        h_ref[...] = _rms(x_ref[...], g_ref[...]).astype(BF16)

    o_ref[...] = _dot(h_ref[...], w_ref[...])


def _proj(x, g, w, *, layer, tm, tn):
    t, d = x.shape
    n = w.shape[2]
    return pl.pallas_call(
        _proj_kernel,
        grid=(t // tm, n // tn),
        in_specs=[
            pl.BlockSpec((tm, d), lambda i, j: (i, 0)),
            pl.BlockSpec((None, 1, d), lambda i, j: (layer, 0, 0)),
            pl.BlockSpec((None, d, tn), lambda i, j: (layer, 0, j)),
        ],
        out_specs=pl.BlockSpec((tm, tn), lambda i, j: (i, j)),
        out_shape=jax.ShapeDtypeStruct((t, n), F32),
        scratch_shapes=[pltpu.VMEM((tm, d), BF16)],
        compiler_params=pltpu.CompilerParams(
            dimension_semantics=("parallel", "arbitrary"), vmem_limit_bytes=VMEM_LIMIT),
        name="proj_in",
    )(x, g, w)


def _mixout_kernel(x_ref, ya_ref, yr_ref, yp_ref, wa_ref, wr_ref, wp_ref, g_ref, o_ref):
    mix = _dot(ya_ref[...], wa_ref[...]) + _dot(yr_ref[...], wr_ref[...]) + _dot(yp_ref[...], wp_ref[...])
    o_ref[...] = x_ref[...] + _rms(mix, g_ref[...])


def _mixout(x, ya, yr, yp, wo, g, *, layer, tm):
    t, d = x.shape
    da, dr, dp = ya.shape[1], yr.shape[1], yp.shape[1]
    assert da == dr and (da + dr) % dp == 0
    return pl.pallas_call(
        _mixout_kernel,
        grid=(t // tm,),
        in_specs=[
            pl.BlockSpec((tm, d), lambda i: (i, 0)),
            pl.BlockSpec((tm, da), lambda i: (i, 0)),
            pl.BlockSpec((tm, dr), lambda i: (i, 0)),
            pl.BlockSpec((tm, dp), lambda i: (i, 0)),
            pl.BlockSpec((None, da, d), lambda i: (layer, 0, 0)),
            pl.BlockSpec((None, dr, d), lambda i: (layer, 1, 0)),
            pl.BlockSpec((None, dp, d), lambda i: (layer, (da + dr) // dp, 0)),
            pl.BlockSpec((None, 1, d), lambda i: (layer, 0, 0)),
        ],
        out_specs=pl.BlockSpec((tm, d), lambda i: (i, 0)),
        out_shape=jax.ShapeDtypeStruct((t, d), F32),
        compiler_params=pltpu.CompilerParams(
            dimension_semantics=("parallel",), vmem_limit_bytes=VMEM_LIMIT),
        name="proj_out",
    )(x, ya, yr, yp, wo, wo, wo, g)


def _cast_kernel(w_ref, o_ref):
    o_ref[...] = w_ref[...].astype(BF16)


def _cast_bf16(w, *, tr):
    nl, r, c = w.shape
    return pl.pallas_call(
        _cast_kernel,
        grid=(nl, r // tr),
        in_specs=[pl.BlockSpec((1, tr, c), lambda i, j: (i, j, 0))],
        out_specs=pl.BlockSpec((1, tr, c), lambda i, j: (i, j, 0)),
        out_shape=jax.ShapeDtypeStruct(w.shape, BF16),
        compiler_params=pltpu.CompilerParams(
            dimension_semantics=("parallel", "parallel"), vmem_limit_bytes=VMEM_LIMIT),
        name="cast_bf16",
    )(w)


def _att_bias_tile(base, tq, tk):
    row = lax.broadcasted_iota(jnp.int32, (tq, tk), 0)
    col = lax.broadcasted_iota(jnp.int32, (tq, tk), 1)
    delta = base + row - col
    cnt = jnp.zeros((tq, tk), jnp.int32)
    for window, dil in DILATED_BRANCHES:
        hit = (delta >= 0) & (delta <= window) & ((delta & (dil - 1)) == 0)
        cnt = cnt + hit.astype(jnp.int32)
    return jnp.where(cnt == 3, math.log2(3.0),
                     jnp.where(cnt == 2, 1.0, jnp.where(cnt == 1, 0.0, NEG_BIG))).astype(F32)


def _att_kernel(q_ref, k_ref, v_ref, cos_ref, sin_ref, o_ref,
                qa_s, qb_s, k_s, va_s, vb_s, bias_s, *, seq, tq, tk, group):
    nq = seq // tq
    ratio = tk // tq

    @pl.when(pl.program_id(1) == 0)
    def _():
        for off in range(nq):
            bias_s[off] = _att_bias_tile(off * tq, tq, tk)

    lane = lax.broadcasted_iota(jnp.int32, (tq, LANES), 1)
    first_half = (lane & (HEAD_DIM - 1)) < (HEAD_DIM // 2)
    head_a = lane < HEAD_DIM

    def rope(x, cos, sin):
        swapped = jnp.where(first_half, pltpu.roll(x, LANES - HEAD_DIM // 2, 1), pltpu.roll(x, HEAD_DIM // 2, 1))
        return x * cos + swapped * sin

    def prep(i, carry):
        rows = pl.ds(pl.multiple_of(i * tq, tq), tq)
        cos = cos_ref[rows, :]
        sin = sin_ref[rows, :]
        q = rope(q_ref[0, rows, :], cos, sin) * (HEAD_DIM ** -0.5 * LOG2_E)
        k = rope(k_ref[0, rows, :], cos, sin)
        v = v_ref[0, rows, :]
        qa_s[rows, :] = jnp.where(head_a, q, 0.0).astype(BF16)
        qb_s[rows, :] = jnp.where(head_a, 0.0, q).astype(BF16)
        k_s[rows, :] = k.astype(BF16)
        va_s[rows, :] = jnp.where(head_a, v, 1.0).astype(BF16)
        vb_s[rows, :] = jnp.where(head_a, 1.0, v).astype(BF16)
        return carry

    lax.fori_loop(0, nq, prep, 0)

    def tile_max(s):
        m = s[:, 0:LANES]
        for t in range(1, s.shape[1] // LANES):
            m = jnp.maximum(m, s[:, t * LANES:(t + 1) * LANES])
        return m

    q_s = (qa_s, qb_s)
    v_s = (va_s, vb_s)
    m_run = {}
    acc = {}
    for kj in range(seq // tk):
        blocks = list(range(ratio * kj, nq))
        for g0 in range(0, len(blocks), group):
            chains = [(qi, hd) for qi in blocks[g0:g0 + group] for hd in range(2)]
            width = {qi: (tq if qi == ratio * kj else tk) for qi, _ in chains}
            s = {}
            for qi, hd in chains:
                krows = slice(kj * tk, kj * tk + width[qi])
                s[qi, hd] = (_dot_nt(q_s[hd][qi * tq:(qi + 1) * tq, :], k_s[krows, :])
                             + bias_s[qi - ratio * kj, :, 0:width[qi]])
            p = {}
            for ch in chains:
                m_tile = jnp.max(tile_max(s[ch]), axis=-1, keepdims=True)
                if kj == 0:
                    m_new = m_tile
                else:
                    m_new = jnp.maximum(m_run[ch], m_tile)
                    acc[ch] = jnp.exp2(m_run[ch] - m_new) * acc[ch]
                m_run[ch] = m_new
                p[ch] = jnp.exp2(s[ch] - m_new).astype(BF16)
            for ch in chains:
                pv = _dot(p[ch], v_s[ch[1]][kj * tk:kj * tk + width[ch[0]], :])
                acc[ch] = pv if kj == 0 else acc[ch] + pv

    for qi in range(nq):
        o_a = acc[qi, 0] / pltpu.roll(acc[qi, 0], HEAD_DIM, 1)
        o_b = acc[qi, 1] / pltpu.roll(acc[qi, 1], HEAD_DIM, 1)
        o_ref[0, qi * tq:(qi + 1) * tq, :] = jnp.where(head_a, o_a, o_b).astype(o_ref.dtype)


def _attention(z3, cos_t, sin_t):
    b, s, _ = z3.shape
    tk = min(ATT_KV_TILE, s)
    tq = tk // 2
    npair = D_ATT // LANES
    kern = functools.partial(_att_kernel, seq=s, tq=tq, tk=tk, group=ATT_GROUP)
    return pl.pallas_call(
        kern,
        grid=(b, npair),
        in_specs=[
            pl.BlockSpec((1, s, LANES), lambda i, p: (i, 0, COL_ATT // LANES + p)),
            pl.BlockSpec((1, s, LANES), lambda i, p: (i, 0, (COL_ATT + D_ATT) // LANES + p)),
            pl.BlockSpec((1, s, LANES), lambda i, p: (i, 0, (COL_ATT + 2 * D_ATT) // LANES + p)),
            pl.BlockSpec((s, LANES), lambda i, p: (0, 0)),
            pl.BlockSpec((s, LANES), lambda i, p: (0, 0)),
        ],
        out_specs=pl.BlockSpec((1, s, LANES), lambda i, p: (i, 0, p)),
        out_shape=jax.ShapeDtypeStruct((b, s, D_ATT), BF16),
        scratch_shapes=[pltpu.VMEM((s, LANES), BF16)] * 5 + [pltpu.VMEM((s // tq, tq, tk), F32)],
        compiler_params=pltpu.CompilerParams(
            dimension_semantics=("parallel", "arbitrary"), vmem_limit_bytes=VMEM_LIMIT),
        name="dilated_attention",
    )(z3, z3, z3, cos_t, sin_t)


POOL_PAD = 16


def _pool_kernel(u_ref, w_ref, sc_ref, o_ref, pad_s, *, seq):
    t1 = (lax.broadcasted_iota(jnp.int32, (seq, POOL_GROUP), 0) + 1).astype(F32)
    pad_s[0:POOL_PAD, :] = jnp.zeros((POOL_PAD, POOL_GROUP), F32)
    for gi, w in enumerate(POOL_WINDOWS):
        cols = slice(gi * POOL_GROUP, (gi + 1) * POOL_GROUP)
        u = u_ref[0, :, cols]
        s = u
        k = 1
        while k < w:
            pad_s[POOL_PAD:POOL_PAD + seq, :] = s
            s = s + pad_s[POOL_PAD - k:POOL_PAD - k + seq, :]
            k *= 2
        p = s / jnp.minimum(t1, float(w)) - u
        y = _dot(p.astype(BF16), w_ref[gi])
        o_ref[0, :, cols] = (y * sc_ref[:, cols]).astype(o_ref.dtype)


def _pool(z3, pool_w, pool_scale):
    b, s, _ = z3.shape
    kern = functools.partial(_pool_kernel, seq=s)
    return pl.pallas_call(
        kern,
        grid=(b,),
        in_specs=[
            pl.BlockSpec((1, s, D_POOL), lambda i: (i, 0, COL_POOL // D_POOL)),
            pl.BlockSpec(pool_w.shape, lambda i: (0, 0, 0)),
            pl.BlockSpec((1, D_POOL), lambda i: (0, 0)),
        ],
        out_specs=pl.BlockSpec((1, s, D_POOL), lambda i: (i, 0, 0)),
        out_shape=jax.ShapeDtypeStruct((b, s, D_POOL), BF16),
        scratch_shapes=[pltpu.VMEM((s + POOL_PAD, POOL_GROUP), F32)],
        compiler_params=pltpu.CompilerParams(
            dimension_semantics=("parallel",), vmem_limit_bytes=VMEM_LIMIT),
        name="pool_mixer",
    )(z3, pool_w, pool_scale)


def _split_dot(x, ones_b, terms):
    acc = None
    rem = x
    for _ in range(terms):
        piece = rem.astype(BF16)
        part = _dot(piece, ones_b)
        acc = part if acc is None else acc + part
        rem = rem - piece.astype(F32)
    return acc


def _rwkv_kernel(zr_ref, zrp_ref, zl_ref, zlp_ref, mur_ref, mul_ref, wup_ref, w0_ref, aup_ref, a0_ref,
                 gup_ref, kk_ref, ka_ref, rk_ref, gnw_ref, gnb_ref, o_ref, h_s, sh_s, *, chunk, rows):
    c = pl.program_id(1)
    npair = D_RWKV // LANES
    halo = 8
    wr = 3 * D_RWKV

    @pl.when(c == 0)
    def _():
        h_s[...] = jnp.zeros_like(h_s)

    row = lax.broadcasted_iota(jnp.int32, (LANES, LANES), 0)
    col = lax.broadcasted_iota(jnp.int32, (LANES, LANES), 1)
    head_ones = ((row < HEAD_DIM) == (col < HEAD_DIM)).astype(BF16)
    trow = lax.broadcasted_iota(jnp.int32, (chunk, chunk), 0)
    tcol = lax.broadcasted_iota(jnp.int32, (chunk, chunk), 1)
    tri = (trow >= tcol).astype(BF16)
    lane = lax.broadcasted_iota(jnp.int32, (chunk, LANES), 1)
    head_a = lane < HEAD_DIM
    strict_lower = row > col
    lower = row >= col
    eye = (row == col).astype(F32)
    first = c == 0

    def seg_sum(x):
        return jnp.concatenate(
            [_split_dot(x[:, p * LANES:(p + 1) * LANES], head_ones, 2) for p in range(npair)], axis=1)

    def stack2(x):
        return jnp.concatenate([jnp.where(head_a, x, 0.0), jnp.where(head_a, 0.0, x)], axis=0)

    tok = []
    for bi in range(rows):
        zr = zr_ref[bi]
        zl = zl_ref[bi]
        sh_s[bi, 0:halo, 0:wr] = jnp.where(first, 0.0, zrp_ref[bi])
        sh_s[bi, 0:halo, wr:wr + D_LORA_PACK] = jnp.where(first, 0.0, zlp_ref[bi])
        sh_s[bi, halo:halo + chunk, 0:wr] = zr
        sh_s[bi, halo:halo + chunk, wr:wr + D_LORA_PACK] = zl
        zr_prev = sh_s[bi, halo - 1:halo - 1 + chunk, 0:wr]
        zl_prev = sh_s[bi, halo - 1:halo - 1 + chunk, wr:wr + D_LORA_PACK]
        zr = zr + (zr_prev - zr) * mur_ref[...]
        zl = zl + (zl_prev - zl) * mul_ref[...]
        r = zr[:, 0:D_RWKV]
        k = zr[:, D_RWKV:2 * D_RWKV]
        v = zr[:, 2 * D_RWKV:3 * D_RWKV]
        zw = zl[:, 0:LORA_PAD]
        za = zl[:, LORA_PAD:2 * LORA_PAD]
        zg = zl[:, 2 * LORA_PAD:D_LORA_PACK]

        lw = w0_ref[...] + _dot(jnp.tanh(zw).astype(BF16), wup_ref[...])
        nlw = -lw
        softplus = jnp.maximum(nlw, 0.0) + jnp.log(1.0 + jnp.exp(-jnp.abs(nlw)))
        ld = -jnp.exp(-softplus - 0.5)
        a = jax.nn.sigmoid(a0_ref[...] + _dot(za.astype(BF16), aup_ref[...]))
        g = _dot(jax.nn.sigmoid(zg).astype(BF16), gup_ref[...])
        kk = k * kk_ref[...]
        k2 = k * (1.0 + (a - 1.0) * ka_ref[...])
        kk = kk / jnp.maximum(jnp.sqrt(seg_sum(kk * kk)), 1e-12)
        bonus = seg_sum(r * k2 * rk_ref[...]) * v
        bb = kk * a

        cum = _split_dot_left(tri, ld, 3)
        cum_end = cum[chunk - 1:chunk, :]
        e_inv = jnp.exp(-cum)
        e_end = jnp.exp(cum_end - cum)
        tok.append(dict(
            r_t=r * jnp.exp(cum), kk_t=kk * jnp.exp(cum - ld), b_h=bb * e_inv, k_h=k2 * e_inv,
            b_a=bb * e_end, k_a=k2 * e_end, a_end=jnp.exp(cum_end), v=v, bonus=bonus, g=g))

    units = [(bi, p) for bi in range(rows) for p in range(npair)]

    def pair(name):
        return [tok[bi][name][:, p * LANES:(p + 1) * LANES] for bi, p in units]

    kks = [stack2(x).astype(BF16) for x in pair("kk_t")]
    bs = [stack2(x).astype(BF16) for x in pair("b_h")]
    ks = [stack2(x).astype(BF16) for x in pair("k_h")]
    rs = [stack2(x).astype(BF16) for x in pair("r_t")]
    vs = [stack2(x).astype(BF16) for x in pair("v")]
    bat = [stack2(x).T.astype(BF16) for x in pair("b_a")]
    kat = [stack2(x).T.astype(BF16) for x in pair("k_a")]
    a_col = [jnp.broadcast_to(x, (LANES, LANES)).T for x in pair("a_end")]
    ids = range(len(units))

    l_b = [jnp.where(strict_lower, _dot_nt(kks[i], bs[i]), 0.0) for i in ids]
    l_k = [jnp.where(strict_lower, _dot_nt(kks[i], ks[i]), 0.0).astype(BF16) for i in ids]
    m_b = [jnp.where(lower, _dot_nt(rs[i], bs[i]), 0.0).astype(BF16) for i in ids]
    m_k = [jnp.where(lower, _dot_nt(rs[i], ks[i]), 0.0).astype(BF16) for i in ids]

    sel = ((row ^ col) < 2) & ((row & 1) != 0) & ((col & 1) == 0)
    t_inv = [eye - jnp.where(sel, l_b[i], 0.0) for i in ids]
    s = 2
    while s < HEAD_DIM:
        sel = ((row ^ col) < 2 * s) & ((row & s) != 0) & ((col & s) == 0)
        t_b = [t_inv[i].astype(BF16) for i in ids]
        inner = [_dot(jnp.where(sel, l_b[i], 0.0).astype(BF16), t_b[i]).astype(BF16) for i in ids]
        t_inv = [t_inv[i] - _dot(t_b[i], inner[i]) for i in ids]
        s *= 2
    t_b = [t_inv[i].astype(BF16) for i in ids]

    lkv = [_dot(l_k[i], vs[i]).astype(BF16) for i in ids]
    w_m = [_dot(t_b[i], kks[i]).astype(BF16) for i in ids]
    u_t = [_dot(t_b[i], lkv[i]) for i in ids]
    y0 = [_dot(m_k[i], vs[i]) for i in ids]
    g_m = [_dot(kat[i], vs[i]) for i in ids]

    h = [h_s[i] for i in ids]
    h_b = [h[i].astype(BF16) for i in ids]
    u_b = [(-(_dot(w_m[i], h_b[i]) + u_t[i])).astype(BF16) for i in ids]
    y = [_dot(rs[i], h_b[i]) + _dot(m_b[i], u_b[i]) + y0[i] for i in ids]
    for i in ids:
        h_s[i] = a_col[i] * h[i] + _dot(bat[i], u_b[i]) + g_m[i]

    for bi in range(rows):
        yb = jnp.concatenate(
            [y[bi * npair + p][0:chunk, :] + y[bi * npair + p][chunk:2 * chunk, :] for p in range(npair)], axis=1)
        mean = seg_sum(yb) * (1.0 / HEAD_DIM)
        d = yb - mean
        var = seg_sum(d * d) * (1.0 / HEAD_DIM)
        yn = d * lax.rsqrt(var + RWKV_GN_EPS) * gnw_ref[...] + gnb_ref[...]
        o_ref[bi] = ((yn + tok[bi]["bonus"]) * tok[bi]["g"]).astype(o_ref.dtype)


def _split_dot_left(tri_b, x, terms):
    acc = None
    rem = x
    for _ in range(terms):
        piece = rem.astype(BF16)
        part = _dot(tri_b, piece)
        acc = part if acc is None else acc + part
        rem = rem - piece.astype(F32)
    return acc


def _rwkv(z3, prm, *, rows):
    b, s, _ = z3.shape
    chunk = RWKV_CHUNK
    nc = s // chunk
    wr = 3 * D_RWKV
    kern = functools.partial(_rwkv_kernel, chunk=chunk, rows=rows)
    rows8 = chunk // 8
    npair = D_RWKV // LANES

    def vec(n):
        return pl.BlockSpec((1, n), lambda i, c: (0, 0))

    def full(a):
        return pl.BlockSpec(a.shape, lambda i, c: (0, 0))

    return pl.pallas_call(
        kern,
        grid=(b // rows, nc),
        in_specs=[
            pl.BlockSpec((rows, chunk, wr), lambda i, c: (i, c, COL_RKV // wr)),
            pl.BlockSpec((rows, 8, wr), lambda i, c: (i, jnp.maximum(c * rows8 - 1, 0), COL_RKV // wr)),
            pl.BlockSpec((rows, chunk, D_LORA_PACK), lambda i, c: (i, c, COL_LORA // D_LORA_PACK)),
            pl.BlockSpec((rows, 8, D_LORA_PACK),
                         lambda i, c: (i, jnp.maximum(c * rows8 - 1, 0), COL_LORA // D_LORA_PACK)),
            vec(wr), vec(D_LORA_PACK),
            full(prm["w_up"]), vec(D_RWKV), full(prm["a_up"]), vec(D_RWKV), full(prm["g_up"]),
            vec(D_RWKV), vec(D_RWKV), vec(D_RWKV), vec(D_RWKV), vec(D_RWKV),
        ],
        out_specs=pl.BlockSpec((rows, chunk, D_RWKV), lambda i, c: (i, c, 0)),
        out_shape=jax.ShapeDtypeStruct((b, s, D_RWKV), BF16),
        scratch_shapes=[
            pltpu.VMEM((rows * npair, LANES, LANES), F32),
            pltpu.VMEM((rows, chunk + 8, wr + D_LORA_PACK), F32),
        ],
        compiler_params=pltpu.CompilerParams(
            dimension_semantics=("parallel", "arbitrary"), vmem_limit_bytes=VMEM_LIMIT),
        name="rwkv7",
    )(z3, z3, z3, z3, prm["mu_r"], prm["mu_l"], prm["w_up"], prm["w0"], prm["a_up"], prm["a0"],
      prm["g_up"], prm["k_k"], prm["k_a"], prm["r_k"], prm["gn_w"], prm["gn_b"])


def _pad_cols(w, n):
    return jnp.pad(w, ((0, 0),) * (w.ndim - 1) + ((0, n - w.shape[-1]),))


def _pack_w_in(w_in):
    d_att3 = 3 * D_ATT
    o = d_att3 + 3 * D_RWKV
    parts = [
        w_in[..., :o],
        _pad_cols(w_in[..., o:o + RWKV_DECAY_LORA], LORA_PAD),
        _pad_cols(w_in[..., o + RWKV_DECAY_LORA:o + RWKV_DECAY_LORA + RWKV_AAA_LORA], LORA_PAD),
        w_in[..., o + RWKV_DECAY_LORA + RWKV_AAA_LORA:],
    ]
    return jnp.concatenate(parts, axis=-1).astype(BF16)


def _pack_mu(mu):
    o = 3 * D_RWKV
    mu_l = jnp.concatenate([
        jnp.pad(mu[o:o + RWKV_DECAY_LORA], (0, LORA_PAD - RWKV_DECAY_LORA)),
        jnp.pad(mu[o + RWKV_DECAY_LORA:o + RWKV_DECAY_LORA + RWKV_AAA_LORA], (0, LORA_PAD - RWKV_AAA_LORA)),
        mu[o + RWKV_DECAY_LORA + RWKV_AAA_LORA:],
    ])
    return mu[None, :o], mu_l[None, :]


def _pad_rows(w, n):
    return jnp.pad(w, ((0, n - w.shape[0]), (0, 0)))


def _rope_tables(seq):
    inv_freq = ROPE_THETA ** (-jnp.arange(0, HEAD_DIM, 2, dtype=F32) / HEAD_DIM)
    ang = jnp.arange(seq, dtype=F32)[:, None] * inv_freq[None, :]
    cos, sin = jnp.cos(ang), jnp.sin(ang)
    reps = LANES // HEAD_DIM
    cos_t = jnp.tile(jnp.concatenate([cos, cos], axis=1), (1, reps))
    sin_t = jnp.tile(jnp.concatenate([-sin, sin], axis=1), (1, reps))
    return cos_t, sin_t


def _tile(n, pref):
    t = min(n, pref)
    assert n % t == 0
    return t


def kernel(x, ffn1_norm_pre, ffn1_norm_post, ffn1_w_gate, ffn1_w_up, ffn1_w_down, mix_norm_pre, mix_norm_post, w_in, w_out, rwkv_mu, rwkv_w_up, rwkv_w0, rwkv_a_up, rwkv_a0, rwkv_g_up, rwkv_k_k, rwkv_k_a, rwkv_r_k, rwkv_gn_w, rwkv_gn_b, pool_w, pool_scale, ffn2_norm_pre, ffn2_norm_post, ffn2_w_gate, ffn2_w_up, ffn2_w_down):
    b, s, d = x.shape
    depth = w_in.shape[0]
    t = b * s
    d_ff = ffn1_w_gate.shape[2]
    assert s % RWKV_CHUNK == 0 and b % RWKV_ROWS == 0 and d == D_ATT + D_RWKV + D_POOL
    tm = _tile(t, 512)
    tm_proj = _tile(t, 1024)
    tf = _tile(d_ff, 512)
    tn = _tile(D_PACK, 512)
    cos_t, sin_t = _rope_tables(s)
    x2 = x.reshape(t, d).astype(F32)

    tr = _tile(d, 256)
    ffn_w = [[_cast_bf16(w, tr=_tile(w.shape[1], tr)) for w in ws] for ws in
             ((ffn1_w_gate, ffn1_w_up, ffn1_w_down), (ffn2_w_gate, ffn2_w_up, ffn2_w_down))]
    ffn_g = [[g[:, None, :] for g in gs] for gs in
             ((ffn1_norm_pre, ffn1_norm_post), (ffn2_norm_pre, ffn2_norm_post))]
    w_in_p = _pack_w_in(_cast_bf16(w_in, tr=tr))
    w_out_b = _cast_bf16(w_out, tr=tr)
    mix_g_pre, mix_g_post = mix_norm_pre[:, None, :], mix_norm_post[:, None, :]

    for i in range(depth):
        x2 = _ffn(x2, *ffn_g[0], *ffn_w[0], layer=i, tm=tm, tf=tf)

        z = _proj(x2, mix_g_pre, w_in_p, layer=i, tm=tm_proj, tn=tn)
        z3 = z.reshape(b, s, D_PACK)
        y_att = _attention(z3, cos_t, sin_t)
        mu_r, mu_l = _pack_mu(rwkv_mu[i])
        prm = dict(
            mu_r=mu_r, mu_l=mu_l,
            w_up=_pad_rows(rwkv_w_up[i], LORA_PAD).astype(BF16), w0=rwkv_w0[i][None],
            a_up=_pad_rows(rwkv_a_up[i], LORA_PAD).astype(BF16), a0=rwkv_a0[i][None],
            g_up=rwkv_g_up[i].astype(BF16), k_k=rwkv_k_k[i][None], k_a=rwkv_k_a[i][None],
            r_k=rwkv_r_k[i].reshape(1, D_RWKV), gn_w=rwkv_gn_w[i][None], gn_b=rwkv_gn_b[i][None])
        y_rwkv = _rwkv(z3, prm, rows=RWKV_ROWS)
        y_pool = _pool(z3, pool_w[i].astype(BF16), pool_scale[i][None])

        x2 = _mixout(x2, y_att.reshape(t, D_ATT), y_rwkv.reshape(t, D_RWKV), y_pool.reshape(t, D_POOL),
                     w_out_b, mix_g_post, layer=i, tm=tm)

        x2 = _ffn(x2, *ffn_g[1], *ffn_w[1], layer=i, tm=tm, tf=tf)
    return x2.reshape(b, s, d).astype(x.dtype)
```

```python
import functools
import math

import jax
import jax.numpy as jnp
from jax import lax
from jax.experimental import pallas as pl
from jax.experimental.pallas import tpu as pltpu

F32 = jnp.float32
BF16 = jnp.bfloat16

HEAD_DIM = 64
N_ATT_HEADS = 12
N_RWKV_HEADS = 12
D_ATT = N_ATT_HEADS * HEAD_DIM
D_RWKV = N_RWKV_HEADS * HEAD_DIM
D_POOL = 512
N_POOL_GROUPS = 4
POOL_GROUP = D_POOL // N_POOL_GROUPS
POOL_WINDOWS = (2, 4, 8, 16)
DILATED_BRANCHES = ((128, 1), (512, 4), (2048, 16))
ROPE_THETA = 10000.0
RWKV_DECAY_LORA = 96
RWKV_AAA_LORA = 96
RWKV_GATE_LORA = 256
NORM_EPS = 1e-6
RWKV_GN_EPS = 64e-5

LANES = 128
LORA_PAD = 128
COL_ATT = 0
COL_RKV = 3 * D_ATT
COL_LORA = COL_RKV + 3 * D_RWKV
D_LORA_PACK = 2 * LORA_PAD + RWKV_GATE_LORA
COL_POOL = COL_LORA + D_LORA_PACK
D_PACK = COL_POOL + D_POOL

SEG_SUM_TERMS = 1
RWKV_ROWS = 2
RWKV_CHUNK = 64
ATT_KV_TILE = 512
ATT_GROUP = 2
VMEM_LIMIT = 56 * 1024 * 1024

NEG_BIG = -1e30
LOG2_E = math.log2(math.e)


def _rms(xf, g):
    return xf * lax.rsqrt(jnp.mean(xf * xf, axis=-1, keepdims=True) + NORM_EPS) * g


def _dot(a, b):
    return jnp.dot(a, b, preferred_element_type=F32)


def _dot_nt(a, b):
    return lax.dot_general(a, b, (((1,), (1,)), ((), ())), preferred_element_type=F32)


def _ffn_kernel(x_ref, gpre_ref, gpost_ref, wg_ref, wu_ref, wd_ref, o_ref, h_ref, acc_ref):
    j = pl.program_id(1)

    @pl.when(j == 0)
    def _():
        h_ref[...] = _rms(x_ref[...], gpre_ref[...]).astype(BF16)
        acc_ref[...] = jnp.zeros_like(acc_ref)

    h = h_ref[...]
    g = _dot(h, wg_ref[...])
    u = _dot(h, wu_ref[...])
    a = (g * jax.nn.sigmoid(g) * u).astype(BF16)
    acc_ref[...] += _dot(a, wd_ref[...])

    @pl.when(j == pl.num_programs(1) - 1)
    def _():
        o_ref[...] = x_ref[...] + 0.5 * _rms(acc_ref[...], gpost_ref[...])


def _ffn(x, g_pre, g_post, wg, wu, wd, *, layer, tm, tf):
    t, d = x.shape
    f = wg.shape[2]
    return pl.pallas_call(
        _ffn_kernel,
        grid=(t // tm, f // tf),
        in_specs=[
            pl.BlockSpec((tm, d), lambda i, j: (i, 0)),
            pl.BlockSpec((None, 1, d), lambda i, j: (layer, 0, 0)),
            pl.BlockSpec((None, 1, d), lambda i, j: (layer, 0, 0)),
            pl.BlockSpec((None, d, tf), lambda i, j: (layer, 0, j)),
            pl.BlockSpec((None, d, tf), lambda i, j: (layer, 0, j)),
            pl.BlockSpec((None, tf, d), lambda i, j: (layer, j, 0)),
        ],
        out_specs=pl.BlockSpec((tm, d), lambda i, j: (i, 0)),
        out_shape=jax.ShapeDtypeStruct((t, d), F32),
        scratch_shapes=[pltpu.VMEM((tm, d), BF16), pltpu.VMEM((tm, d), F32)],
        compiler_params=pltpu.CompilerParams(
            dimension_semantics=("parallel", "arbitrary"), vmem_limit_bytes=VMEM_LIMIT),
        name="ffn",
    )(x, g_pre, g_post, wg, wu, wd)


def _proj_kernel(x_ref, g_ref, w_ref, o_ref, h_ref):
    @pl.when(pl.program_id(1) == 0)
    def _():
        h_ref[...] = _rms(x_ref[...], g_ref[...]).astype(BF16)

    o_ref[...] = _dot(h_ref[...], w_ref[...])


def _proj(x, g, w, *, layer, tm, tn):
    t, d = x.shape
    n = w.shape[2]
    return pl.pallas_call(
        _proj_kernel,
        grid=(t // tm, n // tn),
        in_specs=[
            pl.BlockSpec((tm, d), lambda i, j: (i, 0)),
            pl.BlockSpec((None, 1, d), lambda i, j: (layer, 0, 0)),
            pl.BlockSpec((None, d, tn), lambda i, j: (layer, 0, j)),
        ],
        out_specs=pl.BlockSpec((tm, tn), lambda i, j: (i, j)),
        out_shape=jax.ShapeDtypeStruct((t, n), F32),
        scratch_shapes=[pltpu.VMEM((tm, d), BF16)],
        compiler_params=pltpu.CompilerParams(
            dimension_semantics=("parallel", "arbitrary"), vmem_limit_bytes=VMEM_LIMIT),
        name="proj_in",
    )(x, g, w)


def _mixout_kernel(x_ref, ya_ref, yr_ref, yp_ref, wa_ref, wr_ref, wp_ref, g_ref, o_ref):
    mix = _dot(ya_ref[...], wa_ref[...]) + _dot(yr_ref[...], wr_ref[...]) + _dot(yp_ref[...], wp_ref[...])
    o_ref[...] = x_ref[...] + _rms(mix, g_ref[...])


def _mixout(x, ya, yr, yp, wo, g, *, layer, tm):
    t, d = x.shape
    da, dr, dp = ya.shape[1], yr.shape[1], yp.shape[1]
    assert da == dr and (da + dr) % dp == 0
    return pl.pallas_call(
        _mixout_kernel,
        grid=(t // tm,),
        in_specs=[
            pl.BlockSpec((tm, d), lambda i: (i, 0)),
            pl.BlockSpec((tm, da), lambda i: (i, 0)),
            pl.BlockSpec((tm, dr), lambda i: (i, 0)),
            pl.BlockSpec((tm, dp), lambda i: (i, 0)),
            pl.BlockSpec((None, da, d), lambda i: (layer, 0, 0)),
            pl.BlockSpec((None, dr, d), lambda i: (layer, 1, 0)),
            pl.BlockSpec((None, dp, d), lambda i: (layer, (da + dr) // dp, 0)),
            pl.BlockSpec((None, 1, d), lambda i: (layer, 0, 0)),
        ],
        out_specs=pl.BlockSpec((tm, d), lambda i: (i, 0)),
        out_shape=jax.ShapeDtypeStruct((t, d), F32),
        compiler_params=pltpu.CompilerParams(
            dimension_semantics=("parallel",), vmem_limit_bytes=VMEM_LIMIT),
        name="proj_out",
    )(x, ya, yr, yp, wo, wo, wo, g)


def _cast_kernel(w_ref, o_ref):
    o_ref[...] = w_ref[...].astype(BF16)


def _cast_bf16(w, *, tr):
    nl, r, c = w.shape
    return pl.pallas_call(
        _cast_kernel,
        grid=(nl, r // tr),
        in_specs=[pl.BlockSpec((1, tr, c), lambda i, j: (i, j, 0))],
        out_specs=pl.BlockSpec((1, tr, c), lambda i, j: (i, j, 0)),
        out_shape=jax.ShapeDtypeStruct(w.shape, BF16),
        compiler_params=pltpu.CompilerParams(
            dimension_semantics=("parallel", "parallel"), vmem_limit_bytes=VMEM_LIMIT),
        name="cast_bf16",
    )(w)


def _att_bias_tile(base, tq, tk):
    row = lax.broadcasted_iota(jnp.int32, (tq, tk), 0)
    col = lax.broadcasted_iota(jnp.int32, (tq, tk), 1)
    delta = base + row - col
    cnt = jnp.zeros((tq, tk), jnp.int32)
    for window, dil in DILATED_BRANCHES:
        hit = (delta >= 0) & (delta <= window) & ((delta & (dil - 1)) == 0)
        cnt = cnt + hit.astype(jnp.int32)
    return jnp.where(cnt == 3, math.log2(3.0),
                     jnp.where(cnt == 2, 1.0, jnp.where(cnt == 1, 0.0, NEG_BIG))).astype(F32)


def _att_kernel(q_ref, k_ref, v_ref, cos_ref, sin_ref, o_ref,
                qa_s, qb_s, k_s, va_s, vb_s, bias_s, *, seq, tq, tk, group):
    nq = seq // tq
    ratio = tk // tq

    @pl.when(pl.program_id(1) == 0)
    def _():
        for off in range(nq):
            bias_s[off] = _att_bias_tile(off * tq, tq, tk)

    lane = lax.broadcasted_iota(jnp.int32, (tq, LANES), 1)
    first_half = (lane & (HEAD_DIM - 1)) < (HEAD_DIM // 2)
    head_a = lane < HEAD_DIM

    def rope(x, cos, sin):
        swapped = jnp.where(first_half, pltpu.roll(x, LANES - HEAD_DIM // 2, 1), pltpu.roll(x, HEAD_DIM // 2, 1))
        return x * cos + swapped * sin

    def prep(i, carry):
        rows = pl.ds(pl.multiple_of(i * tq, tq), tq)
        cos = cos_ref[rows, :]
        sin = sin_ref[rows, :]
        q = rope(q_ref[0, rows, :], cos, sin) * (HEAD_DIM ** -0.5 * LOG2_E)
        k = rope(k_ref[0, rows, :], cos, sin)
        v = v_ref[0, rows, :]
        qa_s[rows, :] = jnp.where(head_a, q, 0.0).astype(BF16)
        qb_s[rows, :] = jnp.where(head_a, 0.0, q).astype(BF16)
        k_s[rows, :] = k.astype(BF16)
        va_s[rows, :] = jnp.where(head_a, v, 1.0).astype(BF16)
        vb_s[rows, :] = jnp.where(head_a, 1.0, v).astype(BF16)
        return carry

    lax.fori_loop(0, nq, prep, 0)

    def tile_max(s):
        m = s[:, 0:LANES]
        for t in range(1, s.shape[1] // LANES):
            m = jnp.maximum(m, s[:, t * LANES:(t + 1) * LANES])
        return m

    q_s = (qa_s, qb_s)
    v_s = (va_s, vb_s)
    m_run = {}
    acc = {}
    for kj in range(seq // tk):
        blocks = list(range(ratio * kj, nq))
        for g0 in range(0, len(blocks), group):
            chains = [(qi, hd) for qi in blocks[g0:g0 + group] for hd in range(2)]
            width = {qi: (tq if qi == ratio * kj else tk) for qi, _ in chains}
            s = {}
            for qi, hd in chains:
                krows = slice(kj * tk, kj * tk + width[qi])
                s[qi, hd] = (_dot_nt(q_s[hd][qi * tq:(qi + 1) * tq, :], k_s[krows, :])
                             + bias_s[qi - ratio * kj, :, 0:width[qi]])
            p = {}
            for ch in chains:
                m_tile = jnp.max(tile_max(s[ch]), axis=-1, keepdims=True)
                if kj == 0:
                    m_new = m_tile
                else:
                    m_new = jnp.maximum(m_run[ch], m_tile)
                    acc[ch] = jnp.exp2(m_run[ch] - m_new) * acc[ch]
                m_run[ch] = m_new
                p[ch] = jnp.exp2(s[ch] - m_new).astype(BF16)
            for ch in chains:
                pv = _dot(p[ch], v_s[ch[1]][kj * tk:kj * tk + width[ch[0]], :])
                acc[ch] = pv if kj == 0 else acc[ch] + pv

    for qi in range(nq):
        o_a = acc[qi, 0] / pltpu.roll(acc[qi, 0], HEAD_DIM, 1)
        o_b = acc[qi, 1] / pltpu.roll(acc[qi, 1], HEAD_DIM, 1)
        o_ref[0, qi * tq:(qi + 1) * tq, :] = jnp.where(head_a, o_a, o_b).astype(o_ref.dtype)


def _attention(z3, cos_t, sin_t):
    b, s, _ = z3.shape
    tk = min(ATT_KV_TILE, s)
    tq = tk // 2
    npair = D_ATT // LANES
    kern = functools.partial(_att_kernel, seq=s, tq=tq, tk=tk, group=ATT_GROUP)
    return pl.pallas_call(
        kern,
        grid=(b, npair),
        in_specs=[
            pl.BlockSpec((1, s, LANES), lambda i, p: (i, 0, COL_ATT // LANES + p)),
            pl.BlockSpec((1, s, LANES), lambda i, p: (i, 0, (COL_ATT + D_ATT) // LANES + p)),
            pl.BlockSpec((1, s, LANES), lambda i, p: (i, 0, (COL_ATT + 2 * D_ATT) // LANES + p)),
            pl.BlockSpec((s, LANES), lambda i, p: (0, 0)),
            pl.BlockSpec((s, LANES), lambda i, p: (0, 0)),
        ],
        out_specs=pl.BlockSpec((1, s, LANES), lambda i, p: (i, 0, p)),
        out_shape=jax.ShapeDtypeStruct((b, s, D_ATT), BF16),
        scratch_shapes=[pltpu.VMEM((s, LANES), BF16)] * 5 + [pltpu.VMEM((s // tq, tq, tk), F32)],
        compiler_params=pltpu.CompilerParams(
            dimension_semantics=("parallel", "arbitrary"), vmem_limit_bytes=VMEM_LIMIT),
        name="dilated_attention",
    )(z3, z3, z3, cos_t, sin_t)


POOL_PAD = 16


def _pool_kernel(u_ref, w_ref, sc_ref, o_ref, pad_s, *, seq):
    t1 = (lax.broadcasted_iota(jnp.int32, (seq, POOL_GROUP), 0) + 1).astype(F32)
    pad_s[0:POOL_PAD, :] = jnp.zeros((POOL_PAD, POOL_GROUP), F32)
    for gi, w in enumerate(POOL_WINDOWS):
        cols = slice(gi * POOL_GROUP, (gi + 1) * POOL_GROUP)
        u = u_ref[0, :, cols]
        s = u
        k = 1
        while k < w:
            pad_s[POOL_PAD:POOL_PAD + seq, :] = s
            s = s + pad_s[POOL_PAD - k:POOL_PAD - k + seq, :]
            k *= 2
        p = s / jnp.minimum(t1, float(w)) - u
        y = _dot(p.astype(BF16), w_ref[gi])
        o_ref[0, :, cols] = (y * sc_ref[:, cols]).astype(o_ref.dtype)


def _pool(z3, pool_w, pool_scale):
    b, s, _ = z3.shape
    kern = functools.partial(_pool_kernel, seq=s)
    return pl.pallas_call(
        kern,
        grid=(b,),
        in_specs=[
            pl.BlockSpec((1, s, D_POOL), lambda i: (i, 0, COL_POOL // D_POOL)),
            pl.BlockSpec(pool_w.shape, lambda i: (0, 0, 0)),
            pl.BlockSpec((1, D_POOL), lambda i: (0, 0)),
        ],
        out_specs=pl.BlockSpec((1, s, D_POOL), lambda i: (i, 0, 0)),
        out_shape=jax.ShapeDtypeStruct((b, s, D_POOL), BF16),
        scratch_shapes=[pltpu.VMEM((s + POOL_PAD, POOL_GROUP), F32)],
        compiler_params=pltpu.CompilerParams(
            dimension_semantics=("parallel",), vmem_limit_bytes=VMEM_LIMIT),
        name="pool_mixer",
    )(z3, pool_w, pool_scale)


def _split_dot(x, ones_b, terms):
    acc = None
    rem = x
    for _ in range(terms):
        piece = rem.astype(BF16)
        part = _dot(piece, ones_b)
        acc = part if acc is None else acc + part
        rem = rem - piece.astype(F32)
    return acc


def _rwkv_kernel(zr_ref, zrp_ref, zl_ref, zlp_ref, mur_ref, mul_ref, wup_ref, w0_ref, aup_ref, a0_ref,
                 gup_ref, kk_ref, ka_ref, rk_ref, gnw_ref, gnb_ref, o_ref, h_s, sh_s, *, chunk, rows):
    c = pl.program_id(1)
    npair = D_RWKV // LANES
    halo = 8
    wr = 3 * D_RWKV

    @pl.when(c == 0)
    def _():
        h_s[...] = jnp.zeros_like(h_s)

    row = lax.broadcasted_iota(jnp.int32, (LANES, LANES), 0)
    col = lax.broadcasted_iota(jnp.int32, (LANES, LANES), 1)
    head_ones = ((row < HEAD_DIM) == (col < HEAD_DIM)).astype(BF16)
    trow = lax.broadcasted_iota(jnp.int32, (chunk, chunk), 0)
    tcol = lax.broadcasted_iota(jnp.int32, (chunk, chunk), 1)
    tri = (trow >= tcol).astype(BF16)
    lane = lax.broadcasted_iota(jnp.int32, (chunk, LANES), 1)
    head_a = lane < HEAD_DIM
    strict_lower = row > col
    lower = row >= col
    eye = (row == col).astype(F32)
    first = c == 0

    def seg_sum(x):
        return jnp.concatenate(
            [_split_dot(x[:, p * LANES:(p + 1) * LANES], head_ones, SEG_SUM_TERMS) for p in range(npair)], axis=1)

    def stack2(x):
        return jnp.concatenate([jnp.where(head_a, x, 0.0), jnp.where(head_a, 0.0, x)], axis=0)

    tok = []
    for bi in range(rows):
        zr = zr_ref[bi]
        zl = zl_ref[bi]
        sh_s[bi, 0:halo, 0:wr] = jnp.where(first, 0.0, zrp_ref[bi])
        sh_s[bi, 0:halo, wr:wr + D_LORA_PACK] = jnp.where(first, 0.0, zlp_ref[bi])
        sh_s[bi, halo:halo + chunk, 0:wr] = zr
        sh_s[bi, halo:halo + chunk, wr:wr + D_LORA_PACK] = zl
        zr_prev = sh_s[bi, halo - 1:halo - 1 + chunk, 0:wr]
        zl_prev = sh_s[bi, halo - 1:halo - 1 + chunk, wr:wr + D_LORA_PACK]
        zr = zr + (zr_prev - zr) * mur_ref[...]
        zl = zl + (zl_prev - zl) * mul_ref[...]
        r = zr[:, 0:D_RWKV]
        k = zr[:, D_RWKV:2 * D_RWKV]
        v = zr[:, 2 * D_RWKV:3 * D_RWKV]
        zw = zl[:, 0:LORA_PAD]
        za = zl[:, LORA_PAD:2 * LORA_PAD]
        zg = zl[:, 2 * LORA_PAD:D_LORA_PACK]

        lw = w0_ref[...] + _dot(jnp.tanh(zw).astype(BF16), wup_ref[...])
        nlw = -lw
        softplus = jnp.maximum(nlw, 0.0) + jnp.log(1.0 + jnp.exp(-jnp.abs(nlw)))
        ld = -jnp.exp(-softplus - 0.5)
        a = jax.nn.sigmoid(a0_ref[...] + _dot(za.astype(BF16), aup_ref[...]))
        g = _dot(jax.nn.sigmoid(zg).astype(BF16), gup_ref[...])
        kk = k * kk_ref[...]
        k2 = k * (1.0 + (a - 1.0) * ka_ref[...])
        kk = kk / jnp.maximum(jnp.sqrt(seg_sum(kk * kk)), 1e-12)
        bonus = seg_sum(r * k2 * rk_ref[...]) * v
        bb = kk * a

        cum = _split_dot_left(tri, ld, 3)
        cum_end = cum[chunk - 1:chunk, :]
        e_inv = jnp.exp(-cum)
        e_end = jnp.exp(cum_end - cum)
        tok.append(dict(
            r_t=r * jnp.exp(cum), kk_t=kk * jnp.exp(cum - ld), b_h=bb * e_inv, k_h=k2 * e_inv,
            b_a=bb * e_end, k_a=k2 * e_end, a_end=jnp.exp(cum_end), v=v, bonus=bonus, g=g))

    units = [(bi, p) for bi in range(rows) for p in range(npair)]

    def pair(name):
        return [tok[bi][name][:, p * LANES:(p + 1) * LANES] for bi, p in units]

    kks = [stack2(x).astype(BF16) for x in pair("kk_t")]
    bs = [stack2(x).astype(BF16) for x in pair("b_h")]
    ks = [stack2(x).astype(BF16) for x in pair("k_h")]
    rs = [stack2(x).astype(BF16) for x in pair("r_t")]
    vs = [stack2(x).astype(BF16) for x in pair("v")]
    bat = [stack2(x).T.astype(BF16) for x in pair("b_a")]
    kat = [stack2(x).T.astype(BF16) for x in pair("k_a")]
    a_col = [jnp.broadcast_to(x, (LANES, LANES)).T for x in pair("a_end")]
    ids = range(len(units))

    tt = [_dot_nt(jnp.concatenate([kks[i], rs[i]], axis=0), jnp.concatenate([bs[i], ks[i]], axis=0)) for i in ids]
    l_b = [jnp.where(strict_lower, tt[i][0:LANES, 0:LANES], 0.0) for i in ids]
    l_k = [jnp.where(strict_lower, tt[i][0:LANES, LANES:2 * LANES], 0.0).astype(BF16) for i in ids]
    m_b = [jnp.where(lower, tt[i][LANES:2 * LANES, 0:LANES], 0.0).astype(BF16) for i in ids]
    m_k = [jnp.where(lower, tt[i][LANES:2 * LANES, LANES:2 * LANES], 0.0).astype(BF16) for i in ids]

    sel = ((row ^ col) < 2) & ((row & 1) != 0) & ((col & 1) == 0)
    t_inv = [eye - jnp.where(sel, l_b[i], 0.0) for i in ids]
    s = 2
    while s < HEAD_DIM:
        sel = ((row ^ col) < 2 * s) & ((row & s) != 0) & ((col & s) == 0)
        t_b = [t_inv[i].astype(BF16) for i in ids]
        inner = [_dot(jnp.where(sel, l_b[i], 0.0).astype(BF16), t_b[i]).astype(BF16) for i in ids]
        t_inv = [t_inv[i] - _dot(t_b[i], inner[i]) for i in ids]
        s *= 2
    t_b = [t_inv[i].astype(BF16) for i in ids]

    mlv = [_dot(jnp.concatenate([m_k[i], l_k[i]], axis=0), vs[i]) for i in ids]
    y0 = [mlv[i][0:LANES, :] for i in ids]
    lkv = [mlv[i][LANES:2 * LANES, :].astype(BF16) for i in ids]
    w_m = [_dot(t_b[i], kks[i]).astype(BF16) for i in ids]

    h = [h_s[i] for i in ids]
    h_b = [h[i].astype(BF16) for i in ids]
    u_b = [(-_dot(jnp.concatenate([w_m[i], t_b[i]], axis=1),
                  jnp.concatenate([h_b[i], lkv[i]], axis=0))).astype(BF16) for i in ids]
    y = [_dot(jnp.concatenate([rs[i], m_b[i]], axis=1), jnp.concatenate([h_b[i], u_b[i]], axis=0)) + y0[i]
         for i in ids]
    for i in ids:
        h_s[i] = a_col[i] * h[i] + _dot(jnp.concatenate([bat[i], kat[i]], axis=1),
                                        jnp.concatenate([u_b[i], vs[i]], axis=0))

    for bi in range(rows):
        yb = jnp.concatenate(
            [y[bi * npair + p][0:chunk, :] + y[bi * npair + p][chunk:2 * chunk, :] for p in range(npair)], axis=1)
        mean = seg_sum(yb) * (1.0 / HEAD_DIM)
        d = yb - mean
        var = seg_sum(d * d) * (1.0 / HEAD_DIM)
        yn = d * lax.rsqrt(var + RWKV_GN_EPS) * gnw_ref[...] + gnb_ref[...]
        o_ref[bi] = ((yn + tok[bi]["bonus"]) * tok[bi]["g"]).astype(o_ref.dtype)


def _split_dot_left(tri_b, x, terms):
    acc = None
    rem = x
    for _ in range(terms):
        piece = rem.astype(BF16)
        part = _dot(tri_b, piece)
        acc = part if acc is None else acc + part
        rem = rem - piece.astype(F32)
    return acc


def _rwkv(z3, prm, *, rows):
    b, s, _ = z3.shape
    chunk = RWKV_CHUNK
    nc = s // chunk
    wr = 3 * D_RWKV
    kern = functools.partial(_rwkv_kernel, chunk=chunk, rows=rows)
    rows8 = chunk // 8
    npair = D_RWKV // LANES

    def vec(n):
        return pl.BlockSpec((1, n), lambda i, c: (0, 0))

    def full(a):
        return pl.BlockSpec(a.shape, lambda i, c: (0, 0))

    return pl.pallas_call(
        kern,
        grid=(b // rows, nc),
        in_specs=[
            pl.BlockSpec((rows, chunk, wr), lambda i, c: (i, c, COL_RKV // wr)),
            pl.BlockSpec((rows, 8, wr), lambda i, c: (i, jnp.maximum(c * rows8 - 1, 0), COL_RKV // wr)),
            pl.BlockSpec((rows, chunk, D_LORA_PACK), lambda i, c: (i, c, COL_LORA // D_LORA_PACK)),
            pl.BlockSpec((rows, 8, D_LORA_PACK),
                         lambda i, c: (i, jnp.maximum(c * rows8 - 1, 0), COL_LORA // D_LORA_PACK)),
            vec(wr), vec(D_LORA_PACK),
            full(prm["w_up"]), vec(D_RWKV), full(prm["a_up"]), vec(D_RWKV), full(prm["g_up"]),
            vec(D_RWKV), vec(D_RWKV), vec(D_RWKV), vec(D_RWKV), vec(D_RWKV),
        ],
        out_specs=pl.BlockSpec((rows, chunk, D_RWKV), lambda i, c: (i, c, 0)),
        out_shape=jax.ShapeDtypeStruct((b, s, D_RWKV), BF16),
        scratch_shapes=[
            pltpu.VMEM((rows * npair, LANES, LANES), F32),
            pltpu.VMEM((rows, chunk + 8, wr + D_LORA_PACK), F32),
        ],
        compiler_params=pltpu.CompilerParams(
            dimension_semantics=("parallel", "arbitrary"), vmem_limit_bytes=VMEM_LIMIT),
        name="rwkv7",
    )(z3, z3, z3, z3, prm["mu_r"], prm["mu_l"], prm["w_up"], prm["w0"], prm["a_up"], prm["a0"],
      prm["g_up"], prm["k_k"], prm["k_a"], prm["r_k"], prm["gn_w"], prm["gn_b"])


def _pad_cols(w, n):
    return jnp.pad(w, ((0, 0),) * (w.ndim - 1) + ((0, n - w.shape[-1]),))


def _pack_w_in(w_in):
    d_att3 = 3 * D_ATT
    o = d_att3 + 3 * D_RWKV
    parts = [
        w_in[..., :o],
        _pad_cols(w_in[..., o:o + RWKV_DECAY_LORA], LORA_PAD),
        _pad_cols(w_in[..., o + RWKV_DECAY_LORA:o + RWKV_DECAY_LORA + RWKV_AAA_LORA], LORA_PAD),
        w_in[..., o + RWKV_DECAY_LORA + RWKV_AAA_LORA:],
    ]
    return jnp.concatenate(parts, axis=-1).astype(BF16)


def _pack_mu(mu):
    o = 3 * D_RWKV
    mu_l = jnp.concatenate([
        jnp.pad(mu[o:o + RWKV_DECAY_LORA], (0, LORA_PAD - RWKV_DECAY_LORA)),
        jnp.pad(mu[o + RWKV_DECAY_LORA:o + RWKV_DECAY_LORA + RWKV_AAA_LORA], (0, LORA_PAD - RWKV_AAA_LORA)),
        mu[o + RWKV_DECAY_LORA + RWKV_AAA_LORA:],
    ])
    return mu[None, :o], mu_l[None, :]


def _pad_rows(w, n):
    return jnp.pad(w, ((0, n - w.shape[0]), (0, 0)))


def _rope_tables(seq):
    inv_freq = ROPE_THETA ** (-jnp.arange(0, HEAD_DIM, 2, dtype=F32) / HEAD_DIM)
    ang = jnp.arange(seq, dtype=F32)[:, None] * inv_freq[None, :]
    cos, sin = jnp.cos(ang), jnp.sin(ang)
    reps = LANES // HEAD_DIM
    cos_t = jnp.tile(jnp.concatenate([cos, cos], axis=1), (1, reps))
    sin_t = jnp.tile(jnp.concatenate([-sin, sin], axis=1), (1, reps))
    return cos_t, sin_t


def _tile(n, pref):
    t = min(n, pref)
    assert n % t == 0
    return t


def kernel(x, ffn1_norm_pre, ffn1_norm_post, ffn1_w_gate, ffn1_w_up, ffn1_w_down, mix_norm_pre, mix_norm_post, w_in, w_out, rwkv_mu, rwkv_w_up, rwkv_w0, rwkv_a_up, rwkv_a0, rwkv_g_up, rwkv_k_k, rwkv_k_a, rwkv_r_k, rwkv_gn_w, rwkv_gn_b, pool_w, pool_scale, ffn2_norm_pre, ffn2_norm_post, ffn2_w_gate, ffn2_w_up, ffn2_w_down):
    b, s, d = x.shape
    depth = w_in.shape[0]
    t = b * s
    d_ff = ffn1_w_gate.shape[2]
    assert s % RWKV_CHUNK == 0 and b % RWKV_ROWS == 0 and d == D_ATT + D_RWKV + D_POOL
    tm = _tile(t, 512)
    tm_proj = _tile(t, 1024)
    tf = _tile(d_ff, 512)
    tn = _tile(D_PACK, D_PACK // 4)
    cos_t, sin_t = _rope_tables(s)
    x2 = x.reshape(t, d).astype(F32)

    tr = _tile(d, 256)
    ffn_w = [[_cast_bf16(w, tr=_tile(w.shape[1], tr)) for w in ws] for ws in
             ((ffn1_w_gate, ffn1_w_up, ffn1_w_down), (ffn2_w_gate, ffn2_w_up, ffn2_w_down))]
    ffn_g = [[g[:, None, :] for g in gs] for gs in
             ((ffn1_norm_pre, ffn1_norm_post), (ffn2_norm_pre, ffn2_norm_post))]
    w_in_p = _pack_w_in(_cast_bf16(w_in, tr=tr))
    w_out_b = _cast_bf16(w_out, tr=tr)
    mix_g_pre, mix_g_post = mix_norm_pre[:, None, :], mix_norm_post[:, None, :]

    for i in range(depth):
        x2 = _ffn(x2, *ffn_g[0], *ffn_w[0], layer=i, tm=tm, tf=tf)

        z = _proj(x2, mix_g_pre, w_in_p, layer=i, tm=tm_proj, tn=tn)
        z3 = z.reshape(b, s, D_PACK)
        y_att = _attention(z3, cos_t, sin_t)
        mu_r, mu_l = _pack_mu(rwkv_mu[i])
        prm = dict(
            mu_r=mu_r, mu_l=mu_l,
            w_up=_pad_rows(rwkv_w_up[i], LORA_PAD).astype(BF16), w0=rwkv_w0[i][None],
            a_up=_pad_rows(rwkv_a_up[i], LORA_PAD).astype(BF16), a0=rwkv_a0[i][None],
            g_up=rwkv_g_up[i].astype(BF16), k_k=rwkv_k_k[i][None], k_a=rwkv_k_a[i][None],
            r_k=rwkv_r_k[i].reshape(1, D_RWKV), gn_w=rwkv_gn_w[i][None], gn_b=rwkv_gn_b[i][None])
        y_rwkv = _rwkv(z3, prm, rows=RWKV_ROWS)
        y_pool = _pool(z3, pool_w[i].astype(BF16), pool_scale[i][None])

        x2 = _mixout(x2, y_att.reshape(t, D_ATT), y_rwkv.reshape(t, D_RWKV), y_pool.reshape(t, D_POOL),
                     w_out_b, mix_g_post, layer=i, tm=tm)

        x2 = _ffn(x2, *ffn_g[1], *ffn_w[1], layer=i, tm=tm, tf=tf)
    return x2.reshape(b, s, d).astype(x.dtype)
```

```python
import functools
import math

import jax
import jax.numpy as jnp
from jax import lax
from jax.experimental import pallas as pl
from jax.experimental.pallas import tpu as pltpu

F32 = jnp.float32
BF16 = jnp.bfloat16

HEAD_DIM = 64
N_ATT_HEADS = 12
N_RWKV_HEADS = 12
D_ATT = N_ATT_HEADS * HEAD_DIM
D_RWKV = N_RWKV_HEADS * HEAD_DIM
D_POOL = 512
N_POOL_GROUPS = 4
POOL_GROUP = D_POOL // N_POOL_GROUPS
POOL_WINDOWS = (2, 4, 8, 16)
DILATED_BRANCHES = ((128, 1), (512, 4), (2048, 16))
ROPE_THETA = 10000.0
RWKV_DECAY_LORA = 96
RWKV_AAA_LORA = 96
RWKV_GATE_LORA = 256
NORM_EPS = 1e-6
RWKV_GN_EPS = 64e-5

LANES = 128
LORA_PAD = 128
COL_ATT = 0
COL_RKV = 3 * D_ATT
COL_LORA = COL_RKV + 3 * D_RWKV
D_LORA_PACK = 2 * LORA_PAD + RWKV_GATE_LORA
COL_POOL = COL_LORA + D_LORA_PACK
D_PACK = COL_POOL + D_POOL

SEG_SUM_TERMS = 1
RWKV_ROWS = 2
RWKV_CHUNK = 64
ATT_KV_TILE = 512
ATT_GROUP = 2
VMEM_LIMIT = 56 * 1024 * 1024

NEG_BIG = -1e30
LOG2_E = math.log2(math.e)


def _rms(xf, g):
    return xf * lax.rsqrt(jnp.mean(xf * xf, axis=-1, keepdims=True) + NORM_EPS) * g


def _dot(a, b):
    return jnp.dot(a, b, preferred_element_type=F32)


def _dot_nt(a, b):
    return lax.dot_general(a, b, (((1,), (1,)), ((), ())), preferred_element_type=F32)


def _ffn_kernel(x_ref, gpre_ref, gpost_ref, wg_ref, wu_ref, wd_ref, o_ref, h_ref, acc_ref):
    j = pl.program_id(1)

    @pl.when(j == 0)
    def _():
        h_ref[...] = _rms(x_ref[...], gpre_ref[...]).astype(BF16)
        acc_ref[...] = jnp.zeros_like(acc_ref)

    h = h_ref[...]
    g = _dot(h, wg_ref[...])
    u = _dot(h, wu_ref[...])
    a = (g * jax.nn.sigmoid(g) * u).astype(BF16)
    acc_ref[...] += _dot(a, wd_ref[...])

    @pl.when(j == pl.num_programs(1) - 1)
    def _():
        o_ref[...] = x_ref[...] + 0.5 * _rms(acc_ref[...], gpost_ref[...])


def _ffn(x, g_pre, g_post, wg, wu, wd, *, layer, tm, tf):
    t, d = x.shape
    f = wg.shape[2]
    return pl.pallas_call(
        _ffn_kernel,
        grid=(t // tm, f // tf),
        in_specs=[
            pl.BlockSpec((tm, d), lambda i, j: (i, 0)),
            pl.BlockSpec((None, 1, d), lambda i, j: (layer, 0, 0)),
            pl.BlockSpec((None, 1, d), lambda i, j: (layer, 0, 0)),
            pl.BlockSpec((None, d, tf), lambda i, j: (layer, 0, j)),
            pl.BlockSpec((None, d, tf), lambda i, j: (layer, 0, j)),
            pl.BlockSpec((None, tf, d), lambda i, j: (layer, j, 0)),
        ],
        out_specs=pl.BlockSpec((tm, d), lambda i, j: (i, 0)),
        out_shape=jax.ShapeDtypeStruct((t, d), F32),
        scratch_shapes=[pltpu.VMEM((tm, d), BF16), pltpu.VMEM((tm, d), F32)],
        compiler_params=pltpu.CompilerParams(
            dimension_semantics=("parallel", "arbitrary"), vmem_limit_bytes=VMEM_LIMIT),
        name="ffn",
    )(x, g_pre, g_post, wg, wu, wd)


def _proj_kernel(x_ref, g_ref, w_ref, o_ref, h_ref):
    @pl.when(pl.program_id(1) == 0)
    def _():
        h_ref[...] = _rms(x_ref[...], g_ref[...]).astype(BF16)

    o_ref[...] = _dot(h_ref[...], w_ref[...])


def _proj(x, g, w, *, layer, tm, tn):
    t, d = x.shape
    n = w.shape[2]
    return pl.pallas_call(
        _proj_kernel,
        grid=(t // tm, n // tn),
        in_specs=[
            pl.BlockSpec((tm, d), lambda i, j: (i, 0)),
            pl.BlockSpec((None, 1, d), lambda i, j: (layer, 0, 0)),
            pl.BlockSpec((None, d, tn), lambda i, j: (layer, 0, j)),
        ],
        out_specs=pl.BlockSpec((tm, tn), lambda i, j: (i, j)),
        out_shape=jax.ShapeDtypeStruct((t, n), F32),
        scratch_shapes=[pltpu.VMEM((tm, d), BF16)],
        compiler_params=pltpu.CompilerParams(
            dimension_semantics=("parallel", "arbitrary"), vmem_limit_bytes=VMEM_LIMIT),
        name="proj_in",
    )(x, g, w)


def _mixout_kernel(x_ref, ya_ref, yr_ref, yp_ref, wa_ref, wr_ref, wp_ref, g_ref, o_ref):
    mix = _dot(ya_ref[...], wa_ref[...]) + _dot(yr_ref[...], wr_ref[...]) + _dot(yp_ref[...], wp_ref[...])
    o_ref[...] = x_ref[...] + _rms(mix, g_ref[...])


def _mixout(x, ya, yr, yp, wo, g, *, layer, tm):
    t, d = x.shape
    da, dr, dp = ya.shape[1], yr.shape[1], yp.shape[1]
    assert da == dr and (da + dr) % dp == 0
    return pl.pallas_call(
        _mixout_kernel,
        grid=(t // tm,),
        in_specs=[
            pl.BlockSpec((tm, d), lambda i: (i, 0)),
            pl.BlockSpec((tm, da), lambda i: (i, 0)),
            pl.BlockSpec((tm, dr), lambda i: (i, 0)),
            pl.BlockSpec((tm, dp), lambda i: (i, 0)),
            pl.BlockSpec((None, da, d), lambda i: (layer, 0, 0)),
            pl.BlockSpec((None, dr, d), lambda i: (layer, 1, 0)),
            pl.BlockSpec((None, dp, d), lambda i: (layer, (da + dr) // dp, 0)),
            pl.BlockSpec((None, 1, d), lambda i: (layer, 0, 0)),
        ],
        out_specs=pl.BlockSpec((tm, d), lambda i: (i, 0)),
        out_shape=jax.ShapeDtypeStruct((t, d), F32),
        compiler_params=pltpu.CompilerParams(
            dimension_semantics=("parallel",), vmem_limit_bytes=VMEM_LIMIT),
        name="proj_out",
    )(x, ya, yr, yp, wo, wo, wo, g)


def _cast_kernel(w_ref, o_ref):
    o_ref[...] = w_ref[...].astype(BF16)


def _cast_bf16(w, *, layer, tr):
    _, r, c = w.shape
    return pl.pallas_call(
        _cast_kernel,
        grid=(r // tr,),
        in_specs=[pl.BlockSpec((1, tr, c), lambda j: (layer, j, 0))],
        out_specs=pl.BlockSpec((1, tr, c), lambda j: (0, j, 0)),
        out_shape=jax.ShapeDtypeStruct((1, r, c), BF16),
        compiler_params=pltpu.CompilerParams(
            dimension_semantics=("parallel",), vmem_limit_bytes=VMEM_LIMIT),
        name="cast_bf16",
    )(w)


def _att_bias_tile(base, tq, tk):
    row = lax.broadcasted_iota(jnp.int32, (tq, tk), 0)
    col = lax.broadcasted_iota(jnp.int32, (tq, tk), 1)
    delta = base + row - col
    cnt = jnp.zeros((tq, tk), jnp.int32)
    for window, dil in DILATED_BRANCHES:
        hit = (delta >= 0) & (delta <= window) & ((delta & (dil - 1)) == 0)
        cnt = cnt + hit.astype(jnp.int32)
    return jnp.where(cnt == 3, math.log2(3.0),
                     jnp.where(cnt == 2, 1.0, jnp.where(cnt == 1, 0.0, NEG_BIG))).astype(F32)


def _att_kernel(q_ref, k_ref, v_ref, cos_ref, sin_ref, o_ref,
                qa_s, qb_s, k_s, va_s, vb_s, bias_s, *, seq, tq, tk, group):
    nq = seq // tq
    ratio = tk // tq

    @pl.when(pl.program_id(1) == 0)
    def _():
        for off in range(nq):
            bias_s[off] = _att_bias_tile(off * tq, tq, tk)

    lane = lax.broadcasted_iota(jnp.int32, (tq, LANES), 1)
    first_half = (lane & (HEAD_DIM - 1)) < (HEAD_DIM // 2)
    head_a = lane < HEAD_DIM

    def rope(x, cos, sin):
        swapped = jnp.where(first_half, pltpu.roll(x, LANES - HEAD_DIM // 2, 1), pltpu.roll(x, HEAD_DIM // 2, 1))
        return x * cos + swapped * sin

    def prep(i, carry):
        rows = pl.ds(pl.multiple_of(i * tq, tq), tq)
        cos = cos_ref[rows, :]
        sin = sin_ref[rows, :]
        q = rope(q_ref[0, rows, :], cos, sin) * (HEAD_DIM ** -0.5 * LOG2_E)
        k = rope(k_ref[0, rows, :], cos, sin)
        v = v_ref[0, rows, :]
        qa_s[rows, :] = jnp.where(head_a, q, 0.0).astype(BF16)
        qb_s[rows, :] = jnp.where(head_a, 0.0, q).astype(BF16)
        k_s[rows, :] = k.astype(BF16)
        va_s[rows, :] = jnp.where(head_a, v, 1.0).astype(BF16)
        vb_s[rows, :] = jnp.where(head_a, 1.0, v).astype(BF16)
        return carry

    lax.fori_loop(0, nq, prep, 0)

    def tile_max(s):
        m = s[:, 0:LANES]
        for t in range(1, s.shape[1] // LANES):
            m = jnp.maximum(m, s[:, t * LANES:(t + 1) * LANES])
        return m

    q_s = (qa_s, qb_s)
    v_s = (va_s, vb_s)
    m_run = {}
    acc = {}
    for kj in range(seq // tk):
        blocks = list(range(ratio * kj, nq))
        for g0 in range(0, len(blocks), group):
            chains = [(qi, hd) for qi in blocks[g0:g0 + group] for hd in range(2)]
            width = {qi: (tq if qi == ratio * kj else tk) for qi, _ in chains}
            s = {}
            for qi, hd in chains:
                krows = slice(kj * tk, kj * tk + width[qi])
                s[qi, hd] = (_dot_nt(q_s[hd][qi * tq:(qi + 1) * tq, :], k_s[krows, :])
                             + bias_s[qi - ratio * kj, :, 0:width[qi]])
            p = {}
            for ch in chains:
                m_tile = jnp.max(tile_max(s[ch]), axis=-1, keepdims=True)
                if kj == 0:
                    m_new = m_tile
                else:
                    m_new = jnp.maximum(m_run[ch], m_tile)
                    acc[ch] = jnp.exp2(m_run[ch] - m_new) * acc[ch]
                m_run[ch] = m_new
                p[ch] = jnp.exp2(s[ch] - m_new).astype(BF16)
            for ch in chains:
                pv = _dot(p[ch], v_s[ch[1]][kj * tk:kj * tk + width[ch[0]], :])
                acc[ch] = pv if kj == 0 else acc[ch] + pv

    for qi in range(nq):
        o_a = acc[qi, 0] / pltpu.roll(acc[qi, 0], HEAD_DIM, 1)
        o_b = acc[qi, 1] / pltpu.roll(acc[qi, 1], HEAD_DIM, 1)
        o_ref[0, qi * tq:(qi + 1) * tq, :] = jnp.where(head_a, o_a, o_b).astype(o_ref.dtype)


def _attention(z3, cos_t, sin_t):
    b, s, _ = z3.shape
    tk = min(ATT_KV_TILE, s)
    tq = tk // 2
    npair = D_ATT // LANES
    kern = functools.partial(_att_kernel, seq=s, tq=tq, tk=tk, group=ATT_GROUP)
    return pl.pallas_call(
        kern,
        grid=(b, npair),
        in_specs=[
            pl.BlockSpec((1, s, LANES), lambda i, p: (i, 0, COL_ATT // LANES + p)),
            pl.BlockSpec((1, s, LANES), lambda i, p: (i, 0, (COL_ATT + D_ATT) // LANES + p)),
            pl.BlockSpec((1, s, LANES), lambda i, p: (i, 0, (COL_ATT + 2 * D_ATT) // LANES + p)),
            pl.BlockSpec((s, LANES), lambda i, p: (0, 0)),
            pl.BlockSpec((s, LANES), lambda i, p: (0, 0)),
        ],
        out_specs=pl.BlockSpec((1, s, LANES), lambda i, p: (i, 0, p)),
        out_shape=jax.ShapeDtypeStruct((b, s, D_ATT), BF16),
        scratch_shapes=[pltpu.VMEM((s, LANES), BF16)] * 5 + [pltpu.VMEM((s // tq, tq, tk), F32)],
        compiler_params=pltpu.CompilerParams(
            dimension_semantics=("parallel", "arbitrary"), vmem_limit_bytes=VMEM_LIMIT),
        name="dilated_attention",
    )(z3, z3, z3, cos_t, sin_t)


POOL_PAD = 16


def _pool_kernel(u_ref, w_ref, sc_ref, o_ref, pad_s, *, seq):
    t1 = (lax.broadcasted_iota(jnp.int32, (seq, POOL_GROUP), 0) + 1).astype(F32)
    pad_s[0:POOL_PAD, :] = jnp.zeros((POOL_PAD, POOL_GROUP), F32)
    for gi, w in enumerate(POOL_WINDOWS):
        cols = slice(gi * POOL_GROUP, (gi + 1) * POOL_GROUP)
        u = u_ref[0, :, cols]
        s = u
        k = 1
        while k < w:
            pad_s[POOL_PAD:POOL_PAD + seq, :] = s
            s = s + pad_s[POOL_PAD - k:POOL_PAD - k + seq, :]
            k *= 2
        p = s / jnp.minimum(t1, float(w)) - u
        y = _dot(p.astype(BF16), w_ref[gi])
        o_ref[0, :, cols] = (y * sc_ref[:, cols]).astype(o_ref.dtype)


def _pool(z3, pool_w, pool_scale):
    b, s, _ = z3.shape
    kern = functools.partial(_pool_kernel, seq=s)
    return pl.pallas_call(
        kern,
        grid=(b,),
        in_specs=[
            pl.BlockSpec((1, s, D_POOL), lambda i: (i, 0, COL_POOL // D_POOL)),
            pl.BlockSpec(pool_w.shape, lambda i: (0, 0, 0)),
            pl.BlockSpec((1, D_POOL), lambda i: (0, 0)),
        ],
        out_specs=pl.BlockSpec((1, s, D_POOL), lambda i: (i, 0, 0)),
        out_shape=jax.ShapeDtypeStruct((b, s, D_POOL), BF16),
        scratch_shapes=[pltpu.VMEM((s + POOL_PAD, POOL_GROUP), F32)],
        compiler_params=pltpu.CompilerParams(
            dimension_semantics=("parallel",), vmem_limit_bytes=VMEM_LIMIT),
        name="pool_mixer",
    )(z3, pool_w, pool_scale)


def _split_dot(x, ones_b, terms):
    acc = None
    rem = x
    for _ in range(terms):
        piece = rem.astype(BF16)
        part = _dot(piece, ones_b)
        acc = part if acc is None else acc + part
        rem = rem - piece.astype(F32)
    return acc


N_RWKV_IN = 16


def _rwkv_kernel(*refs, chunk, rows, cast_every_step):
    (zr_ref, zrp_ref, zl_ref, zlp_ref, mur_ref, mul_ref, wup_ref, w0_ref, aup_ref, a0_ref,
     gup_ref, kk_ref, ka_ref, rk_ref, gnw_ref, gnb_ref) = refs[:N_RWKV_IN]
    n_cast = len(cast_every_step)
    cast_in = refs[N_RWKV_IN:N_RWKV_IN + n_cast]
    o_ref = refs[N_RWKV_IN + n_cast]
    cast_out = refs[N_RWKV_IN + n_cast + 1:N_RWKV_IN + 2 * n_cast + 1]
    h_s, sh_s = refs[N_RWKV_IN + 2 * n_cast + 1:]
    c = pl.program_id(1)
    npair = D_RWKV // LANES
    halo = 8
    wr = 3 * D_RWKV

    for src, dst, every in zip(cast_in, cast_out, cast_every_step):
        if every:
            dst[...] = src[...].astype(BF16)
        else:
            @pl.when(pl.program_id(0) == 0)
            def _(src=src, dst=dst):
                dst[...] = src[...].astype(BF16)

    @pl.when(c == 0)
    def _():
        h_s[...] = jnp.zeros_like(h_s)

    row = lax.broadcasted_iota(jnp.int32, (LANES, LANES), 0)
    col = lax.broadcasted_iota(jnp.int32, (LANES, LANES), 1)
    head_ones = ((row < HEAD_DIM) == (col < HEAD_DIM)).astype(BF16)
    trow = lax.broadcasted_iota(jnp.int32, (chunk, chunk), 0)
    tcol = lax.broadcasted_iota(jnp.int32, (chunk, chunk), 1)
    tri = (trow >= tcol).astype(BF16)
    lane = lax.broadcasted_iota(jnp.int32, (chunk, LANES), 1)
    head_a = lane < HEAD_DIM
    strict_lower = row > col
    lower = row >= col
    eye = (row == col).astype(F32)
    first = c == 0

    def seg_sum(x):
        return jnp.concatenate(
            [_split_dot(x[:, p * LANES:(p + 1) * LANES], head_ones, SEG_SUM_TERMS) for p in range(npair)], axis=1)

    def stack2(x):
        return jnp.concatenate([jnp.where(head_a, x, 0.0), jnp.where(head_a, 0.0, x)], axis=0)

    tok = []
    for bi in range(rows):
        zr = zr_ref[bi]
        zl = zl_ref[bi]
        sh_s[bi, 0:halo, 0:wr] = jnp.where(first, 0.0, zrp_ref[bi])
        sh_s[bi, 0:halo, wr:wr + D_LORA_PACK] = jnp.where(first, 0.0, zlp_ref[bi])
        sh_s[bi, halo:halo + chunk, 0:wr] = zr
        sh_s[bi, halo:halo + chunk, wr:wr + D_LORA_PACK] = zl
        zr_prev = sh_s[bi, halo - 1:halo - 1 + chunk, 0:wr]
        zl_prev = sh_s[bi, halo - 1:halo - 1 + chunk, wr:wr + D_LORA_PACK]
        zr = zr + (zr_prev - zr) * mur_ref[...]
        zl = zl + (zl_prev - zl) * mul_ref[...]
        r = zr[:, 0:D_RWKV]
        k = zr[:, D_RWKV:2 * D_RWKV]
        v = zr[:, 2 * D_RWKV:3 * D_RWKV]
        zw = zl[:, 0:LORA_PAD]
        za = zl[:, LORA_PAD:2 * LORA_PAD]
        zg = zl[:, 2 * LORA_PAD:D_LORA_PACK]

        lw = w0_ref[...] + _dot(jnp.tanh(zw).astype(BF16), wup_ref[...])
        nlw = -lw
        softplus = jnp.maximum(nlw, 0.0) + jnp.log(1.0 + jnp.exp(-jnp.abs(nlw)))
        ld = -jnp.exp(-softplus - 0.5)
        a = jax.nn.sigmoid(a0_ref[...] + _dot(za.astype(BF16), aup_ref[...]))
        g = _dot(jax.nn.sigmoid(zg).astype(BF16), gup_ref[...])
        kk = k * kk_ref[...]
        k2 = k * (1.0 + (a - 1.0) * ka_ref[...])
        kk = kk / jnp.maximum(jnp.sqrt(seg_sum(kk * kk)), 1e-12)
        bonus = seg_sum(r * k2 * rk_ref[...]) * v
        bb = kk * a

        cum = _split_dot_left(tri, ld, 3)
        cum_end = cum[chunk - 1:chunk, :]
        e_inv = jnp.exp(-cum)
        e_end = jnp.exp(cum_end - cum)
        tok.append(dict(
            r_t=r * jnp.exp(cum), kk_t=kk * jnp.exp(cum - ld), b_h=bb * e_inv, k_h=k2 * e_inv,
            b_a=bb * e_end, k_a=k2 * e_end, a_end=jnp.exp(cum_end), v=v, bonus=bonus, g=g))

    units = [(bi, p) for bi in range(rows) for p in range(npair)]

    def pair(name):
        return [tok[bi][name][:, p * LANES:(p + 1) * LANES] for bi, p in units]

    kks = [stack2(x).astype(BF16) for x in pair("kk_t")]
    bs = [stack2(x).astype(BF16) for x in pair("b_h")]
    ks = [stack2(x).astype(BF16) for x in pair("k_h")]
    rs = [stack2(x).astype(BF16) for x in pair("r_t")]
    vs = [stack2(x).astype(BF16) for x in pair("v")]
    bat = [stack2(x).T.astype(BF16) for x in pair("b_a")]
    kat = [stack2(x).T.astype(BF16) for x in pair("k_a")]
    a_col = [jnp.broadcast_to(x, (LANES, LANES)).T for x in pair("a_end")]
    ids = range(len(units))

    tt = [_dot_nt(jnp.concatenate([kks[i], rs[i]], axis=0), jnp.concatenate([bs[i], ks[i]], axis=0)) for i in ids]
    l_b = [jnp.where(strict_lower, tt[i][0:LANES, 0:LANES], 0.0) for i in ids]
    l_k = [jnp.where(strict_lower, tt[i][0:LANES, LANES:2 * LANES], 0.0).astype(BF16) for i in ids]
    m_b = [jnp.where(lower, tt[i][LANES:2 * LANES, 0:LANES], 0.0).astype(BF16) for i in ids]
    m_k = [jnp.where(lower, tt[i][LANES:2 * LANES, LANES:2 * LANES], 0.0).astype(BF16) for i in ids]

    sel = ((row ^ col) < 2) & ((row & 1) != 0) & ((col & 1) == 0)
    t_inv = [eye - jnp.where(sel, l_b[i], 0.0) for i in ids]
    s = 2
    while s < HEAD_DIM:
        sel = ((row ^ col) < 2 * s) & ((row & s) != 0) & ((col & s) == 0)
        t_b = [t_inv[i].astype(BF16) for i in ids]
        inner = [_dot(jnp.where(sel, l_b[i], 0.0).astype(BF16), t_b[i]).astype(BF16) for i in ids]
        t_inv = [t_inv[i] - _dot(t_b[i], inner[i]) for i in ids]
        s *= 2
    t_b = [t_inv[i].astype(BF16) for i in ids]

    mlv = [_dot(jnp.concatenate([m_k[i], l_k[i]], axis=0), vs[i]) for i in ids]
    y0 = [mlv[i][0:LANES, :] for i in ids]
    lkv = [mlv[i][LANES:2 * LANES, :].astype(BF16) for i in ids]
    w_m = [_dot(t_b[i], kks[i]).astype(BF16) for i in ids]

    h = [h_s[i] for i in ids]
    h_b = [h[i].astype(BF16) for i in ids]
    u_b = [(-_dot(jnp.concatenate([w_m[i], t_b[i]], axis=1),
                  jnp.concatenate([h_b[i], lkv[i]], axis=0))).astype(BF16) for i in ids]
    y = [_dot(jnp.concatenate([rs[i], m_b[i]], axis=1), jnp.concatenate([h_b[i], u_b[i]], axis=0)) + y0[i]
         for i in ids]
    for i in ids:
        h_s[i] = a_col[i] * h[i] + _dot(jnp.concatenate([bat[i], kat[i]], axis=1),
                                        jnp.concatenate([u_b[i], vs[i]], axis=0))

    for bi in range(rows):
        yb = jnp.concatenate(
            [y[bi * npair + p][0:chunk, :] + y[bi * npair + p][chunk:2 * chunk, :] for p in range(npair)], axis=1)
        mean = seg_sum(yb) * (1.0 / HEAD_DIM)
        d = yb - mean
        var = seg_sum(d * d) * (1.0 / HEAD_DIM)
        yn = d * lax.rsqrt(var + RWKV_GN_EPS) * gnw_ref[...] + gnb_ref[...]
        o_ref[bi] = ((yn + tok[bi]["bonus"]) * tok[bi]["g"]).astype(o_ref.dtype)


def _split_dot_left(tri_b, x, terms):
    acc = None
    rem = x
    for _ in range(terms):
        piece = rem.astype(BF16)
        part = _dot(tri_b, piece)
        acc = part if acc is None else acc + part
        rem = rem - piece.astype(F32)
    return acc


BF16_SUBLANES = 16


def _rwkv(z3, prm, *, rows, cast=()):
    b, s, _ = z3.shape
    chunk = RWKV_CHUNK
    nc = s // chunk
    nb = b // rows
    wr = 3 * D_RWKV
    rows8 = chunk // 8
    npair = D_RWKV // LANES

    def vec(n):
        return pl.BlockSpec((1, n), lambda i, c: (0, 0))

    def full(a):
        return pl.BlockSpec(a.shape, lambda i, c: (0, 0))

    cast_in_specs, cast_out_specs, cast_shapes, cast_every = [], [], [], []
    for w, layer in cast:
        _, r, cc = w.shape
        every = r % (nb * nc * BF16_SUBLANES) == 0
        tr = r // (nb * nc) if every else r // nc
        assert tr % BF16_SUBLANES == 0 and tr * (nb * nc if every else nc) == r
        if every:
            row_blk = lambda i, c: i * nc + c
        else:
            row_blk = lambda i, c: jnp.where(i == 0, c, nc - 1)
        cast_in_specs.append(pl.BlockSpec((1, tr, cc), lambda i, c, f=row_blk, l=layer: (l, f(i, c), 0)))
        cast_out_specs.append(pl.BlockSpec((1, tr, cc), lambda i, c, f=row_blk: (0, f(i, c), 0)))
        cast_shapes.append(jax.ShapeDtypeStruct((1, r, cc), BF16))
        cast_every.append(every)
    kern = functools.partial(_rwkv_kernel, chunk=chunk, rows=rows, cast_every_step=tuple(cast_every))

    outs = pl.pallas_call(
        kern,
        grid=(nb, nc),
        in_specs=[
            pl.BlockSpec((rows, chunk, wr), lambda i, c: (i, c, COL_RKV // wr)),
            pl.BlockSpec((rows, 8, wr), lambda i, c: (i, jnp.maximum(c * rows8 - 1, 0), COL_RKV // wr)),
            pl.BlockSpec((rows, chunk, D_LORA_PACK), lambda i, c: (i, c, COL_LORA // D_LORA_PACK)),
            pl.BlockSpec((rows, 8, D_LORA_PACK),
                         lambda i, c: (i, jnp.maximum(c * rows8 - 1, 0), COL_LORA // D_LORA_PACK)),
            vec(wr), vec(D_LORA_PACK),
            full(prm["w_up"]), vec(D_RWKV), full(prm["a_up"]), vec(D_RWKV), full(prm["g_up"]),
            vec(D_RWKV), vec(D_RWKV), vec(D_RWKV), vec(D_RWKV), vec(D_RWKV),
        ] + cast_in_specs,
        out_specs=[pl.BlockSpec((rows, chunk, D_RWKV), lambda i, c: (i, c, 0))] + cast_out_specs,
        out_shape=[jax.ShapeDtypeStruct((b, s, D_RWKV), BF16)] + cast_shapes,
        scratch_shapes=[
            pltpu.VMEM((rows * npair, LANES, LANES), F32),
            pltpu.VMEM((rows, chunk + 8, wr + D_LORA_PACK), F32),
        ],
        compiler_params=pltpu.CompilerParams(
            dimension_semantics=("arbitrary", "arbitrary"), vmem_limit_bytes=VMEM_LIMIT),
        name="rwkv7",
    )(z3, z3, z3, z3, prm["mu_r"], prm["mu_l"], prm["w_up"], prm["w0"], prm["a_up"], prm["a0"],
      prm["g_up"], prm["k_k"], prm["k_a"], prm["r_k"], prm["gn_w"], prm["gn_b"], *[w for w, _ in cast])
    return outs[0], list(outs[1:])


def _pad_cols(w, n):
    return jnp.pad(w, ((0, 0),) * (w.ndim - 1) + ((0, n - w.shape[-1]),))


def _pack_w_in(w_in):
    d_att3 = 3 * D_ATT
    o = d_att3 + 3 * D_RWKV
    parts = [
        w_in[..., :o],
        _pad_cols(w_in[..., o:o + RWKV_DECAY_LORA], LORA_PAD),
        _pad_cols(w_in[..., o + RWKV_DECAY_LORA:o + RWKV_DECAY_LORA + RWKV_AAA_LORA], LORA_PAD),
        w_in[..., o + RWKV_DECAY_LORA + RWKV_AAA_LORA:],
    ]
    return jnp.concatenate(parts, axis=-1).astype(BF16)


def _pack_mu(mu):
    o = 3 * D_RWKV
    mu_l = jnp.concatenate([
        jnp.pad(mu[o:o + RWKV_DECAY_LORA], (0, LORA_PAD - RWKV_DECAY_LORA)),
        jnp.pad(mu[o + RWKV_DECAY_LORA:o + RWKV_DECAY_LORA + RWKV_AAA_LORA], (0, LORA_PAD - RWKV_AAA_LORA)),
        mu[o + RWKV_DECAY_LORA + RWKV_AAA_LORA:],
    ])
    return mu[None, :o], mu_l[None, :]


def _pad_rows(w, n):
    return jnp.pad(w, ((0, n - w.shape[0]), (0, 0)))


def _rope_tables(seq):
    inv_freq = ROPE_THETA ** (-jnp.arange(0, HEAD_DIM, 2, dtype=F32) / HEAD_DIM)
    ang = jnp.arange(seq, dtype=F32)[:, None] * inv_freq[None, :]
    cos, sin = jnp.cos(ang), jnp.sin(ang)
    reps = LANES // HEAD_DIM
    cos_t = jnp.tile(jnp.concatenate([cos, cos], axis=1), (1, reps))
    sin_t = jnp.tile(jnp.concatenate([-sin, sin], axis=1), (1, reps))
    return cos_t, sin_t


def _tile(n, pref):
    t = min(n, pref)
    assert n % t == 0
    return t


def kernel(x, ffn1_norm_pre, ffn1_norm_post, ffn1_w_gate, ffn1_w_up, ffn1_w_down, mix_norm_pre, mix_norm_post, w_in, w_out, rwkv_mu, rwkv_w_up, rwkv_w0, rwkv_a_up, rwkv_a0, rwkv_g_up, rwkv_k_k, rwkv_k_a, rwkv_r_k, rwkv_gn_w, rwkv_gn_b, pool_w, pool_scale, ffn2_norm_pre, ffn2_norm_post, ffn2_w_gate, ffn2_w_up, ffn2_w_down):
    b, s, d = x.shape
    depth = w_in.shape[0]
    t = b * s
    d_ff = ffn1_w_gate.shape[2]
    assert s % RWKV_CHUNK == 0 and b % RWKV_ROWS == 0 and d == D_ATT + D_RWKV + D_POOL
    tm = _tile(t, 512)
    tm_proj = _tile(t, 1024)
    tf = _tile(d_ff, 512)
    tn = _tile(D_PACK, D_PACK // 4)
    cos_t, sin_t = _rope_tables(s)
    x2 = x.reshape(t, d).astype(F32)

    big = (ffn1_w_gate, ffn1_w_up, ffn1_w_down, ffn2_w_gate, ffn2_w_up, ffn2_w_down, w_in, w_out)
    tr = _tile(d, 256)
    wb = [_cast_bf16(w, layer=0, tr=_tile(w.shape[1], tr)) for w in big]
    ffn_g = [[g[:, None, :] for g in gs] for gs in
             ((ffn1_norm_pre, ffn1_norm_post), (ffn2_norm_pre, ffn2_norm_post))]
    mix_g_pre, mix_g_post = mix_norm_pre[:, None, :], mix_norm_post[:, None, :]

    def gains(gs, i):
        return [g[i:i + 1] for g in gs]

    for i in range(depth):
        x2 = _ffn(x2, *gains(ffn_g[0], i), *wb[0:3], layer=0, tm=tm, tf=tf)

        z = _proj(x2, mix_g_pre[i:i + 1], _pack_w_in(wb[6]), layer=0, tm=tm_proj, tn=tn)
        z3 = z.reshape(b, s, D_PACK)
        y_att = _attention(z3, cos_t, sin_t)
        mu_r, mu_l = _pack_mu(rwkv_mu[i])
        prm = dict(
            mu_r=mu_r, mu_l=mu_l,
            w_up=_pad_rows(rwkv_w_up[i], LORA_PAD).astype(BF16), w0=rwkv_w0[i][None],
            a_up=_pad_rows(rwkv_a_up[i], LORA_PAD).astype(BF16), a0=rwkv_a0[i][None],
            g_up=rwkv_g_up[i].astype(BF16), k_k=rwkv_k_k[i][None], k_a=rwkv_k_a[i][None],
            r_k=rwkv_r_k[i].reshape(1, D_RWKV), gn_w=rwkv_gn_w[i][None], gn_b=rwkv_gn_b[i][None])
        nxt = [(w, i + 1) for w in big] if i + 1 < depth else []
        y_rwkv, wb_next = _rwkv(z3, prm, rows=RWKV_ROWS, cast=nxt)
        y_pool = _pool(z3, pool_w[i].astype(BF16), pool_scale[i][None])

        x2 = _mixout(x2, y_att.reshape(t, D_ATT), y_rwkv.reshape(t, D_RWKV), y_pool.reshape(t, D_POOL),
                     wb[7], mix_g_post[i:i + 1], layer=0, tm=tm)

        x2 = _ffn(x2, *gains(ffn_g[1], i), *wb[3:6], layer=0, tm=tm, tf=tf)
        wb = wb_next
    return x2.reshape(b, s, d).astype(x.dtype)
```

```python
import functools
import math

import jax
import jax.numpy as jnp
from jax import lax
from jax.experimental import pallas as pl
from jax.experimental.pallas import tpu as pltpu

F32 = jnp.float32
BF16 = jnp.bfloat16

HEAD_DIM = 64
N_ATT_HEADS = 12
N_RWKV_HEADS = 12
D_ATT = N_ATT_HEADS * HEAD_DIM
D_RWKV = N_RWKV_HEADS * HEAD_DIM
D_POOL = 512
N_POOL_GROUPS = 4
POOL_GROUP = D_POOL // N_POOL_GROUPS
POOL_WINDOWS = (2, 4, 8, 16)
DILATED_BRANCHES = ((128, 1), (512, 4), (2048, 16))
ROPE_THETA = 10000.0
RWKV_DECAY_LORA = 96
RWKV_AAA_LORA = 96
RWKV_GATE_LORA = 256
NORM_EPS = 1e-6
RWKV_GN_EPS = 64e-5

LANES = 128
LORA_PAD = 128
COL_ATT = 0
COL_RKV = 3 * D_ATT
COL_LORA = COL_RKV + 3 * D_RWKV
D_LORA_PACK = 2 * LORA_PAD + RWKV_GATE_LORA
COL_POOL = COL_LORA + D_LORA_PACK
D_PACK = COL_POOL + D_POOL

SEG_SUM_TERMS = 1
RWKV_ROWS = 2
RWKV_CHUNK = 64
ATT_KV_TILE = 512
ATT_GROUP = 2
VMEM_LIMIT = 56 * 1024 * 1024

NEG_BIG = -1e30
LOG2_E = math.log2(math.e)


def _rms(xf, g):
    return xf * lax.rsqrt(jnp.mean(xf * xf, axis=-1, keepdims=True) + NORM_EPS) * g


def _dot(a, b):
    return jnp.dot(a, b, preferred_element_type=F32)


def _dot_nt(a, b):
    return lax.dot_general(a, b, (((1,), (1,)), ((), ())), preferred_element_type=F32)


def _ffn_kernel(x_ref, gpre_ref, gpost_ref, wg_ref, wu_ref, wd_ref, o_ref, h_ref, acc_ref):
    j = pl.program_id(1)

    @pl.when(j == 0)
    def _():
        h_ref[...] = _rms(x_ref[...], gpre_ref[...]).astype(BF16)
        acc_ref[...] = jnp.zeros_like(acc_ref)

    h = h_ref[...]
    g = _dot(h, wg_ref[...])
    u = _dot(h, wu_ref[...])
    a = (g * jax.nn.sigmoid(g) * u).astype(BF16)
    acc_ref[...] += _dot(a, wd_ref[...])

    @pl.when(j == pl.num_programs(1) - 1)
    def _():
        o_ref[...] = x_ref[...] + 0.5 * _rms(acc_ref[...], gpost_ref[...])


def _ffn(x, g_pre, g_post, wg, wu, wd, *, layer, tm, tf):
    t, d = x.shape
    f = wg.shape[2]
    return pl.pallas_call(
        _ffn_kernel,
        grid=(t // tm, f // tf),
        in_specs=[
            pl.BlockSpec((tm, d), lambda i, j: (i, 0)),
            pl.BlockSpec((None, 1, d), lambda i, j: (layer, 0, 0)),
            pl.BlockSpec((None, 1, d), lambda i, j: (layer, 0, 0)),
            pl.BlockSpec((None, d, tf), lambda i, j: (layer, 0, j)),
            pl.BlockSpec((None, d, tf), lambda i, j: (layer, 0, j)),
            pl.BlockSpec((None, tf, d), lambda i, j: (layer, j, 0)),
        ],
        out_specs=pl.BlockSpec((tm, d), lambda i, j: (i, 0)),
        out_shape=jax.ShapeDtypeStruct((t, d), F32),
        scratch_shapes=[pltpu.VMEM((tm, d), BF16), pltpu.VMEM((tm, d), F32)],
        compiler_params=pltpu.CompilerParams(
            dimension_semantics=("parallel", "arbitrary"), vmem_limit_bytes=VMEM_LIMIT),
        name="ffn",
    )(x, g_pre, g_post, wg, wu, wd)


def _proj_kernel(x_ref, g_ref, wt_ref, o_ref, h_ref):
    @pl.when(pl.program_id(1) == 0)
    def _():
        h_ref[...] = _rms(x_ref[...], g_ref[...]).astype(BF16)

    o_ref[...] = _dot_nt(h_ref[...], wt_ref[...])


def _proj(x, g, wt, *, layer, tm, tn):
    t, d = x.shape
    n = wt.shape[1]
    return pl.pallas_call(
        _proj_kernel,
        grid=(t // tm, n // tn),
        in_specs=[
            pl.BlockSpec((tm, d), lambda i, j: (i, 0)),
            pl.BlockSpec((None, 1, d), lambda i, j: (layer, 0, 0)),
            pl.BlockSpec((None, tn, d), lambda i, j: (layer, j, 0)),
        ],
        out_specs=pl.BlockSpec((tm, tn), lambda i, j: (i, j)),
        out_shape=jax.ShapeDtypeStruct((t, n), F32),
        scratch_shapes=[pltpu.VMEM((tm, d), BF16)],
        compiler_params=pltpu.CompilerParams(
            dimension_semantics=("parallel", "arbitrary"), vmem_limit_bytes=VMEM_LIMIT),
        name="proj_in",
    )(x, g, wt)


def _mixout_kernel(x_ref, ya_ref, yr_ref, yp_ref, wa_ref, wr_ref, wp_ref, g_ref, o_ref):
    mix = _dot(ya_ref[...], wa_ref[...]) + _dot(yr_ref[...], wr_ref[...]) + _dot(yp_ref[...], wp_ref[...])
    o_ref[...] = x_ref[...] + _rms(mix, g_ref[...])


def _mixout(x, ya, yr, yp, wo, g, *, layer, tm):
    t, d = x.shape
    da, dr, dp = ya.shape[1], yr.shape[1], yp.shape[1]
    assert da == dr and (da + dr) % dp == 0
    return pl.pallas_call(
        _mixout_kernel,
        grid=(t // tm,),
        in_specs=[
            pl.BlockSpec((tm, d), lambda i: (i, 0)),
            pl.BlockSpec((tm, da), lambda i: (i, 0)),
            pl.BlockSpec((tm, dr), lambda i: (i, 0)),
            pl.BlockSpec((tm, dp), lambda i: (i, 0)),
            pl.BlockSpec((None, da, d), lambda i: (layer, 0, 0)),
            pl.BlockSpec((None, dr, d), lambda i: (layer, 1, 0)),
            pl.BlockSpec((None, dp, d), lambda i: (layer, (da + dr) // dp, 0)),
            pl.BlockSpec((None, 1, d), lambda i: (layer, 0, 0)),
        ],
        out_specs=pl.BlockSpec((tm, d), lambda i: (i, 0)),
        out_shape=jax.ShapeDtypeStruct((t, d), F32),
        compiler_params=pltpu.CompilerParams(
            dimension_semantics=("parallel",), vmem_limit_bytes=VMEM_LIMIT),
        name="proj_out",
    )(x, ya, yr, yp, wo, wo, wo, g)


def _cast_kernel(w_ref, o_ref):
    o_ref[...] = w_ref[...].astype(BF16)


def _cast_bf16(w, *, layer, tr):
    _, r, c = w.shape
    return pl.pallas_call(
        _cast_kernel,
        grid=(r // tr,),
        in_specs=[pl.BlockSpec((1, tr, c), lambda j: (layer, j, 0))],
        out_specs=pl.BlockSpec((1, tr, c), lambda j: (0, j, 0)),
        out_shape=jax.ShapeDtypeStruct((1, r, c), BF16),
        compiler_params=pltpu.CompilerParams(
            dimension_semantics=("parallel",), vmem_limit_bytes=VMEM_LIMIT),
        name="cast_bf16",
    )(w)


def _att_bias_tile(base, tq, tk):
    row = lax.broadcasted_iota(jnp.int32, (tq, tk), 0)
    col = lax.broadcasted_iota(jnp.int32, (tq, tk), 1)
    delta = base + row - col
    cnt = jnp.zeros((tq, tk), jnp.int32)
    for window, dil in DILATED_BRANCHES:
        hit = (delta >= 0) & (delta <= window) & ((delta & (dil - 1)) == 0)
        cnt = cnt + hit.astype(jnp.int32)
    return jnp.where(cnt == 3, math.log2(3.0),
                     jnp.where(cnt == 2, 1.0, jnp.where(cnt == 1, 0.0, NEG_BIG))).astype(F32)


def _att_kernel(q_ref, k_ref, v_ref, cos_ref, sin_ref, o_ref,
                qa_s, qb_s, k_s, va_s, vb_s, bias_s, *, seq, tq, tk, group):
    nq = seq // tq
    ratio = tk // tq

    @pl.when(pl.program_id(1) == 0)
    def _():
        for off in range(nq):
            bias_s[off] = _att_bias_tile(off * tq, tq, tk)

    lane = lax.broadcasted_iota(jnp.int32, (tq, LANES), 1)
    first_half = (lane & (HEAD_DIM - 1)) < (HEAD_DIM // 2)
    head_a = lane < HEAD_DIM

    def rope(x, cos, sin):
        swapped = jnp.where(first_half, pltpu.roll(x, LANES - HEAD_DIM // 2, 1), pltpu.roll(x, HEAD_DIM // 2, 1))
        return x * cos + swapped * sin

    def prep(i, carry):
        rows = pl.ds(pl.multiple_of(i * tq, tq), tq)
        cos = cos_ref[rows, :]
        sin = sin_ref[rows, :]
        q = rope(q_ref[0, rows, :], cos, sin) * (HEAD_DIM ** -0.5 * LOG2_E)
        k = rope(k_ref[0, rows, :], cos, sin)
        v = v_ref[0, rows, :]
        qa_s[rows, :] = jnp.where(head_a, q, 0.0).astype(BF16)
        qb_s[rows, :] = jnp.where(head_a, 0.0, q).astype(BF16)
        k_s[rows, :] = k.astype(BF16)
        va_s[rows, :] = jnp.where(head_a, v, 1.0).astype(BF16)
        vb_s[rows, :] = jnp.where(head_a, 1.0, v).astype(BF16)
        return carry

    lax.fori_loop(0, nq, prep, 0)

    def tile_max(s):
        m = s[:, 0:LANES]
        for t in range(1, s.shape[1] // LANES):
            m = jnp.maximum(m, s[:, t * LANES:(t + 1) * LANES])
        return m

    q_s = (qa_s, qb_s)
    v_s = (va_s, vb_s)
    m_run = {}
    acc = {}
    for kj in range(seq // tk):
        blocks = list(range(ratio * kj, nq))
        for g0 in range(0, len(blocks), group):
            chains = [(qi, hd) for qi in blocks[g0:g0 + group] for hd in range(2)]
            width = {qi: (tq if qi == ratio * kj else tk) for qi, _ in chains}
            s = {}
            for qi, hd in chains:
                krows = slice(kj * tk, kj * tk + width[qi])
                s[qi, hd] = (_dot_nt(q_s[hd][qi * tq:(qi + 1) * tq, :], k_s[krows, :])
                             + bias_s[qi - ratio * kj, :, 0:width[qi]])
            p = {}
            for ch in chains:
                m_tile = jnp.max(tile_max(s[ch]), axis=-1, keepdims=True)
                if kj == 0:
                    m_new = m_tile
                else:
                    m_new = jnp.maximum(m_run[ch], m_tile)
                    acc[ch] = jnp.exp2(m_run[ch] - m_new) * acc[ch]
                m_run[ch] = m_new
                p[ch] = jnp.exp2(s[ch] - m_new).astype(BF16)
            for ch in chains:
                pv = _dot(p[ch], v_s[ch[1]][kj * tk:kj * tk + width[ch[0]], :])
                acc[ch] = pv if kj == 0 else acc[ch] + pv

    for qi in range(nq):
        o_a = acc[qi, 0] / pltpu.roll(acc[qi, 0], HEAD_DIM, 1)
        o_b = acc[qi, 1] / pltpu.roll(acc[qi, 1], HEAD_DIM, 1)
        o_ref[0, qi * tq:(qi + 1) * tq, :] = jnp.where(head_a, o_a, o_b).astype(o_ref.dtype)


def _attention(z3, cos_t, sin_t):
    b, s, _ = z3.shape
    tk = min(ATT_KV_TILE, s)
    tq = tk // 2
    npair = D_ATT // LANES
    kern = functools.partial(_att_kernel, seq=s, tq=tq, tk=tk, group=ATT_GROUP)
    return pl.pallas_call(
        kern,
        grid=(b, npair),
        in_specs=[
            pl.BlockSpec((1, s, LANES), lambda i, p: (i, 0, COL_ATT // LANES + p)),
            pl.BlockSpec((1, s, LANES), lambda i, p: (i, 0, (COL_ATT + D_ATT) // LANES + p)),
            pl.BlockSpec((1, s, LANES), lambda i, p: (i, 0, (COL_ATT + 2 * D_ATT) // LANES + p)),
            pl.BlockSpec((s, LANES), lambda i, p: (0, 0)),
            pl.BlockSpec((s, LANES), lambda i, p: (0, 0)),
        ],
        out_specs=pl.BlockSpec((1, s, LANES), lambda i, p: (i, 0, p)),
        out_shape=jax.ShapeDtypeStruct((b, s, D_ATT), BF16),
        scratch_shapes=[pltpu.VMEM((s, LANES), BF16)] * 5 + [pltpu.VMEM((s // tq, tq, tk), F32)],
        compiler_params=pltpu.CompilerParams(
            dimension_semantics=("parallel", "arbitrary"), vmem_limit_bytes=VMEM_LIMIT),
        name="dilated_attention",
    )(z3, z3, z3, cos_t, sin_t)


POOL_PAD = 16


def _pool_kernel(u_ref, w_ref, sc_ref, o_ref, pad_s, *, seq):
    t1 = (lax.broadcasted_iota(jnp.int32, (seq, POOL_GROUP), 0) + 1).astype(F32)
    pad_s[0:POOL_PAD, :] = jnp.zeros((POOL_PAD, POOL_GROUP), F32)
    for gi, w in enumerate(POOL_WINDOWS):
        cols = slice(gi * POOL_GROUP, (gi + 1) * POOL_GROUP)
        u = u_ref[0, :, cols]
        s = u
        k = 1
        while k < w:
            pad_s[POOL_PAD:POOL_PAD + seq, :] = s
            s = s + pad_s[POOL_PAD - k:POOL_PAD - k + seq, :]
            k *= 2
        p = s / jnp.minimum(t1, float(w)) - u
        y = _dot(p.astype(BF16), w_ref[gi])
        o_ref[0, :, cols] = (y * sc_ref[:, cols]).astype(o_ref.dtype)


def _pool(z3, pool_w, pool_scale):
    b, s, _ = z3.shape
    kern = functools.partial(_pool_kernel, seq=s)
    return pl.pallas_call(
        kern,
        grid=(b,),
        in_specs=[
            pl.BlockSpec((1, s, D_POOL), lambda i: (i, 0, COL_POOL // D_POOL)),
            pl.BlockSpec(pool_w.shape, lambda i: (0, 0, 0)),
            pl.BlockSpec((1, D_POOL), lambda i: (0, 0)),
        ],
        out_specs=pl.BlockSpec((1, s, D_POOL), lambda i: (i, 0, 0)),
        out_shape=jax.ShapeDtypeStruct((b, s, D_POOL), BF16),
        scratch_shapes=[pltpu.VMEM((s + POOL_PAD, POOL_GROUP), F32)],
        compiler_params=pltpu.CompilerParams(
            dimension_semantics=("parallel",), vmem_limit_bytes=VMEM_LIMIT),
        name="pool_mixer",
    )(z3, pool_w, pool_scale)


def _split_dot(x, ones_b, terms):
    acc = None
    rem = x
    for _ in range(terms):
        piece = rem.astype(BF16)
        part = _dot(piece, ones_b)
        acc = part if acc is None else acc + part
        rem = rem - piece.astype(F32)
    return acc


N_RWKV_IN = 16


def _rwkv_kernel(*refs, chunk, rows, n_steps, cast_steps):
    (zr_ref, zrp_ref, zl_ref, zlp_ref, mur_ref, mul_ref, wup_ref, w0_ref, aup_ref, a0_ref,
     gup_ref, kk_ref, ka_ref, rk_ref, gnw_ref, gnb_ref) = refs[:N_RWKV_IN]
    n_cast = len(cast_steps)
    cast_in = refs[N_RWKV_IN:N_RWKV_IN + n_cast]
    o_ref = refs[N_RWKV_IN + n_cast]
    cast_out = refs[N_RWKV_IN + n_cast + 1:N_RWKV_IN + 2 * n_cast + 1]
    h_s, sh_s = refs[N_RWKV_IN + 2 * n_cast + 1:]
    c = pl.program_id(1)
    npair = D_RWKV // LANES
    halo = 8
    wr = 3 * D_RWKV

    step = pl.program_id(0) * pl.num_programs(1) + c
    for src, dst, used in zip(cast_in, cast_out, cast_steps):
        if used == n_steps:
            dst[...] = src[...].astype(BF16)
        else:
            @pl.when(step < used)
            def _(src=src, dst=dst):
                dst[...] = src[...].astype(BF16)

    @pl.when(c == 0)
    def _():
        h_s[...] = jnp.zeros_like(h_s)

    row = lax.broadcasted_iota(jnp.int32, (LANES, LANES), 0)
    col = lax.broadcasted_iota(jnp.int32, (LANES, LANES), 1)
    head_ones = ((row < HEAD_DIM) == (col < HEAD_DIM)).astype(BF16)
    trow = lax.broadcasted_iota(jnp.int32, (chunk, chunk), 0)
    tcol = lax.broadcasted_iota(jnp.int32, (chunk, chunk), 1)
    tri = (trow >= tcol).astype(BF16)
    lane = lax.broadcasted_iota(jnp.int32, (chunk, LANES), 1)
    head_a = lane < HEAD_DIM
    strict_lower = row > col
    lower = row >= col
    eye = (row == col).astype(F32)
    first = c == 0

    def seg_sum(x):
        return jnp.concatenate(
            [_split_dot(x[:, p * LANES:(p + 1) * LANES], head_ones, SEG_SUM_TERMS) for p in range(npair)], axis=1)

    def stack2(x):
        return jnp.concatenate([jnp.where(head_a, x, 0.0), jnp.where(head_a, 0.0, x)], axis=0)

    tok = []
    for bi in range(rows):
        zr = zr_ref[bi]
        zl = zl_ref[bi]
        sh_s[bi, 0:halo, 0:wr] = jnp.where(first, 0.0, zrp_ref[bi])
        sh_s[bi, 0:halo, wr:wr + D_LORA_PACK] = jnp.where(first, 0.0, zlp_ref[bi])
        sh_s[bi, halo:halo + chunk, 0:wr] = zr
        sh_s[bi, halo:halo + chunk, wr:wr + D_LORA_PACK] = zl
        zr_prev = sh_s[bi, halo - 1:halo - 1 + chunk, 0:wr]
        zl_prev = sh_s[bi, halo - 1:halo - 1 + chunk, wr:wr + D_LORA_PACK]
        zr = zr + (zr_prev - zr) * mur_ref[...]
        zl = zl + (zl_prev - zl) * mul_ref[...]
        r = zr[:, 0:D_RWKV]
        k = zr[:, D_RWKV:2 * D_RWKV]
        v = zr[:, 2 * D_RWKV:3 * D_RWKV]
        zw = zl[:, 0:LORA_PAD]
        za = zl[:, LORA_PAD:2 * LORA_PAD]
        zg = zl[:, 2 * LORA_PAD:D_LORA_PACK]

        lw = w0_ref[...] + _dot(jnp.tanh(zw).astype(BF16), wup_ref[...])
        nlw = -lw
        softplus = jnp.maximum(nlw, 0.0) + jnp.log(1.0 + jnp.exp(-jnp.abs(nlw)))
        ld = -jnp.exp(-softplus - 0.5)
        a = jax.nn.sigmoid(a0_ref[...] + _dot(za.astype(BF16), aup_ref[...]))
        g = _dot(jax.nn.sigmoid(zg).astype(BF16), gup_ref[...])
        kk = k * kk_ref[...]
        k2 = k * (1.0 + (a - 1.0) * ka_ref[...])
        kk = kk / jnp.maximum(jnp.sqrt(seg_sum(kk * kk)), 1e-12)
        bonus = seg_sum(r * k2 * rk_ref[...]) * v
        bb = kk * a

        cum = _split_dot_left(tri, ld, 3)
        cum_end = cum[chunk - 1:chunk, :]
        e_inv = jnp.exp(-cum)
        e_end = jnp.exp(cum_end - cum)
        tok.append(dict(
            r_t=r * jnp.exp(cum), kk_t=kk * jnp.exp(cum - ld), b_h=bb * e_inv, k_h=k2 * e_inv,
            b_a=bb * e_end, k_a=k2 * e_end, a_end=jnp.exp(cum_end), v=v, bonus=bonus, g=g))

    units = [(bi, p) for bi in range(rows) for p in range(npair)]

    def pair(name):
        return [tok[bi][name][:, p * LANES:(p + 1) * LANES] for bi, p in units]

    kks = [stack2(x).astype(BF16) for x in pair("kk_t")]
    bs = [stack2(x).astype(BF16) for x in pair("b_h")]
    ks = [stack2(x).astype(BF16) for x in pair("k_h")]
    rs = [stack2(x).astype(BF16) for x in pair("r_t")]
    vs = [stack2(x).astype(BF16) for x in pair("v")]
    bat = [stack2(x).T.astype(BF16) for x in pair("b_a")]
    kat = [stack2(x).T.astype(BF16) for x in pair("k_a")]
    a_col = [jnp.broadcast_to(x, (LANES, LANES)).T for x in pair("a_end")]
    ids = range(len(units))

    tt = [_dot_nt(jnp.concatenate([kks[i], rs[i]], axis=0), jnp.concatenate([bs[i], ks[i]], axis=0)) for i in ids]
    l_b = [jnp.where(strict_lower, tt[i][0:LANES, 0:LANES], 0.0) for i in ids]
    l_k = [jnp.where(strict_lower, tt[i][0:LANES, LANES:2 * LANES], 0.0).astype(BF16) for i in ids]
    m_b = [jnp.where(lower, tt[i][LANES:2 * LANES, 0:LANES], 0.0).astype(BF16) for i in ids]
    m_k = [jnp.where(lower, tt[i][LANES:2 * LANES, LANES:2 * LANES], 0.0).astype(BF16) for i in ids]

    sel = ((row ^ col) < 2) & ((row & 1) != 0) & ((col & 1) == 0)
    t_inv = [eye - jnp.where(sel, l_b[i], 0.0) for i in ids]
    s = 2
    while s < HEAD_DIM:
        sel = ((row ^ col) < 2 * s) & ((row & s) != 0) & ((col & s) == 0)
        t_b = [t_inv[i].astype(BF16) for i in ids]
        inner = [_dot(jnp.where(sel, l_b[i], 0.0).astype(BF16), t_b[i]).astype(BF16) for i in ids]
        t_inv = [t_inv[i] - _dot(t_b[i], inner[i]) for i in ids]
        s *= 2
    t_b = [t_inv[i].astype(BF16) for i in ids]

    mlv = [_dot(jnp.concatenate([m_k[i], l_k[i]], axis=0), vs[i]) for i in ids]
    y0 = [mlv[i][0:LANES, :] for i in ids]
    lkv = [mlv[i][LANES:2 * LANES, :].astype(BF16) for i in ids]
    w_m = [_dot(t_b[i], kks[i]).astype(BF16) for i in ids]

    h = [h_s[i] for i in ids]
    h_b = [h[i].astype(BF16) for i in ids]
    u_b = [(-_dot(jnp.concatenate([w_m[i], t_b[i]], axis=1),
                  jnp.concatenate([h_b[i], lkv[i]], axis=0))).astype(BF16) for i in ids]
    y = [_dot(jnp.concatenate([rs[i], m_b[i]], axis=1), jnp.concatenate([h_b[i], u_b[i]], axis=0)) + y0[i]
         for i in ids]
    for i in ids:
        h_s[i] = a_col[i] * h[i] + _dot(jnp.concatenate([bat[i], kat[i]], axis=1),
                                        jnp.concatenate([u_b[i], vs[i]], axis=0))

    for bi in range(rows):
        yb = jnp.concatenate(
            [y[bi * npair + p][0:chunk, :] + y[bi * npair + p][chunk:2 * chunk, :] for p in range(npair)], axis=1)
        mean = seg_sum(yb) * (1.0 / HEAD_DIM)
        d = yb - mean
        var = seg_sum(d * d) * (1.0 / HEAD_DIM)
        yn = d * lax.rsqrt(var + RWKV_GN_EPS) * gnw_ref[...] + gnb_ref[...]
        o_ref[bi] = ((yn + tok[bi]["bonus"]) * tok[bi]["g"]).astype(o_ref.dtype)


def _split_dot_left(tri_b, x, terms):
    acc = None
    rem = x
    for _ in range(terms):
        piece = rem.astype(BF16)
        part = _dot(tri_b, piece)
        acc = part if acc is None else acc + part
        rem = rem - piece.astype(F32)
    return acc


BF16_SUBLANES = 16


def _rwkv(z3, prm, *, rows, cast=()):
    b, s, _ = z3.shape
    chunk = RWKV_CHUNK
    nc = s // chunk
    nb = b // rows
    wr = 3 * D_RWKV
    rows8 = chunk // 8
    npair = D_RWKV // LANES

    def vec(n):
        return pl.BlockSpec((1, n), lambda i, c: (0, 0))

    def full(a):
        return pl.BlockSpec(a.shape, lambda i, c: (0, 0))

    n_steps = nb * nc
    cast_in_specs, cast_out_specs, cast_shapes, cast_steps = [], [], [], []
    for w, layer in cast:
        _, r, cc = w.shape
        assert r % BF16_SUBLANES == 0
        used = max(k for k in range(1, n_steps + 1) if (r // BF16_SUBLANES) % k == 0)
        tr = r // used
        row_blk = lambda i, c, last=used - 1: jnp.minimum(i * nc + c, last)
        cast_in_specs.append(pl.BlockSpec((1, tr, cc), lambda i, c, f=row_blk, l=layer: (l, f(i, c), 0)))
        cast_out_specs.append(pl.BlockSpec((1, tr, cc), lambda i, c, f=row_blk: (0, f(i, c), 0)))
        cast_shapes.append(jax.ShapeDtypeStruct((1, r, cc), BF16))
        cast_steps.append(used)
    kern = functools.partial(_rwkv_kernel, chunk=chunk, rows=rows, n_steps=n_steps, cast_steps=tuple(cast_steps))

    outs = pl.pallas_call(
        kern,
        grid=(nb, nc),
        in_specs=[
            pl.BlockSpec((rows, chunk, wr), lambda i, c: (i, c, COL_RKV // wr)),
            pl.BlockSpec((rows, 8, wr), lambda i, c: (i, jnp.maximum(c * rows8 - 1, 0), COL_RKV // wr)),
            pl.BlockSpec((rows, chunk, D_LORA_PACK), lambda i, c: (i, c, COL_LORA // D_LORA_PACK)),
            pl.BlockSpec((rows, 8, D_LORA_PACK),
                         lambda i, c: (i, jnp.maximum(c * rows8 - 1, 0), COL_LORA // D_LORA_PACK)),
            vec(wr), vec(D_LORA_PACK),
            full(prm["w_up"]), vec(D_RWKV), full(prm["a_up"]), vec(D_RWKV), full(prm["g_up"]),
            vec(D_RWKV), vec(D_RWKV), vec(D_RWKV), vec(D_RWKV), vec(D_RWKV),
        ] + cast_in_specs,
        out_specs=[pl.BlockSpec((rows, chunk, D_RWKV), lambda i, c: (i, c, 0))] + cast_out_specs,
        out_shape=[jax.ShapeDtypeStruct((b, s, D_RWKV), BF16)] + cast_shapes,
        scratch_shapes=[
            pltpu.VMEM((rows * npair, LANES, LANES), F32),
            pltpu.VMEM((rows, chunk + 8, wr + D_LORA_PACK), F32),
        ],
        compiler_params=pltpu.CompilerParams(
            dimension_semantics=("arbitrary", "arbitrary"), vmem_limit_bytes=VMEM_LIMIT),
        name="rwkv7",
    )(z3, z3, z3, z3, prm["mu_r"], prm["mu_l"], prm["w_up"], prm["w0"], prm["a_up"], prm["a0"],
      prm["g_up"], prm["k_k"], prm["k_a"], prm["r_k"], prm["gn_w"], prm["gn_b"], *[w for w, _ in cast])
    return outs[0], list(outs[1:])


def _pad_cols(w, n):
    return jnp.pad(w, ((0, 0),) * (w.ndim - 1) + ((0, n - w.shape[-1]),))


def _pack_w_in_t(wt):
    o = 3 * D_ATT + 3 * D_RWKV
    o2 = o + RWKV_DECAY_LORA
    o3 = o2 + RWKV_AAA_LORA
    pad = ((0, 0), (0, LORA_PAD - RWKV_DECAY_LORA), (0, 0))
    return jnp.concatenate([wt[:, :o], jnp.pad(wt[:, o:o2], pad), jnp.pad(wt[:, o2:o3], pad), wt[:, o3:]], axis=1)


def _pack_mu(mu):
    o = 3 * D_RWKV
    mu_l = jnp.concatenate([
        jnp.pad(mu[o:o + RWKV_DECAY_LORA], (0, LORA_PAD - RWKV_DECAY_LORA)),
        jnp.pad(mu[o + RWKV_DECAY_LORA:o + RWKV_DECAY_LORA + RWKV_AAA_LORA], (0, LORA_PAD - RWKV_AAA_LORA)),
        mu[o + RWKV_DECAY_LORA + RWKV_AAA_LORA:],
    ])
    return mu[None, :o], mu_l[None, :]


def _pad_rows(w, n):
    return jnp.pad(w, ((0, n - w.shape[0]), (0, 0)))


def _rope_tables(seq):
    inv_freq = ROPE_THETA ** (-jnp.arange(0, HEAD_DIM, 2, dtype=F32) / HEAD_DIM)
    ang = jnp.arange(seq, dtype=F32)[:, None] * inv_freq[None, :]
    cos, sin = jnp.cos(ang), jnp.sin(ang)
    reps = LANES // HEAD_DIM
    cos_t = jnp.tile(jnp.concatenate([cos, cos], axis=1), (1, reps))
    sin_t = jnp.tile(jnp.concatenate([-sin, sin], axis=1), (1, reps))
    return cos_t, sin_t


def _tile(n, pref):
    t = min(n, pref)
    assert n % t == 0
    return t


def _row_tile(r, pref):
    return max(k for k in range(BF16_SUBLANES, pref + 1, BF16_SUBLANES) if r % k == 0)


def kernel(x, ffn1_norm_pre, ffn1_norm_post, ffn1_w_gate, ffn1_w_up, ffn1_w_down, mix_norm_pre, mix_norm_post, w_in, w_out, rwkv_mu, rwkv_w_up, rwkv_w0, rwkv_a_up, rwkv_a0, rwkv_g_up, rwkv_k_k, rwkv_k_a, rwkv_r_k, rwkv_gn_w, rwkv_gn_b, pool_w, pool_scale, ffn2_norm_pre, ffn2_norm_post, ffn2_w_gate, ffn2_w_up, ffn2_w_down):
    b, s, d = x.shape
    depth = w_in.shape[0]
    t = b * s
    d_ff = ffn1_w_gate.shape[2]
    assert s % RWKV_CHUNK == 0 and b % RWKV_ROWS == 0 and d == D_ATT + D_RWKV + D_POOL
    tm = _tile(t, 512)
    tm_proj = _tile(t, 1024)
    tf = _tile(d_ff, 512)
    tn = _tile(D_PACK, D_PACK // 4)
    cos_t, sin_t = _rope_tables(s)
    x2 = x.reshape(t, d).astype(F32)

    w_in_t = jnp.swapaxes(w_in, 1, 2)
    big = (ffn1_w_gate, ffn1_w_up, ffn1_w_down, ffn2_w_gate, ffn2_w_up, ffn2_w_down, w_in_t, w_out)
    wb = [_cast_bf16(w, layer=0, tr=_row_tile(w.shape[1], 256)) for w in big]
    ffn_g = [[g[:, None, :] for g in gs] for gs in
             ((ffn1_norm_pre, ffn1_norm_post), (ffn2_norm_pre, ffn2_norm_post))]
    mix_g_pre, mix_g_post = mix_norm_pre[:, None, :], mix_norm_post[:, None, :]

    def gains(gs, i):
        return [g[i:i + 1] for g in gs]

    for i in range(depth):
        x2 = _ffn(x2, *gains(ffn_g[0], i), *wb[0:3], layer=0, tm=tm, tf=tf)

        z = _proj(x2, mix_g_pre[i:i + 1], _pack_w_in_t(wb[6]), layer=0, tm=tm_proj, tn=tn)
        z3 = z.reshape(b, s, D_PACK)
        y_att = _attention(z3, cos_t, sin_t)
        mu_r, mu_l = _pack_mu(rwkv_mu[i])
        prm = dict(
            mu_r=mu_r, mu_l=mu_l,
            w_up=_pad_rows(rwkv_w_up[i], LORA_PAD).astype(BF16), w0=rwkv_w0[i][None],
            a_up=_pad_rows(rwkv_a_up[i], LORA_PAD).astype(BF16), a0=rwkv_a0[i][None],
            g_up=rwkv_g_up[i].astype(BF16), k_k=rwkv_k_k[i][None], k_a=rwkv_k_a[i][None],
            r_k=rwkv_r_k[i].reshape(1, D_RWKV), gn_w=rwkv_gn_w[i][None], gn_b=rwkv_gn_b[i][None])
        nxt = [(w, i + 1) for w in big] if i + 1 < depth else []
        y_rwkv, wb_next = _rwkv(z3, prm, rows=RWKV_ROWS, cast=nxt)
        y_pool = _pool(z3, pool_w[i].astype(BF16), pool_scale[i][None])

        x2 = _mixout(x2, y_att.reshape(t, D_ATT), y_rwkv.reshape(t, D_RWKV), y_pool.reshape(t, D_POOL),
                     wb[7], mix_g_post[i:i + 1], layer=0, tm=tm)

        x2 = _ffn(x2, *gains(ffn_g[1], i), *wb[3:6], layer=0, tm=tm, tf=tf)
        wb = wb_next
    return x2.reshape(b, s, d).astype(x.dtype)
```

```python
import functools
import math

import jax
import jax.numpy as jnp
from jax import lax
from jax.experimental import pallas as pl
from jax.experimental.pallas import tpu as pltpu

F32 = jnp.float32
BF16 = jnp.bfloat16

HEAD_DIM = 64
N_ATT_HEADS = 12
N_RWKV_HEADS = 12
D_ATT = N_ATT_HEADS * HEAD_DIM
D_RWKV = N_RWKV_HEADS * HEAD_DIM
D_POOL = 512
N_POOL_GROUPS = 4
POOL_GROUP = D_POOL // N_POOL_GROUPS
POOL_WINDOWS = (2, 4, 8, 16)
DILATED_BRANCHES = ((128, 1), (512, 4), (2048, 16))
ROPE_THETA = 10000.0
RWKV_DECAY_LORA = 96
RWKV_AAA_LORA = 96
RWKV_GATE_LORA = 256
NORM_EPS = 1e-6
RWKV_GN_EPS = 64e-5

LANES = 128
LORA_PAD = 128
COL_ATT = 0
COL_RKV = 3 * D_ATT
COL_LORA = COL_RKV + 3 * D_RWKV
D_LORA_PACK = 2 * LORA_PAD + RWKV_GATE_LORA
COL_POOL = COL_LORA + D_LORA_PACK
D_PACK = COL_POOL + D_POOL

SEG_SUM_TERMS = 1
RWKV_ROWS = 2
RWKV_CHUNK = 64
ATT_KV_TILE = 512
ATT_GROUP = 2
VMEM_LIMIT = 56 * 1024 * 1024
VMEM_LIMIT_FFN = 62 * 1024 * 1024

NEG_BIG = -1e30
LOG2_E = math.log2(math.e)


def _rms(xf, g):
    return xf * lax.rsqrt(jnp.mean(xf * xf, axis=-1, keepdims=True) + NORM_EPS) * g


def _dot(a, b):
    return jnp.dot(a, b, preferred_element_type=F32)


def _dot_nt(a, b):
    return lax.dot_general(a, b, (((1,), (1,)), ((), ())), preferred_element_type=F32)


def _ffn_kernel(x_ref, gpre_ref, gpost_ref, wg_ref, wu_ref, wd_ref, o_ref, h_ref):
    j = pl.program_id(1)

    @pl.when(j == 0)
    def _():
        h_ref[...] = _rms(x_ref[...], gpre_ref[...]).astype(BF16)
        o_ref[...] = jnp.zeros_like(o_ref)

    half = o_ref.shape[0] // 2
    for r0 in (0, half):
        rows = slice(r0, r0 + half)
        h = h_ref[rows, :]
        g = _dot(h, wg_ref[...])
        u = _dot(h, wu_ref[...])
        a = (g * jax.nn.sigmoid(g) * u).astype(BF16)
        o_ref[rows, :] += _dot(a, wd_ref[...])

    @pl.when(j == pl.num_programs(1) - 1)
    def _():
        o_ref[...] = x_ref[...] + 0.5 * _rms(o_ref[...], gpost_ref[...])


def _ffn(x, g_pre, g_post, wg, wu, wd, *, layer, tm, tf):
    t, d = x.shape
    f = wg.shape[2]
    return pl.pallas_call(
        _ffn_kernel,
        grid=(t // tm, f // tf),
        in_specs=[
            pl.BlockSpec((tm, d), lambda i, j: (i, 0)),
            pl.BlockSpec((None, 1, d), lambda i, j: (layer, 0, 0)),
            pl.BlockSpec((None, 1, d), lambda i, j: (layer, 0, 0)),
            pl.BlockSpec((None, d, tf), lambda i, j: (layer, 0, j)),
            pl.BlockSpec((None, d, tf), lambda i, j: (layer, 0, j)),
            pl.BlockSpec((None, tf, d), lambda i, j: (layer, j, 0)),
        ],
        out_specs=pl.BlockSpec((tm, d), lambda i, j: (i, 0)),
        out_shape=jax.ShapeDtypeStruct((t, d), F32),
        scratch_shapes=[pltpu.VMEM((tm, d), BF16)],
        compiler_params=pltpu.CompilerParams(
            dimension_semantics=("parallel", "arbitrary"), vmem_limit_bytes=VMEM_LIMIT_FFN),
        name="ffn",
    )(x, g_pre, g_post, wg, wu, wd)


def _proj_kernel(x_ref, g_ref, wt_ref, o_ref, h_ref):
    @pl.when(pl.program_id(1) == 0)
    def _():
        h_ref[...] = _rms(x_ref[...], g_ref[...]).astype(BF16)

    o_ref[...] = _dot_nt(h_ref[...], wt_ref[...])


def _proj(x, g, wt, *, layer, tm, tn):
    t, d = x.shape
    n = wt.shape[1]
    return pl.pallas_call(
        _proj_kernel,
        grid=(t // tm, n // tn),
        in_specs=[
            pl.BlockSpec((tm, d), lambda i, j: (i, 0)),
            pl.BlockSpec((None, 1, d), lambda i, j: (layer, 0, 0)),
            pl.BlockSpec((None, tn, d), lambda i, j: (layer, j, 0)),
        ],
        out_specs=pl.BlockSpec((tm, tn), lambda i, j: (i, j)),
        out_shape=jax.ShapeDtypeStruct((t, n), F32),
        scratch_shapes=[pltpu.VMEM((tm, d), BF16)],
        compiler_params=pltpu.CompilerParams(
            dimension_semantics=("parallel", "arbitrary"), vmem_limit_bytes=VMEM_LIMIT),
        name="proj_in",
    )(x, g, wt)


def _mixout_kernel(x_ref, ya_ref, yr_ref, yp_ref, wa_ref, wr_ref, wp_ref, g_ref, o_ref):
    mix = _dot(ya_ref[...], wa_ref[...]) + _dot(yr_ref[...], wr_ref[...]) + _dot(yp_ref[...], wp_ref[...])
    o_ref[...] = x_ref[...] + _rms(mix, g_ref[...])


def _mixout(x, ya, yr, yp, wo, g, *, layer, tm):
    t, d = x.shape
    da, dr, dp = ya.shape[1], yr.shape[1], yp.shape[1]
    assert da == dr and (da + dr) % dp == 0
    return pl.pallas_call(
        _mixout_kernel,
        grid=(t // tm,),
        in_specs=[
            pl.BlockSpec((tm, d), lambda i: (i, 0)),
            pl.BlockSpec((tm, da), lambda i: (i, 0)),
            pl.BlockSpec((tm, dr), lambda i: (i, 0)),
            pl.BlockSpec((tm, dp), lambda i: (i, 0)),
            pl.BlockSpec((None, da, d), lambda i: (layer, 0, 0)),
            pl.BlockSpec((None, dr, d), lambda i: (layer, 1, 0)),
            pl.BlockSpec((None, dp, d), lambda i: (layer, (da + dr) // dp, 0)),
            pl.BlockSpec((None, 1, d), lambda i: (layer, 0, 0)),
        ],
        out_specs=pl.BlockSpec((tm, d), lambda i: (i, 0)),
        out_shape=jax.ShapeDtypeStruct((t, d), F32),
        compiler_params=pltpu.CompilerParams(
            dimension_semantics=("parallel",), vmem_limit_bytes=VMEM_LIMIT),
        name="proj_out",
    )(x, ya, yr, yp, wo, wo, wo, g)


def _cast_kernel(w_ref, o_ref):
    o_ref[...] = w_ref[...].astype(BF16)


def _cast_bf16(w, *, layer, tr):
    _, r, c = w.shape
    return pl.pallas_call(
        _cast_kernel,
        grid=(r // tr,),
        in_specs=[pl.BlockSpec((1, tr, c), lambda j: (layer, j, 0))],
        out_specs=pl.BlockSpec((1, tr, c), lambda j: (0, j, 0)),
        out_shape=jax.ShapeDtypeStruct((1, r, c), BF16),
        compiler_params=pltpu.CompilerParams(
            dimension_semantics=("parallel",), vmem_limit_bytes=VMEM_LIMIT),
        name="cast_bf16",
    )(w)


def _att_bias_tile(base, tq, tk):
    row = lax.broadcasted_iota(jnp.int32, (tq, tk), 0)
    col = lax.broadcasted_iota(jnp.int32, (tq, tk), 1)
    delta = base + row - col
    cnt = jnp.zeros((tq, tk), jnp.int32)
    for window, dil in DILATED_BRANCHES:
        hit = (delta >= 0) & (delta <= window) & ((delta & (dil - 1)) == 0)
        cnt = cnt + hit.astype(jnp.int32)
    return jnp.where(cnt == 3, math.log2(3.0),
                     jnp.where(cnt == 2, 1.0, jnp.where(cnt == 1, 0.0, NEG_BIG))).astype(F32)


def _att_kernel(q_ref, k_ref, v_ref, cos_ref, sin_ref, o_ref,
                qa_s, qb_s, k_s, va_s, vb_s, bias_s, *, seq, tq, tk, group):
    nq = seq // tq
    ratio = tk // tq

    @pl.when(pl.program_id(1) == 0)
    def _():
        for off in range(nq):
            bias_s[off] = _att_bias_tile(off * tq, tq, tk)

    lane = lax.broadcasted_iota(jnp.int32, (tq, LANES), 1)
    first_half = (lane & (HEAD_DIM - 1)) < (HEAD_DIM // 2)
    head_a = lane < HEAD_DIM

    def rope(x, cos, sin):
        swapped = jnp.where(first_half, pltpu.roll(x, LANES - HEAD_DIM // 2, 1), pltpu.roll(x, HEAD_DIM // 2, 1))
        return x * cos + swapped * sin

    def prep(i, carry):
        rows = pl.ds(pl.multiple_of(i * tq, tq), tq)
        cos = cos_ref[rows, :]
        sin = sin_ref[rows, :]
        q = rope(q_ref[0, rows, :], cos, sin) * (HEAD_DIM ** -0.5 * LOG2_E)
        k = rope(k_ref[0, rows, :], cos, sin)
        v = v_ref[0, rows, :]
        qa_s[rows, :] = jnp.where(head_a, q, 0.0).astype(BF16)
        qb_s[rows, :] = jnp.where(head_a, 0.0, q).astype(BF16)
        k_s[rows, :] = k.astype(BF16)
        va_s[rows, :] = jnp.where(head_a, v, 1.0).astype(BF16)
        vb_s[rows, :] = jnp.where(head_a, 1.0, v).astype(BF16)
        return carry

    lax.fori_loop(0, nq, prep, 0)

    def tile_max(s):
        m = s[:, 0:LANES]
        for t in range(1, s.shape[1] // LANES):
            m = jnp.maximum(m, s[:, t * LANES:(t + 1) * LANES])
        return m

    q_s = (qa_s, qb_s)
    v_s = (va_s, vb_s)
    m_run = {}
    acc = {}
    for kj in range(seq // tk):
        blocks = list(range(ratio * kj, nq))
        for g0 in range(0, len(blocks), group):
            chains = [(qi, hd) for qi in blocks[g0:g0 + group] for hd in range(2)]
            width = {qi: (tq if qi == ratio * kj else tk) for qi, _ in chains}
            s = {}
            for qi, hd in chains:
                krows = slice(kj * tk, kj * tk + width[qi])
                s[qi, hd] = (_dot_nt(q_s[hd][qi * tq:(qi + 1) * tq, :], k_s[krows, :])
                             + bias_s[qi - ratio * kj, :, 0:width[qi]])
            p = {}
            for ch in chains:
                m_tile = jnp.max(tile_max(s[ch]), axis=-1, keepdims=True)
                if kj == 0:
                    m_new = m_tile
                else:
                    m_new = jnp.maximum(m_run[ch], m_tile)
                    acc[ch] = jnp.exp2(m_run[ch] - m_new) * acc[ch]
                m_run[ch] = m_new
                p[ch] = jnp.exp2(s[ch] - m_new).astype(BF16)
            for ch in chains:
                pv = _dot(p[ch], v_s[ch[1]][kj * tk:kj * tk + width[ch[0]], :])
                acc[ch] = pv if kj == 0 else acc[ch] + pv

    for qi in range(nq):
        o_a = acc[qi, 0] / pltpu.roll(acc[qi, 0], HEAD_DIM, 1)
        o_b = acc[qi, 1] / pltpu.roll(acc[qi, 1], HEAD_DIM, 1)
        o_ref[0, qi * tq:(qi + 1) * tq, :] = jnp.where(head_a, o_a, o_b).astype(o_ref.dtype)


def _attention(z3, cos_t, sin_t):
    b, s, _ = z3.shape
    tk = min(ATT_KV_TILE, s)
    tq = tk // 2
    npair = D_ATT // LANES
    kern = functools.partial(_att_kernel, seq=s, tq=tq, tk=tk, group=ATT_GROUP)
    return pl.pallas_call(
        kern,
        grid=(b, npair),
        in_specs=[
            pl.BlockSpec((1, s, LANES), lambda i, p: (i, 0, COL_ATT // LANES + p)),
            pl.BlockSpec((1, s, LANES), lambda i, p: (i, 0, (COL_ATT + D_ATT) // LANES + p)),
            pl.BlockSpec((1, s, LANES), lambda i, p: (i, 0, (COL_ATT + 2 * D_ATT) // LANES + p)),
            pl.BlockSpec((s, LANES), lambda i, p: (0, 0)),
            pl.BlockSpec((s, LANES), lambda i, p: (0, 0)),
        ],
        out_specs=pl.BlockSpec((1, s, LANES), lambda i, p: (i, 0, p)),
        out_shape=jax.ShapeDtypeStruct((b, s, D_ATT), BF16),
        scratch_shapes=[pltpu.VMEM((s, LANES), BF16)] * 5 + [pltpu.VMEM((s // tq, tq, tk), F32)],
        compiler_params=pltpu.CompilerParams(
            dimension_semantics=("parallel", "arbitrary"), vmem_limit_bytes=VMEM_LIMIT),
        name="dilated_attention",
    )(z3, z3, z3, cos_t, sin_t)


POOL_PAD = 16


def _pool_kernel(u_ref, w_ref, sc_ref, o_ref, pad_s, *, seq):
    t1 = (lax.broadcasted_iota(jnp.int32, (seq, POOL_GROUP), 0) + 1).astype(F32)
    pad_s[0:POOL_PAD, :] = jnp.zeros((POOL_PAD, POOL_GROUP), F32)
    for gi, w in enumerate(POOL_WINDOWS):
        cols = slice(gi * POOL_GROUP, (gi + 1) * POOL_GROUP)
        u = u_ref[0, :, cols]
        s = u
        k = 1
        while k < w:
            pad_s[POOL_PAD:POOL_PAD + seq, :] = s
            s = s + pad_s[POOL_PAD - k:POOL_PAD - k + seq, :]
            k *= 2
        p = s / jnp.minimum(t1, float(w)) - u
        y = _dot(p.astype(BF16), w_ref[gi])
        o_ref[0, :, cols] = (y * sc_ref[:, cols]).astype(o_ref.dtype)


def _pool(z3, pool_w, pool_scale):
    b, s, _ = z3.shape
    kern = functools.partial(_pool_kernel, seq=s)
    return pl.pallas_call(
        kern,
        grid=(b,),
        in_specs=[
            pl.BlockSpec((1, s, D_POOL), lambda i: (i, 0, COL_POOL // D_POOL)),
            pl.BlockSpec(pool_w.shape, lambda i: (0, 0, 0)),
            pl.BlockSpec((1, D_POOL), lambda i: (0, 0)),
        ],
        out_specs=pl.BlockSpec((1, s, D_POOL), lambda i: (i, 0, 0)),
        out_shape=jax.ShapeDtypeStruct((b, s, D_POOL), BF16),
        scratch_shapes=[pltpu.VMEM((s + POOL_PAD, POOL_GROUP), F32)],
        compiler_params=pltpu.CompilerParams(
            dimension_semantics=("parallel",), vmem_limit_bytes=VMEM_LIMIT),
        name="pool_mixer",
    )(z3, pool_w, pool_scale)


def _split_dot(x, ones_b, terms):
    acc = None
    rem = x
    for _ in range(terms):
        piece = rem.astype(BF16)
        part = _dot(piece, ones_b)
        acc = part if acc is None else acc + part
        rem = rem - piece.astype(F32)
    return acc


N_RWKV_IN = 18


def _rwkv_kernel(*refs, chunk, rows, n_steps, cast_steps):
    (zr_ref, zrp_ref, zl_ref, zlp_ref, mur_ref, mul_ref, wup_ref, w0_ref, aup_ref, a0_ref,
     gup_ref, kk_ref, ka_ref, rk_ref, gnw_ref, gnb_ref, tri_ref, ones_ref) = refs[:N_RWKV_IN]
    n_cast = len(cast_steps)
    cast_in = refs[N_RWKV_IN:N_RWKV_IN + n_cast]
    o_ref = refs[N_RWKV_IN + n_cast]
    cast_out = refs[N_RWKV_IN + n_cast + 1:N_RWKV_IN + 2 * n_cast + 1]
    h_s, sh_s = refs[N_RWKV_IN + 2 * n_cast + 1:]
    c = pl.program_id(1)
    npair = D_RWKV // LANES
    halo = 8
    wr = 3 * D_RWKV

    step = pl.program_id(0) * pl.num_programs(1) + c
    for src, dst, used in zip(cast_in, cast_out, cast_steps):
        if used == n_steps:
            dst[...] = src[...].astype(BF16)
        else:
            @pl.when(step < used)
            def _(src=src, dst=dst):
                dst[...] = src[...].astype(BF16)

    @pl.when(c == 0)
    def _():
        h_s[...] = jnp.zeros_like(h_s)

    row = lax.broadcasted_iota(jnp.int32, (LANES, LANES), 0)
    col = lax.broadcasted_iota(jnp.int32, (LANES, LANES), 1)
    lane = lax.broadcasted_iota(jnp.int32, (chunk, LANES), 1)
    head_a = lane < HEAD_DIM
    strict_lower = row > col
    lower = row >= col
    eye = (row == col).astype(F32)
    tri = tri_ref[...]
    first = c == 0

    def seg_sum(x):
        return jnp.concatenate(
            [_split_dot(x[:, p * LANES:(p + 1) * LANES], ones_ref[...], SEG_SUM_TERMS) for p in range(npair)],
            axis=1)

    def stack2(x):
        return jnp.concatenate([jnp.where(head_a, x, 0.0), jnp.where(head_a, 0.0, x)], axis=0)

    tok = []
    for bi in range(rows):
        zr = zr_ref[bi]
        zl = zl_ref[bi]
        sh_s[bi, 0:halo, 0:wr] = jnp.where(first, 0.0, zrp_ref[bi])
        sh_s[bi, 0:halo, wr:wr + D_LORA_PACK] = jnp.where(first, 0.0, zlp_ref[bi])
        sh_s[bi, halo:halo + chunk, 0:wr] = zr
        sh_s[bi, halo:halo + chunk, wr:wr + D_LORA_PACK] = zl
        zr_prev = sh_s[bi, halo - 1:halo - 1 + chunk, 0:wr]
        zl_prev = sh_s[bi, halo - 1:halo - 1 + chunk, wr:wr + D_LORA_PACK]
        zr = zr + (zr_prev - zr) * mur_ref[...]
        zl = zl + (zl_prev - zl) * mul_ref[...]
        r = zr[:, 0:D_RWKV]
        k = zr[:, D_RWKV:2 * D_RWKV]
        v = zr[:, 2 * D_RWKV:3 * D_RWKV]
        zw = zl[:, 0:LORA_PAD]
        za = zl[:, LORA_PAD:2 * LORA_PAD]
        zg = zl[:, 2 * LORA_PAD:D_LORA_PACK]

        lw = w0_ref[...] + _dot(jnp.tanh(zw).astype(BF16), wup_ref[...])
        ld = -math.exp(-0.5) * jax.nn.sigmoid(lw)
        a = jax.nn.sigmoid(a0_ref[...] + _dot(za.astype(BF16), aup_ref[...]))
        g = _dot(jax.nn.sigmoid(zg).astype(BF16), gup_ref[...])
        kk = k * kk_ref[...]
        k2 = k * (1.0 + (a - 1.0) * ka_ref[...])
        kk = kk / jnp.maximum(jnp.sqrt(seg_sum(kk * kk)), 1e-12)
        bonus = seg_sum(r * k2 * rk_ref[...]) * v
        bb = kk * a

        cum = _split_dot_left(tri, ld, 3)
        cum_end = cum[chunk - 1:chunk, :]
        e_inv = jnp.exp(-cum)
        e_end = jnp.exp(cum_end - cum)
        tok.append(dict(
            r_t=r * jnp.exp(cum), kk_t=kk * jnp.exp(cum - ld), b_h=bb * e_inv, k_h=k2 * e_inv,
            b_a=bb * e_end, k_a=k2 * e_end, a_end=jnp.exp(cum_end), v=v, bonus=bonus, g=g))

    units = [(bi, p) for bi in range(rows) for p in range(npair)]

    def pair(name):
        return [tok[bi][name][:, p * LANES:(p + 1) * LANES] for bi, p in units]

    kks = [stack2(x).astype(BF16) for x in pair("kk_t")]
    bs = [stack2(x).astype(BF16) for x in pair("b_h")]
    ks = [stack2(x).astype(BF16) for x in pair("k_h")]
    rs = [stack2(x).astype(BF16) for x in pair("r_t")]
    vs = [stack2(x).astype(BF16) for x in pair("v")]
    bat = [stack2(x).T.astype(BF16) for x in pair("b_a")]
    kat = [stack2(x).T.astype(BF16) for x in pair("k_a")]
    a_col = [jnp.broadcast_to(x, (LANES, LANES)).T for x in pair("a_end")]
    ids = range(len(units))

    tt = [_dot_nt(jnp.concatenate([kks[i], rs[i]], axis=0), jnp.concatenate([bs[i], ks[i]], axis=0)) for i in ids]
    l_b = [jnp.where(strict_lower, tt[i][0:LANES, 0:LANES], 0.0) for i in ids]
    l_k = [jnp.where(strict_lower, tt[i][0:LANES, LANES:2 * LANES], 0.0).astype(BF16) for i in ids]
    m_b = [jnp.where(lower, tt[i][LANES:2 * LANES, 0:LANES], 0.0).astype(BF16) for i in ids]
    m_k = [jnp.where(lower, tt[i][LANES:2 * LANES, LANES:2 * LANES], 0.0).astype(BF16) for i in ids]

    sel = ((row ^ col) < 2) & ((row & 1) != 0) & ((col & 1) == 0)
    t_inv = [eye - jnp.where(sel, l_b[i], 0.0) for i in ids]
    s = 2
    while s < HEAD_DIM:
        sel = ((row ^ col) < 2 * s) & ((row & s) != 0) & ((col & s) == 0)
        t_b = [t_inv[i].astype(BF16) for i in ids]
        inner = [_dot(jnp.where(sel, l_b[i], 0.0).astype(BF16), t_b[i]).astype(BF16) for i in ids]
        t_inv = [t_inv[i] - _dot(t_b[i], inner[i]) for i in ids]
        s *= 2
    t_b = [t_inv[i].astype(BF16) for i in ids]

    mlv = [_dot(jnp.concatenate([m_k[i], l_k[i]], axis=0), vs[i]) for i in ids]
    y0 = [mlv[i][0:LANES, :] for i in ids]
    lkv = [mlv[i][LANES:2 * LANES, :].astype(BF16) for i in ids]
    w_m = [_dot(t_b[i], kks[i]).astype(BF16) for i in ids]

    h = [h_s[i] for i in ids]
    h_b = [h[i].astype(BF16) for i in ids]
    u_b = [(-_dot(jnp.concatenate([w_m[i], t_b[i]], axis=1),
                  jnp.concatenate([h_b[i], lkv[i]], axis=0))).astype(BF16) for i in ids]
    y = [_dot(jnp.concatenate([rs[i], m_b[i]], axis=1), jnp.concatenate([h_b[i], u_b[i]], axis=0)) + y0[i]
         for i in ids]
    for i in ids:
        h_s[i] = a_col[i] * h[i] + _dot(jnp.concatenate([bat[i], kat[i]], axis=1),
                                        jnp.concatenate([u_b[i], vs[i]], axis=0))

    for bi in range(rows):
        yb = jnp.concatenate(
            [y[bi * npair + p][0:chunk, :] + y[bi * npair + p][chunk:2 * chunk, :] for p in range(npair)], axis=1)
        mean = seg_sum(yb) * (1.0 / HEAD_DIM)
        d = yb - mean
        var = seg_sum(d * d) * (1.0 / HEAD_DIM)
        yn = d * lax.rsqrt(var + RWKV_GN_EPS) * gnw_ref[...] + gnb_ref[...]
        o_ref[bi] = ((yn + tok[bi]["bonus"]) * tok[bi]["g"]).astype(o_ref.dtype)


def _split_dot_left(tri_b, x, terms):
    acc = None
    rem = x
    for _ in range(terms):
        piece = rem.astype(BF16)
        part = _dot(tri_b, piece)
        acc = part if acc is None else acc + part
        rem = rem - piece.astype(F32)
    return acc


BF16_SUBLANES = 16


def _rwkv_ones(chunk):
    t = jnp.arange(chunk)
    tri = (t[:, None] >= t[None, :]).astype(BF16)
    r = jnp.arange(LANES)[:, None]
    c = jnp.arange(LANES)[None, :]
    head_ones = ((r < HEAD_DIM) == (c < HEAD_DIM)).astype(BF16)
    return tri, head_ones


def _rwkv(z3, prm, *, rows, cast=()):
    b, s, _ = z3.shape
    chunk = RWKV_CHUNK
    nc = s // chunk
    nb = b // rows
    wr = 3 * D_RWKV
    rows8 = chunk // 8
    npair = D_RWKV // LANES

    def vec(n):
        return pl.BlockSpec((1, n), lambda i, c: (0, 0))

    def full(a):
        return pl.BlockSpec(a.shape, lambda i, c: (0, 0))

    tri, head_ones = _rwkv_ones(chunk)

    n_steps = nb * nc
    cast_in_specs, cast_out_specs, cast_shapes, cast_steps = [], [], [], []
    for w, layer in cast:
        _, r, cc = w.shape
        assert r % BF16_SUBLANES == 0
        used = max(k for k in range(1, n_steps + 1) if (r // BF16_SUBLANES) % k == 0)
        tr = r // used
        row_blk = lambda i, c, last=used - 1: jnp.minimum(i * nc + c, last)
        cast_in_specs.append(pl.BlockSpec((1, tr, cc), lambda i, c, f=row_blk, l=layer: (l, f(i, c), 0)))
        cast_out_specs.append(pl.BlockSpec((1, tr, cc), lambda i, c, f=row_blk: (0, f(i, c), 0)))
        cast_shapes.append(jax.ShapeDtypeStruct((1, r, cc), BF16))
        cast_steps.append(used)
    kern = functools.partial(_rwkv_kernel, chunk=chunk, rows=rows, n_steps=n_steps, cast_steps=tuple(cast_steps))

    outs = pl.pallas_call(
        kern,
        grid=(nb, nc),
        in_specs=[
            pl.BlockSpec((rows, chunk, wr), lambda i, c: (i, c, COL_RKV // wr)),
            pl.BlockSpec((rows, 8, wr), lambda i, c: (i, jnp.maximum(c * rows8 - 1, 0), COL_RKV // wr)),
            pl.BlockSpec((rows, chunk, D_LORA_PACK), lambda i, c: (i, c, COL_LORA // D_LORA_PACK)),
            pl.BlockSpec((rows, 8, D_LORA_PACK),
                         lambda i, c: (i, jnp.maximum(c * rows8 - 1, 0), COL_LORA // D_LORA_PACK)),
            vec(wr), vec(D_LORA_PACK),
            full(prm["w_up"]), vec(D_RWKV), full(prm["a_up"]), vec(D_RWKV), full(prm["g_up"]),
            vec(D_RWKV), vec(D_RWKV), vec(D_RWKV), vec(D_RWKV), vec(D_RWKV),
            full(tri), full(head_ones),
        ] + cast_in_specs,
        out_specs=[pl.BlockSpec((rows, chunk, D_RWKV), lambda i, c: (i, c, 0))] + cast_out_specs,
        out_shape=[jax.ShapeDtypeStruct((b, s, D_RWKV), BF16)] + cast_shapes,
        scratch_shapes=[
            pltpu.VMEM((rows * npair, LANES, LANES), F32),
            pltpu.VMEM((rows, chunk + 8, wr + D_LORA_PACK), F32),
        ],
        compiler_params=pltpu.CompilerParams(
            dimension_semantics=("arbitrary", "arbitrary"), vmem_limit_bytes=VMEM_LIMIT),
        name="rwkv7",
    )(z3, z3, z3, z3, prm["mu_r"], prm["mu_l"], prm["w_up"], prm["w0"], prm["a_up"], prm["a0"],
      prm["g_up"], prm["k_k"], prm["k_a"], prm["r_k"], prm["gn_w"], prm["gn_b"],
      tri, head_ones, *[w for w, _ in cast])
    return outs[0], list(outs[1:])


def _pad_cols(w, n):
    return jnp.pad(w, ((0, 0),) * (w.ndim - 1) + ((0, n - w.shape[-1]),))


def _pack_w_in_t(wt):
    o = 3 * D_ATT + 3 * D_RWKV
    o2 = o + RWKV_DECAY_LORA
    o3 = o2 + RWKV_AAA_LORA
    pad = ((0, 0), (0, LORA_PAD - RWKV_DECAY_LORA), (0, 0))
    return jnp.concatenate([wt[:, :o], jnp.pad(wt[:, o:o2], pad), jnp.pad(wt[:, o2:o3], pad), wt[:, o3:]], axis=1)


def _pack_mu(mu):
    o = 3 * D_RWKV
    mu_l = jnp.concatenate([
        jnp.pad(mu[o:o + RWKV_DECAY_LORA], (0, LORA_PAD - RWKV_DECAY_LORA)),
        jnp.pad(mu[o + RWKV_DECAY_LORA:o + RWKV_DECAY_LORA + RWKV_AAA_LORA], (0, LORA_PAD - RWKV_AAA_LORA)),
        mu[o + RWKV_DECAY_LORA + RWKV_AAA_LORA:],
    ])
    return mu[None, :o], mu_l[None, :]


def _pad_rows(w, n):
    return jnp.pad(w, ((0, n - w.shape[0]), (0, 0)))


def _rope_tables(seq):
    inv_freq = ROPE_THETA ** (-jnp.arange(0, HEAD_DIM, 2, dtype=F32) / HEAD_DIM)
    ang = jnp.arange(seq, dtype=F32)[:, None] * inv_freq[None, :]
    cos, sin = jnp.cos(ang), jnp.sin(ang)
    reps = LANES // HEAD_DIM
    cos_t = jnp.tile(jnp.concatenate([cos, cos], axis=1), (1, reps))
    sin_t = jnp.tile(jnp.concatenate([-sin, sin], axis=1), (1, reps))
    return cos_t, sin_t


def _tile(n, pref):
    t = min(n, pref)
    assert n % t == 0
    return t


def _row_tile(r, pref):
    return max(k for k in range(BF16_SUBLANES, pref + 1, BF16_SUBLANES) if r % k == 0)


def kernel(x, ffn1_norm_pre, ffn1_norm_post, ffn1_w_gate, ffn1_w_up, ffn1_w_down, mix_norm_pre, mix_norm_post, w_in, w_out, rwkv_mu, rwkv_w_up, rwkv_w0, rwkv_a_up, rwkv_a0, rwkv_g_up, rwkv_k_k, rwkv_k_a, rwkv_r_k, rwkv_gn_w, rwkv_gn_b, pool_w, pool_scale, ffn2_norm_pre, ffn2_norm_post, ffn2_w_gate, ffn2_w_up, ffn2_w_down):
    b, s, d = x.shape
    depth = w_in.shape[0]
    t = b * s
    d_ff = ffn1_w_gate.shape[2]
    assert s % RWKV_CHUNK == 0 and b % RWKV_ROWS == 0 and d == D_ATT + D_RWKV + D_POOL
    tm = _tile(t, 512)
    tm_ffn = _tile(t, 1024)
    tm_proj = _tile(t, 1024)
    tf = _tile(d_ff, 512)
    tn = _tile(D_PACK, D_PACK // 4)
    cos_t, sin_t = _rope_tables(s)
    x2 = x.reshape(t, d).astype(F32)

    w_in_t = jnp.swapaxes(w_in, 1, 2)
    big = (ffn1_w_gate, ffn1_w_up, ffn1_w_down, ffn2_w_gate, ffn2_w_up, ffn2_w_down, w_in_t, w_out)
    wb = [_cast_bf16(w, layer=0, tr=_row_tile(w.shape[1], 256)) for w in big]
    ffn_g = [[g[:, None, :] for g in gs] for gs in
             ((ffn1_norm_pre, ffn1_norm_post), (ffn2_norm_pre, ffn2_norm_post))]
    mix_g_pre, mix_g_post = mix_norm_pre[:, None, :], mix_norm_post[:, None, :]

    def gains(gs, i):
        return [g[i:i + 1] for g in gs]

    for i in range(depth):
        x2 = _ffn(x2, *gains(ffn_g[0], i), *wb[0:3], layer=0, tm=tm_ffn, tf=tf)

        z = _proj(x2, mix_g_pre[i:i + 1], _pack_w_in_t(wb[6]), layer=0, tm=tm_proj, tn=tn)
        z3 = z.reshape(b, s, D_PACK)
        y_att = _attention(z3, cos_t, sin_t)
        mu_r, mu_l = _pack_mu(rwkv_mu[i])
        prm = dict(
            mu_r=mu_r, mu_l=mu_l,
            w_up=_pad_rows(rwkv_w_up[i], LORA_PAD).astype(BF16), w0=rwkv_w0[i][None],
            a_up=_pad_rows(rwkv_a_up[i], LORA_PAD).astype(BF16), a0=rwkv_a0[i][None],
            g_up=rwkv_g_up[i].astype(BF16), k_k=rwkv_k_k[i][None], k_a=rwkv_k_a[i][None],
            r_k=rwkv_r_k[i].reshape(1, D_RWKV), gn_w=rwkv_gn_w[i][None], gn_b=rwkv_gn_b[i][None])
        nxt = [(w, i + 1) for w in big] if i + 1 < depth else []
        y_rwkv, wb_next = _rwkv(z3, prm, rows=RWKV_ROWS, cast=nxt)
        y_pool = _pool(z3, pool_w[i].astype(BF16), pool_scale[i][None])

        x2 = _mixout(x2, y_att.reshape(t, D_ATT), y_rwkv.reshape(t, D_RWKV), y_pool.reshape(t, D_POOL),
                     wb[7], mix_g_post[i:i + 1], layer=0, tm=tm)

        x2 = _ffn(x2, *gains(ffn_g[1], i), *wb[3:6], layer=0, tm=tm_ffn, tf=tf)
        wb = wb_next
    return x2.reshape(b, s, d).astype(x.dtype)
```

```python
import functools
import math

import jax
import jax.numpy as jnp
from jax import lax
from jax.experimental import pallas as pl
from jax.experimental.pallas import tpu as pltpu

F32 = jnp.float32
BF16 = jnp.bfloat16

HEAD_DIM = 64
N_ATT_HEADS = 12
N_RWKV_HEADS = 12
D_ATT = N_ATT_HEADS * HEAD_DIM
D_RWKV = N_RWKV_HEADS * HEAD_DIM
D_POOL = 512
N_POOL_GROUPS = 4
POOL_GROUP = D_POOL // N_POOL_GROUPS
POOL_WINDOWS = (2, 4, 8, 16)
DILATED_BRANCHES = ((128, 1), (512, 4), (2048, 16))
ROPE_THETA = 10000.0
RWKV_DECAY_LORA = 96
RWKV_AAA_LORA = 96
RWKV_GATE_LORA = 256
NORM_EPS = 1e-6
RWKV_GN_EPS = 64e-5

LANES = 128
LORA_PAD = 128
COL_ATT = 0
COL_RKV = 3 * D_ATT
COL_LORA = COL_RKV + 3 * D_RWKV
D_LORA_PACK = 2 * LORA_PAD + RWKV_GATE_LORA
COL_POOL = COL_LORA + D_LORA_PACK
D_PACK = COL_POOL + D_POOL

SEG_SUM_TERMS = 1
RWKV_ROWS = 2
RWKV_CHUNK = 64
ATT_KV_TILE = 512
ATT_GROUP = 2
VMEM_LIMIT = 56 * 1024 * 1024
VMEM_LIMIT_FFN = 62 * 1024 * 1024

NEG_BIG = -1e30
LOG2_E = math.log2(math.e)


def _rms(xf, g):
    return xf * lax.rsqrt(jnp.mean(xf * xf, axis=-1, keepdims=True) + NORM_EPS) * g


def _dot(a, b):
    return jnp.dot(a, b, preferred_element_type=F32)


def _dot_nt(a, b):
    return lax.dot_general(a, b, (((1,), (1,)), ((), ())), preferred_element_type=F32)


def _ffn_kernel(x_ref, gpre_ref, gpost_ref, wg_ref, wu_ref, wd_ref, o_ref, h_ref):
    j = pl.program_id(1)

    @pl.when(j == 0)
    def _():
        h_ref[...] = _rms(x_ref[...], gpre_ref[...]).astype(BF16)
        o_ref[...] = jnp.zeros_like(o_ref)

    half = o_ref.shape[0] // 2
    for r0 in (0, half):
        rows = slice(r0, r0 + half)
        h = h_ref[rows, :]
        g = _dot(h, wg_ref[...])
        u = _dot(h, wu_ref[...])
        a = (g * jax.nn.sigmoid(g) * u).astype(BF16)
        o_ref[rows, :] += _dot(a, wd_ref[...])

    @pl.when(j == pl.num_programs(1) - 1)
    def _():
        o_ref[...] = x_ref[...] + 0.5 * _rms(o_ref[...], gpost_ref[...])


def _ffn(x, g_pre, g_post, wg, wu, wd, *, layer, tm, tf):
    t, d = x.shape
    f = wg.shape[2]
    return pl.pallas_call(
        _ffn_kernel,
        grid=(t // tm, f // tf),
        in_specs=[
            pl.BlockSpec((tm, d), lambda i, j: (i, 0)),
            pl.BlockSpec((None, 1, d), lambda i, j: (layer, 0, 0)),
            pl.BlockSpec((None, 1, d), lambda i, j: (layer, 0, 0)),
            pl.BlockSpec((None, d, tf), lambda i, j: (layer, 0, j)),
            pl.BlockSpec((None, d, tf), lambda i, j: (layer, 0, j)),
            pl.BlockSpec((None, tf, d), lambda i, j: (layer, j, 0)),
        ],
        out_specs=pl.BlockSpec((tm, d), lambda i, j: (i, 0)),
        out_shape=jax.ShapeDtypeStruct((t, d), F32),
        scratch_shapes=[pltpu.VMEM((tm, d), BF16)],
        compiler_params=pltpu.CompilerParams(
            dimension_semantics=("parallel", "arbitrary"), vmem_limit_bytes=VMEM_LIMIT_FFN),
        name="ffn",
    )(x, g_pre, g_post, wg, wu, wd)


def _proj_kernel(x_ref, g_ref, wt_ref, o_ref, h_ref):
    @pl.when(pl.program_id(1) == 0)
    def _():
        h_ref[...] = _rms(x_ref[...], g_ref[...]).astype(BF16)

    o_ref[...] = _dot_nt(h_ref[...], wt_ref[...])


def _proj(x, g, wt, *, layer, tm, tn):
    t, d = x.shape
    n = wt.shape[1]
    return pl.pallas_call(
        _proj_kernel,
        grid=(t // tm, n // tn),
        in_specs=[
            pl.BlockSpec((tm, d), lambda i, j: (i, 0)),
            pl.BlockSpec((None, 1, d), lambda i, j: (layer, 0, 0)),
            pl.BlockSpec((None, tn, d), lambda i, j: (layer, j, 0)),
        ],
        out_specs=pl.BlockSpec((tm, tn), lambda i, j: (i, j)),
        out_shape=jax.ShapeDtypeStruct((t, n), F32),
        scratch_shapes=[pltpu.VMEM((tm, d), BF16)],
        compiler_params=pltpu.CompilerParams(
            dimension_semantics=("parallel", "arbitrary"), vmem_limit_bytes=VMEM_LIMIT),
        name="proj_in",
    )(x, g, wt)


def _mixout_kernel(x_ref, ya_ref, yr_ref, yp_ref, wa_ref, wr_ref, wp_ref, g_ref, o_ref):
    mix = _dot(ya_ref[...], wa_ref[...]) + _dot(yr_ref[...], wr_ref[...]) + _dot(yp_ref[...], wp_ref[...])
    o_ref[...] = x_ref[...] + _rms(mix, g_ref[...])


def _mixout(x, ya, yr, yp, wo, g, *, layer, tm):
    t, d = x.shape
    da, dr, dp = ya.shape[1], yr.shape[1], yp.shape[1]
    assert da == dr and (da + dr) % dp == 0
    return pl.pallas_call(
        _mixout_kernel,
        grid=(t // tm,),
        in_specs=[
            pl.BlockSpec((tm, d), lambda i: (i, 0)),
            pl.BlockSpec((tm, da), lambda i: (i, 0)),
            pl.BlockSpec((tm, dr), lambda i: (i, 0)),
            pl.BlockSpec((tm, dp), lambda i: (i, 0)),
            pl.BlockSpec((None, da, d), lambda i: (layer, 0, 0)),
            pl.BlockSpec((None, dr, d), lambda i: (layer, 1, 0)),
            pl.BlockSpec((None, dp, d), lambda i: (layer, (da + dr) // dp, 0)),
            pl.BlockSpec((None, 1, d), lambda i: (layer, 0, 0)),
        ],
        out_specs=pl.BlockSpec((tm, d), lambda i: (i, 0)),
        out_shape=jax.ShapeDtypeStruct((t, d), F32),
        compiler_params=pltpu.CompilerParams(
            dimension_semantics=("parallel",), vmem_limit_bytes=VMEM_LIMIT),
        name="proj_out",
    )(x, ya, yr, yp, wo, wo, wo, g)


def _cast_kernel(w_ref, o_ref):
    o_ref[...] = w_ref[...].astype(BF16)


def _cast_bf16(w, *, layer, tr):
    _, r, c = w.shape
    return pl.pallas_call(
        _cast_kernel,
        grid=(r // tr,),
        in_specs=[pl.BlockSpec((1, tr, c), lambda j: (layer, j, 0))],
        out_specs=pl.BlockSpec((1, tr, c), lambda j: (0, j, 0)),
        out_shape=jax.ShapeDtypeStruct((1, r, c), BF16),
        compiler_params=pltpu.CompilerParams(
            dimension_semantics=("parallel",), vmem_limit_bytes=VMEM_LIMIT),
        name="cast_bf16",
    )(w)


def _att_bias_tile(base, tq, tk):
    row = lax.broadcasted_iota(jnp.int32, (tq, tk), 0)
    col = lax.broadcasted_iota(jnp.int32, (tq, tk), 1)
    delta = base + row - col
    cnt = jnp.zeros((tq, tk), jnp.int32)
    for window, dil in DILATED_BRANCHES:
        hit = (delta >= 0) & (delta <= window) & ((delta & (dil - 1)) == 0)
        cnt = cnt + hit.astype(jnp.int32)
    return jnp.where(cnt == 3, math.log2(3.0),
                     jnp.where(cnt == 2, 1.0, jnp.where(cnt == 1, 0.0, NEG_BIG))).astype(BF16)


def _att_kernel(q_ref, k_ref, v_ref, cos_ref, sin_ref, o_ref,
                qa_s, qb_s, k_s, va_s, vb_s, bias_s, *, seq, tq, tk, group):
    nq = seq // tq
    ratio = tk // tq

    @pl.when(pl.program_id(1) == 0)
    def _():
        for off in range(nq):
            bias_s[off] = _att_bias_tile(off * tq, tq, tk)

    lane = lax.broadcasted_iota(jnp.int32, (tq, LANES), 1)
    first_half = (lane & (HEAD_DIM - 1)) < (HEAD_DIM // 2)
    head_a = lane < HEAD_DIM

    def rope(x, cos, sin):
        swapped = jnp.where(first_half, pltpu.roll(x, LANES - HEAD_DIM // 2, 1), pltpu.roll(x, HEAD_DIM // 2, 1))
        return x * cos + swapped * sin

    def prep(i, carry):
        rows = pl.ds(pl.multiple_of(i * tq, tq), tq)
        cos = cos_ref[rows, :]
        sin = sin_ref[rows, :]
        q = rope(q_ref[0, rows, :], cos, sin) * (HEAD_DIM ** -0.5 * LOG2_E)
        k = rope(k_ref[0, rows, :], cos, sin)
        v = v_ref[0, rows, :]
        qa_s[rows, :] = jnp.where(head_a, q, 0.0).astype(BF16)
        qb_s[rows, :] = jnp.where(head_a, 0.0, q).astype(BF16)
        k_s[rows, :] = k.astype(BF16)
        va_s[rows, :] = jnp.where(head_a, v, 1.0).astype(BF16)
        vb_s[rows, :] = jnp.where(head_a, 1.0, v).astype(BF16)
        return carry

    lax.fori_loop(0, nq, prep, 0)

    def tile_max(s):
        m = s[:, 0:LANES]
        for t in range(1, s.shape[1] // LANES):
            m = jnp.maximum(m, s[:, t * LANES:(t + 1) * LANES])
        return m

    q_s = (qa_s, qb_s)
    v_s = (va_s, vb_s)
    m_run = {}
    acc = {}
    for kj in range(seq // tk):
        blocks = list(range(ratio * kj, nq))
        for g0 in range(0, len(blocks), group):
            chains = [(qi, hd) for qi in blocks[g0:g0 + group] for hd in range(2)]
            width = {qi: (tq if qi == ratio * kj else tk) for qi, _ in chains}
            s = {}
            for qi, hd in chains:
                krows = slice(kj * tk, kj * tk + width[qi])
                s[qi, hd] = (_dot_nt(q_s[hd][qi * tq:(qi + 1) * tq, :], k_s[krows, :]).astype(BF16)
                             + bias_s[qi - ratio * kj, :, 0:width[qi]])
            p = {}
            for ch in chains:
                m_tile = jnp.max(tile_max(s[ch]).astype(F32), axis=-1, keepdims=True)
                if kj == 0:
                    m_new = m_tile
                else:
                    m_new = jnp.maximum(m_run[ch], m_tile)
                    acc[ch] = jnp.exp2(m_run[ch] - m_new) * acc[ch]
                m_run[ch] = m_new
                p[ch] = jnp.exp2(s[ch] - m_new.astype(BF16))
            for ch in chains:
                pv = _dot(p[ch], v_s[ch[1]][kj * tk:kj * tk + width[ch[0]], :])
                acc[ch] = pv if kj == 0 else acc[ch] + pv

    for qi in range(nq):
        o_a = acc[qi, 0] / pltpu.roll(acc[qi, 0], HEAD_DIM, 1)
        o_b = acc[qi, 1] / pltpu.roll(acc[qi, 1], HEAD_DIM, 1)
        o_ref[0, qi * tq:(qi + 1) * tq, :] = jnp.where(head_a, o_a, o_b).astype(o_ref.dtype)


def _attention(z3, cos_t, sin_t):
    b, s, _ = z3.shape
    tk = min(ATT_KV_TILE, s)
    tq = tk // 2
    npair = D_ATT // LANES
    kern = functools.partial(_att_kernel, seq=s, tq=tq, tk=tk, group=ATT_GROUP)
    return pl.pallas_call(
        kern,
        grid=(b, npair),
        in_specs=[
            pl.BlockSpec((1, s, LANES), lambda i, p: (i, 0, COL_ATT // LANES + p)),
            pl.BlockSpec((1, s, LANES), lambda i, p: (i, 0, (COL_ATT + D_ATT) // LANES + p)),
            pl.BlockSpec((1, s, LANES), lambda i, p: (i, 0, (COL_ATT + 2 * D_ATT) // LANES + p)),
            pl.BlockSpec((s, LANES), lambda i, p: (0, 0)),
            pl.BlockSpec((s, LANES), lambda i, p: (0, 0)),
        ],
        out_specs=pl.BlockSpec((1, s, LANES), lambda i, p: (i, 0, p)),
        out_shape=jax.ShapeDtypeStruct((b, s, D_ATT), BF16),
        scratch_shapes=[pltpu.VMEM((s, LANES), BF16)] * 5 + [pltpu.VMEM((s // tq, tq, tk), BF16)],
        compiler_params=pltpu.CompilerParams(
            dimension_semantics=("parallel", "arbitrary"), vmem_limit_bytes=VMEM_LIMIT),
        name="dilated_attention",
    )(z3, z3, z3, cos_t, sin_t)


POOL_PAD = 16


def _pool_kernel(u_ref, w_ref, sc_ref, o_ref, pad_s, *, seq):
    t1 = (lax.broadcasted_iota(jnp.int32, (seq, POOL_GROUP), 0) + 1).astype(F32)
    pad_s[0:POOL_PAD, :] = jnp.zeros((POOL_PAD, POOL_GROUP), F32)
    for gi, w in enumerate(POOL_WINDOWS):
        cols = slice(gi * POOL_GROUP, (gi + 1) * POOL_GROUP)
        u = u_ref[0, :, cols]
        s = u
        k = 1
        while k < w:
            pad_s[POOL_PAD:POOL_PAD + seq, :] = s
            s = s + pad_s[POOL_PAD - k:POOL_PAD - k + seq, :]
            k *= 2
        p = s / jnp.minimum(t1, float(w)) - u
        y = _dot(p.astype(BF16), w_ref[gi])
        o_ref[0, :, cols] = (y * sc_ref[:, cols]).astype(o_ref.dtype)


def _pool(z3, pool_w, pool_scale):
    b, s, _ = z3.shape
    kern = functools.partial(_pool_kernel, seq=s)
    return pl.pallas_call(
        kern,
        grid=(b,),
        in_specs=[
            pl.BlockSpec((1, s, D_POOL), lambda i: (i, 0, COL_POOL // D_POOL)),
            pl.BlockSpec(pool_w.shape, lambda i: (0, 0, 0)),
            pl.BlockSpec((1, D_POOL), lambda i: (0, 0)),
        ],
        out_specs=pl.BlockSpec((1, s, D_POOL), lambda i: (i, 0, 0)),
        out_shape=jax.ShapeDtypeStruct((b, s, D_POOL), BF16),
        scratch_shapes=[pltpu.VMEM((s + POOL_PAD, POOL_GROUP), F32)],
        compiler_params=pltpu.CompilerParams(
            dimension_semantics=("parallel",), vmem_limit_bytes=VMEM_LIMIT),
        name="pool_mixer",
    )(z3, pool_w, pool_scale)


def _split_dot(x, ones_b, terms):
    acc = None
    rem = x
    for _ in range(terms):
        piece = rem.astype(BF16)
        part = _dot(piece, ones_b)
        acc = part if acc is None else acc + part
        rem = rem - piece.astype(F32)
    return acc


N_RWKV_IN = 18


def _rwkv_kernel(*refs, chunk, rows, n_steps, cast_steps):
    (zr_ref, zrp_ref, zl_ref, zlp_ref, mur_ref, mul_ref, wup_ref, w0_ref, aup_ref, a0_ref,
     gup_ref, kk_ref, ka_ref, rk_ref, gnw_ref, gnb_ref, tri_ref, ones_ref) = refs[:N_RWKV_IN]
    n_cast = len(cast_steps)
    cast_in = refs[N_RWKV_IN:N_RWKV_IN + n_cast]
    o_ref = refs[N_RWKV_IN + n_cast]
    cast_out = refs[N_RWKV_IN + n_cast + 1:N_RWKV_IN + 2 * n_cast + 1]
    h_s, sh_s = refs[N_RWKV_IN + 2 * n_cast + 1:]
    c = pl.program_id(1)
    npair = D_RWKV // LANES
    halo = 8
    wr = 3 * D_RWKV

    step = pl.program_id(0) * pl.num_programs(1) + c
    for src, dst, used in zip(cast_in, cast_out, cast_steps):
        if used == n_steps:
            dst[...] = src[...].astype(BF16)
        else:
            @pl.when(step < used)
            def _(src=src, dst=dst):
                dst[...] = src[...].astype(BF16)

    @pl.when(c == 0)
    def _():
        h_s[...] = jnp.zeros_like(h_s)

    row = lax.broadcasted_iota(jnp.int32, (LANES, LANES), 0)
    col = lax.broadcasted_iota(jnp.int32, (LANES, LANES), 1)
    lane = lax.broadcasted_iota(jnp.int32, (chunk, LANES), 1)
    head_a = lane < HEAD_DIM
    strict_lower = row > col
    lower = row >= col
    eye = (row == col).astype(F32)
    tri = tri_ref[...]
    first = c == 0

    def seg_sum(x):
        return jnp.concatenate(
            [_split_dot(x[:, p * LANES:(p + 1) * LANES], ones_ref[...], SEG_SUM_TERMS) for p in range(npair)],
            axis=1)

    def stack2(x):
        return jnp.concatenate([jnp.where(head_a, x, 0.0), jnp.where(head_a, 0.0, x)], axis=0)

    tok = []
    for bi in range(rows):
        zr = zr_ref[bi]
        zl = zl_ref[bi]
        sh_s[bi, 0:halo, 0:wr] = jnp.where(first, 0.0, zrp_ref[bi])
        sh_s[bi, 0:halo, wr:wr + D_LORA_PACK] = jnp.where(first, 0.0, zlp_ref[bi])
        sh_s[bi, halo:halo + chunk, 0:wr] = zr
        sh_s[bi, halo:halo + chunk, wr:wr + D_LORA_PACK] = zl
        zr_prev = sh_s[bi, halo - 1:halo - 1 + chunk, 0:wr]
        zl_prev = sh_s[bi, halo - 1:halo - 1 + chunk, wr:wr + D_LORA_PACK]
        zr = zr + (zr_prev - zr) * mur_ref[...]
        zl = zl + (zl_prev - zl) * mul_ref[...]
        r = zr[:, 0:D_RWKV]
        k = zr[:, D_RWKV:2 * D_RWKV]
        v = zr[:, 2 * D_RWKV:3 * D_RWKV]
        zw = zl[:, 0:LORA_PAD]
        za = zl[:, LORA_PAD:2 * LORA_PAD]
        zg = zl[:, 2 * LORA_PAD:D_LORA_PACK]

        lw = w0_ref[...] + _dot(jnp.tanh(zw).astype(BF16), wup_ref[...])
        ld = -math.exp(-0.5) * jax.nn.sigmoid(lw)
        a = jax.nn.sigmoid(a0_ref[...] + _dot(za.astype(BF16), aup_ref[...]))
        g = _dot(jax.nn.sigmoid(zg).astype(BF16), gup_ref[...])
        kk = k * kk_ref[...]
        k2 = k * (1.0 + (a - 1.0) * ka_ref[...])
        kk = kk / jnp.maximum(jnp.sqrt(seg_sum(kk * kk)), 1e-12)
        bonus = seg_sum(r * k2 * rk_ref[...]) * v
        bb = kk * a

        cum = _split_dot_left(tri, ld, 3)
        cum_end = cum[chunk - 1:chunk, :]
        e_inv = jnp.exp(-cum)
        e_end = jnp.exp(cum_end - cum)
        tok.append(dict(
            r_t=r * jnp.exp(cum), kk_t=kk * jnp.exp(cum - ld), b_h=bb * e_inv, k_h=k2 * e_inv,
            b_a=bb * e_end, k_a=k2 * e_end, a_end=jnp.exp(cum_end), v=v, bonus=bonus, g=g))

    units = [(bi, p) for bi in range(rows) for p in range(npair)]

    def pair(name):
        return [tok[bi][name][:, p * LANES:(p + 1) * LANES] for bi, p in units]

    kks = [stack2(x).astype(BF16) for x in pair("kk_t")]
    bs = [stack2(x).astype(BF16) for x in pair("b_h")]
    ks = [stack2(x).astype(BF16) for x in pair("k_h")]
    rs = [stack2(x).astype(BF16) for x in pair("r_t")]
    vs = [stack2(x).astype(BF16) for x in pair("v")]
    bat = [stack2(x).T.astype(BF16) for x in pair("b_a")]
    kat = [stack2(x).T.astype(BF16) for x in pair("k_a")]
    a_col = [jnp.broadcast_to(x, (LANES, LANES)).T for x in pair("a_end")]
    ids = range(len(units))

    tt = [_dot_nt(jnp.concatenate([kks[i], rs[i]], axis=0), jnp.concatenate([bs[i], ks[i]], axis=0)) for i in ids]
    l_b = [jnp.where(strict_lower, tt[i][0:LANES, 0:LANES], 0.0) for i in ids]
    l_k = [jnp.where(strict_lower, tt[i][0:LANES, LANES:2 * LANES], 0.0).astype(BF16) for i in ids]
    m_b = [jnp.where(lower, tt[i][LANES:2 * LANES, 0:LANES], 0.0).astype(BF16) for i in ids]
    m_k = [jnp.where(lower, tt[i][LANES:2 * LANES, LANES:2 * LANES], 0.0).astype(BF16) for i in ids]

    sel = ((row ^ col) < 2) & ((row & 1) != 0) & ((col & 1) == 0)
    t_inv = [eye - jnp.where(sel, l_b[i], 0.0) for i in ids]
    s = 2
    while s < HEAD_DIM:
        sel = ((row ^ col) < 2 * s) & ((row & s) != 0) & ((col & s) == 0)
        t_b = [t_inv[i].astype(BF16) for i in ids]
        inner = [_dot(jnp.where(sel, l_b[i], 0.0).astype(BF16), t_b[i]).astype(BF16) for i in ids]
        t_inv = [t_inv[i] - _dot(t_b[i], inner[i]) for i in ids]
        s *= 2
    t_b = [t_inv[i].astype(BF16) for i in ids]

    mlv = [_dot(jnp.concatenate([m_k[i], l_k[i]], axis=0), vs[i]) for i in ids]
    y0 = [mlv[i][0:LANES, :] for i in ids]
    lkv = [mlv[i][LANES:2 * LANES, :].astype(BF16) for i in ids]
    w_m = [_dot(t_b[i], kks[i]).astype(BF16) for i in ids]

    h = [h_s[i] for i in ids]
    h_b = [h[i].astype(BF16) for i in ids]
    u_b = [(-_dot(jnp.concatenate([w_m[i], t_b[i]], axis=1),
                  jnp.concatenate([h_b[i], lkv[i]], axis=0))).astype(BF16) for i in ids]
    y = [_dot(jnp.concatenate([rs[i], m_b[i]], axis=1), jnp.concatenate([h_b[i], u_b[i]], axis=0)) + y0[i]
         for i in ids]
    for i in ids:
        h_s[i] = a_col[i] * h[i] + _dot(jnp.concatenate([bat[i], kat[i]], axis=1),
                                        jnp.concatenate([u_b[i], vs[i]], axis=0))

    for bi in range(rows):
        yb = jnp.concatenate(
            [y[bi * npair + p][0:chunk, :] + y[bi * npair + p][chunk:2 * chunk, :] for p in range(npair)], axis=1)
        mean = seg_sum(yb) * (1.0 / HEAD_DIM)
        d = yb - mean
        var = seg_sum(d * d) * (1.0 / HEAD_DIM)
        yn = d * lax.rsqrt(var + RWKV_GN_EPS) * gnw_ref[...] + gnb_ref[...]
        o_ref[bi] = ((yn + tok[bi]["bonus"]) * tok[bi]["g"]).astype(o_ref.dtype)


def _split_dot_left(tri_b, x, terms):
    acc = None
    rem = x
    for _ in range(terms):
        piece = rem.astype(BF16)
        part = _dot(tri_b, piece)
        acc = part if acc is None else acc + part
        rem = rem - piece.astype(F32)
    return acc


BF16_SUBLANES = 16


def _rwkv_ones(chunk):
    t = jnp.arange(chunk)
    tri = (t[:, None] >= t[None, :]).astype(BF16)
    r = jnp.arange(LANES)[:, None]
    c = jnp.arange(LANES)[None, :]
    head_ones = ((r < HEAD_DIM) == (c < HEAD_DIM)).astype(BF16)
    return tri, head_ones


def _rwkv(z3, prm, *, rows, cast=()):
    b, s, _ = z3.shape
    chunk = RWKV_CHUNK
    nc = s // chunk
    nb = b // rows
    wr = 3 * D_RWKV
    rows8 = chunk // 8
    npair = D_RWKV // LANES

    def vec(n):
        return pl.BlockSpec((1, n), lambda i, c: (0, 0))

    def full(a):
        return pl.BlockSpec(a.shape, lambda i, c: (0, 0))

    tri, head_ones = _rwkv_ones(chunk)

    n_steps = nb * nc
    cast_in_specs, cast_out_specs, cast_shapes, cast_steps = [], [], [], []
    for w, layer in cast:
        _, r, cc = w.shape
        assert r % BF16_SUBLANES == 0
        used = max(k for k in range(1, n_steps + 1) if (r // BF16_SUBLANES) % k == 0)
        tr = r // used
        row_blk = lambda i, c, last=used - 1: jnp.minimum(i * nc + c, last)
        cast_in_specs.append(pl.BlockSpec((1, tr, cc), lambda i, c, f=row_blk, l=layer: (l, f(i, c), 0)))
        cast_out_specs.append(pl.BlockSpec((1, tr, cc), lambda i, c, f=row_blk: (0, f(i, c), 0)))
        cast_shapes.append(jax.ShapeDtypeStruct((1, r, cc), BF16))
        cast_steps.append(used)
    kern = functools.partial(_rwkv_kernel, chunk=chunk, rows=rows, n_steps=n_steps, cast_steps=tuple(cast_steps))

    outs = pl.pallas_call(
        kern,
        grid=(nb, nc),
        in_specs=[
            pl.BlockSpec((rows, chunk, wr), lambda i, c: (i, c, COL_RKV // wr)),
            pl.BlockSpec((rows, 8, wr), lambda i, c: (i, jnp.maximum(c * rows8 - 1, 0), COL_RKV // wr)),
            pl.BlockSpec((rows, chunk, D_LORA_PACK), lambda i, c: (i, c, COL_LORA // D_LORA_PACK)),
            pl.BlockSpec((rows, 8, D_LORA_PACK),
                         lambda i, c: (i, jnp.maximum(c * rows8 - 1, 0), COL_LORA // D_LORA_PACK)),
            vec(wr), vec(D_LORA_PACK),
            full(prm["w_up"]), vec(D_RWKV), full(prm["a_up"]), vec(D_RWKV), full(prm["g_up"]),
            vec(D_RWKV), vec(D_RWKV), vec(D_RWKV), vec(D_RWKV), vec(D_RWKV),
            full(tri), full(head_ones),
        ] + cast_in_specs,
        out_specs=[pl.BlockSpec((rows, chunk, D_RWKV), lambda i, c: (i, c, 0))] + cast_out_specs,
        out_shape=[jax.ShapeDtypeStruct((b, s, D_RWKV), BF16)] + cast_shapes,
        scratch_shapes=[
            pltpu.VMEM((rows * npair, LANES, LANES), F32),
            pltpu.VMEM((rows, chunk + 8, wr + D_LORA_PACK), F32),
        ],
        compiler_params=pltpu.CompilerParams(
            dimension_semantics=("arbitrary", "arbitrary"), vmem_limit_bytes=VMEM_LIMIT),
        name="rwkv7",
    )(z3, z3, z3, z3, prm["mu_r"], prm["mu_l"], prm["w_up"], prm["w0"], prm["a_up"], prm["a0"],
      prm["g_up"], prm["k_k"], prm["k_a"], prm["r_k"], prm["gn_w"], prm["gn_b"],
      tri, head_ones, *[w for w, _ in cast])
    return outs[0], list(outs[1:])


def _pad_cols(w, n):
    return jnp.pad(w, ((0, 0),) * (w.ndim - 1) + ((0, n - w.shape[-1]),))


def _pack_w_in_t(wt):
    o = 3 * D_ATT + 3 * D_RWKV
    o2 = o + RWKV_DECAY_LORA
    o3 = o2 + RWKV_AAA_LORA
    pad = ((0, 0), (0, LORA_PAD - RWKV_DECAY_LORA), (0, 0))
    return jnp.concatenate([wt[:, :o], jnp.pad(wt[:, o:o2], pad), jnp.pad(wt[:, o2:o3], pad), wt[:, o3:]], axis=1)


def _pack_mu(mu):
    o = 3 * D_RWKV
    mu_l = jnp.concatenate([
        jnp.pad(mu[o:o + RWKV_DECAY_LORA], (0, LORA_PAD - RWKV_DECAY_LORA)),
        jnp.pad(mu[o + RWKV_DECAY_LORA:o + RWKV_DECAY_LORA + RWKV_AAA_LORA], (0, LORA_PAD - RWKV_AAA_LORA)),
        mu[o + RWKV_DECAY_LORA + RWKV_AAA_LORA:],
    ])
    return mu[None, :o], mu_l[None, :]


def _pad_rows(w, n):
    return jnp.pad(w, ((0, n - w.shape[0]), (0, 0)))


def _rope_tables(seq):
    inv_freq = ROPE_THETA ** (-jnp.arange(0, HEAD_DIM, 2, dtype=F32) / HEAD_DIM)
    ang = jnp.arange(seq, dtype=F32)[:, None] * inv_freq[None, :]
    cos, sin = jnp.cos(ang), jnp.sin(ang)
    reps = LANES // HEAD_DIM
    cos_t = jnp.tile(jnp.concatenate([cos, cos], axis=1), (1, reps))
    sin_t = jnp.tile(jnp.concatenate([-sin, sin], axis=1), (1, reps))
    return cos_t, sin_t


def _tile(n, pref):
    t = min(n, pref)
    assert n % t == 0
    return t


def _row_tile(r, pref):
    return max(k for k in range(BF16_SUBLANES, pref + 1, BF16_SUBLANES) if r % k == 0)


def kernel(x, ffn1_norm_pre, ffn1_norm_post, ffn1_w_gate, ffn1_w_up, ffn1_w_down, mix_norm_pre, mix_norm_post, w_in, w_out, rwkv_mu, rwkv_w_up, rwkv_w0, rwkv_a_up, rwkv_a0, rwkv_g_up, rwkv_k_k, rwkv_k_a, rwkv_r_k, rwkv_gn_w, rwkv_gn_b, pool_w, pool_scale, ffn2_norm_pre, ffn2_norm_post, ffn2_w_gate, ffn2_w_up, ffn2_w_down):
    b, s, d = x.shape
    depth = w_in.shape[0]
    t = b * s
    d_ff = ffn1_w_gate.shape[2]
    assert s % RWKV_CHUNK == 0 and b % RWKV_ROWS == 0 and d == D_ATT + D_RWKV + D_POOL
    tm = _tile(t, 512)
    tm_ffn = _tile(t, 1024)
    tm_proj = _tile(t, 512)
    tf = _tile(d_ff, 512)
    tn = _tile(D_PACK, D_PACK // 2)
    cos_t, sin_t = _rope_tables(s)
    x2 = x.reshape(t, d).astype(F32)

    w_in_t = jnp.swapaxes(w_in, 1, 2)
    big = (ffn1_w_gate, ffn1_w_up, ffn1_w_down, ffn2_w_gate, ffn2_w_up, ffn2_w_down, w_in_t, w_out)
    wb = [_cast_bf16(w, layer=0, tr=_row_tile(w.shape[1], 256)) for w in big]
    ffn_g = [[g[:, None, :] for g in gs] for gs in
             ((ffn1_norm_pre, ffn1_norm_post), (ffn2_norm_pre, ffn2_norm_post))]
    mix_g_pre, mix_g_post = mix_norm_pre[:, None, :], mix_norm_post[:, None, :]

    def gains(gs, i):
        return [g[i:i + 1] for g in gs]

    for i in range(depth):
        x2 = _ffn(x2, *gains(ffn_g[0], i), *wb[0:3], layer=0, tm=tm_ffn, tf=tf)

        z = _proj(x2, mix_g_pre[i:i + 1], _pack_w_in_t(wb[6]), layer=0, tm=tm_proj, tn=tn)
        z3 = z.reshape(b, s, D_PACK)
        y_att = _attention(z3, cos_t, sin_t)
        mu_r, mu_l = _pack_mu(rwkv_mu[i])
        prm = dict(
            mu_r=mu_r, mu_l=mu_l,
            w_up=_pad_rows(rwkv_w_up[i], LORA_PAD).astype(BF16), w0=rwkv_w0[i][None],
            a_up=_pad_rows(rwkv_a_up[i], LORA_PAD).astype(BF16), a0=rwkv_a0[i][None],
            g_up=rwkv_g_up[i].astype(BF16), k_k=rwkv_k_k[i][None], k_a=rwkv_k_a[i][None],
            r_k=rwkv_r_k[i].reshape(1, D_RWKV), gn_w=rwkv_gn_w[i][None], gn_b=rwkv_gn_b[i][None])
        nxt = [(w, i + 1) for w in big] if i + 1 < depth else []
        y_rwkv, wb_next = _rwkv(z3, prm, rows=RWKV_ROWS, cast=nxt)
        y_pool = _pool(z3, pool_w[i].astype(BF16), pool_scale[i][None])

        x2 = _mixout(x2, y_att.reshape(t, D_ATT), y_rwkv.reshape(t, D_RWKV), y_pool.reshape(t, D_POOL),
                     wb[7], mix_g_post[i:i + 1], layer=0, tm=tm)

        x2 = _ffn(x2, *gains(ffn_g[1], i), *wb[3:6], layer=0, tm=tm_ffn, tf=tf)
        wb = wb_next
    return x2.reshape(b, s, d).astype(x.dtype)
```

```python
import functools
import math

import jax
import jax.numpy as jnp
from jax import lax
from jax.experimental import pallas as pl
from jax.experimental.pallas import tpu as pltpu

F32 = jnp.float32
BF16 = jnp.bfloat16

HEAD_DIM = 64
N_ATT_HEADS = 12
N_RWKV_HEADS = 12
D_ATT = N_ATT_HEADS * HEAD_DIM
D_RWKV = N_RWKV_HEADS * HEAD_DIM
D_POOL = 512
N_POOL_GROUPS = 4
POOL_GROUP = D_POOL // N_POOL_GROUPS
POOL_WINDOWS = (2, 4, 8, 16)
DILATED_BRANCHES = ((128, 1), (512, 4), (2048, 16))
ROPE_THETA = 10000.0
RWKV_DECAY_LORA = 96
RWKV_AAA_LORA = 96
RWKV_GATE_LORA = 256
NORM_EPS = 1e-6
RWKV_GN_EPS = 64e-5

LANES = 128
LORA_PAD = 128
COL_ATT = 0
COL_RKV = 3 * D_ATT
COL_LORA = COL_RKV + 3 * D_RWKV
D_LORA_PACK = 2 * LORA_PAD + RWKV_GATE_LORA
COL_POOL = COL_LORA + D_LORA_PACK
D_PACK = COL_POOL + D_POOL

SEG_SUM_TERMS = 1
RWKV_ROWS = 2
RWKV_CHUNK = 64
ATT_KV_TILE = 512
ATT_GROUP = 2
VMEM_LIMIT = 56 * 1024 * 1024
VMEM_LIMIT_FFN = 62 * 1024 * 1024

NEG_BIG = -1e30
LOG2_E = math.log2(math.e)


def _rms(xf, g):
    return xf * lax.rsqrt(jnp.mean(xf * xf, axis=-1, keepdims=True) + NORM_EPS) * g


def _dot(a, b):
    return jnp.dot(a, b, preferred_element_type=F32)


def _dot_nt(a, b):
    return lax.dot_general(a, b, (((1,), (1,)), ((), ())), preferred_element_type=F32)


def _ffn_kernel(x_ref, gpre_ref, gpost_ref, wg_ref, wu_ref, wd_ref, o_ref, h_ref):
    j = pl.program_id(1)

    @pl.when(j == 0)
    def _():
        h_ref[...] = _rms(x_ref[...], gpre_ref[...]).astype(BF16)
        o_ref[...] = jnp.zeros_like(o_ref)

    half = o_ref.shape[0] // 2
    for r0 in (0, half):
        rows = slice(r0, r0 + half)
        h = h_ref[rows, :]
        g = _dot(h, wg_ref[...])
        u = _dot(h, wu_ref[...])
        a = (g * jax.nn.sigmoid(g) * u).astype(BF16)
        o_ref[rows, :] += _dot(a, wd_ref[...])

    @pl.when(j == pl.num_programs(1) - 1)
    def _():
        o_ref[...] = x_ref[...] + 0.5 * _rms(o_ref[...], gpost_ref[...])


def _ffn(x, g_pre, g_post, wg, wu, wd, *, layer, tm, tf):
    t, d = x.shape
    f = wg.shape[2]
    return pl.pallas_call(
        _ffn_kernel,
        grid=(t // tm, f // tf),
        in_specs=[
            pl.BlockSpec((tm, d), lambda i, j: (i, 0)),
            pl.BlockSpec((None, 1, d), lambda i, j: (layer, 0, 0)),
            pl.BlockSpec((None, 1, d), lambda i, j: (layer, 0, 0)),
            pl.BlockSpec((None, d, tf), lambda i, j: (layer, 0, j)),
            pl.BlockSpec((None, d, tf), lambda i, j: (layer, 0, j)),
            pl.BlockSpec((None, tf, d), lambda i, j: (layer, j, 0)),
        ],
        out_specs=pl.BlockSpec((tm, d), lambda i, j: (i, 0)),
        out_shape=jax.ShapeDtypeStruct((t, d), F32),
        scratch_shapes=[pltpu.VMEM((tm, d), BF16)],
        compiler_params=pltpu.CompilerParams(
            dimension_semantics=("parallel", "arbitrary"), vmem_limit_bytes=VMEM_LIMIT_FFN),
        name="ffn",
    )(x, g_pre, g_post, wg, wu, wd)


def _proj_kernel(x_ref, g_ref, wt_ref, o_ref, h_ref):
    @pl.when(pl.program_id(1) == 0)
    def _():
        h_ref[...] = _rms(x_ref[...], g_ref[...]).astype(BF16)

    o_ref[...] = _dot_nt(h_ref[...], wt_ref[...])


def _proj(x, g, wt, *, layer, tm, tn):
    t, d = x.shape
    n = wt.shape[1]
    return pl.pallas_call(
        _proj_kernel,
        grid=(t // tm, n // tn),
        in_specs=[
            pl.BlockSpec((tm, d), lambda i, j: (i, 0)),
            pl.BlockSpec((None, 1, d), lambda i, j: (layer, 0, 0)),
            pl.BlockSpec((None, tn, d), lambda i, j: (layer, j, 0)),
        ],
        out_specs=pl.BlockSpec((tm, tn), lambda i, j: (i, j)),
        out_shape=jax.ShapeDtypeStruct((t, n), F32),
        scratch_shapes=[pltpu.VMEM((tm, d), BF16)],
        compiler_params=pltpu.CompilerParams(
            dimension_semantics=("parallel", "arbitrary"), vmem_limit_bytes=VMEM_LIMIT),
        name="proj_in",
    )(x, g, wt)


def _mixout_kernel(x_ref, ya_ref, yr_ref, yp_ref, wa_ref, wr_ref, wp_ref, g_ref, o_ref):
    mix = _dot(ya_ref[...], wa_ref[...]) + _dot(yr_ref[...], wr_ref[...]) + _dot(yp_ref[...], wp_ref[...])
    o_ref[...] = x_ref[...] + _rms(mix, g_ref[...])


def _mixout(x, ya, yr, yp, wo, g, *, layer, tm):
    t, d = x.shape
    da, dr, dp = ya.shape[1], yr.shape[1], yp.shape[1]
    assert da == dr and (da + dr) % dp == 0
    return pl.pallas_call(
        _mixout_kernel,
        grid=(t // tm,),
        in_specs=[
            pl.BlockSpec((tm, d), lambda i: (i, 0)),
            pl.BlockSpec((tm, da), lambda i: (i, 0)),
            pl.BlockSpec((tm, dr), lambda i: (i, 0)),
            pl.BlockSpec((tm, dp), lambda i: (i, 0)),
            pl.BlockSpec((None, da, d), lambda i: (layer, 0, 0)),
            pl.BlockSpec((None, dr, d), lambda i: (layer, 1, 0)),
            pl.BlockSpec((None, dp, d), lambda i: (layer, (da + dr) // dp, 0)),
            pl.BlockSpec((None, 1, d), lambda i: (layer, 0, 0)),
        ],
        out_specs=pl.BlockSpec((tm, d), lambda i: (i, 0)),
        out_shape=jax.ShapeDtypeStruct((t, d), F32),
        compiler_params=pltpu.CompilerParams(
            dimension_semantics=("parallel",), vmem_limit_bytes=VMEM_LIMIT),
        name="proj_out",
    )(x, ya, yr, yp, wo, wo, wo, g)


def _cast_kernel(w_ref, o_ref):
    o_ref[...] = w_ref[...].astype(BF16)


def _cast_bf16(w, *, layer, tr):
    _, r, c = w.shape
    return pl.pallas_call(
        _cast_kernel,
        grid=(r // tr,),
        in_specs=[pl.BlockSpec((1, tr, c), lambda j: (layer, j, 0))],
        out_specs=pl.BlockSpec((1, tr, c), lambda j: (0, j, 0)),
        out_shape=jax.ShapeDtypeStruct((1, r, c), BF16),
        compiler_params=pltpu.CompilerParams(
            dimension_semantics=("parallel",), vmem_limit_bytes=VMEM_LIMIT),
        name="cast_bf16",
    )(w)


def _att_bias_tile(base, tq, tk):
    row = lax.broadcasted_iota(jnp.int32, (tq, tk), 0)
    col = lax.broadcasted_iota(jnp.int32, (tq, tk), 1)
    delta = base + row - col
    cnt = jnp.zeros((tq, tk), jnp.int32)
    for window, dil in DILATED_BRANCHES:
        hit = (delta >= 0) & (delta <= window) & ((delta & (dil - 1)) == 0)
        cnt = cnt + hit.astype(jnp.int32)
    return jnp.where(cnt == 3, math.log2(3.0),
                     jnp.where(cnt == 2, 1.0, jnp.where(cnt == 1, 0.0, NEG_BIG))).astype(BF16)


def _att_kernel(q_ref, k_ref, v_ref, cos_ref, sin_ref, o_ref,
                qa_s, qb_s, k_s, va_s, vb_s, bias_s, *, seq, tq, tk, group):
    nq = seq // tq
    ratio = tk // tq

    @pl.when(pl.program_id(1) == 0)
    def _():
        for off in range(nq):
            bias_s[off] = _att_bias_tile(off * tq, tq, tk)

    lane = lax.broadcasted_iota(jnp.int32, (tq, LANES), 1)
    first_half = (lane & (HEAD_DIM - 1)) < (HEAD_DIM // 2)
    head_a = lane < HEAD_DIM

    def rope(x, cos, sin):
        swapped = jnp.where(first_half, pltpu.roll(x, LANES - HEAD_DIM // 2, 1), pltpu.roll(x, HEAD_DIM // 2, 1))
        return x * cos + swapped * sin

    staged_q, staged_kv = set(), set()

    def stage_q(i):
        if i in staged_q:
            return
        staged_q.add(i)
        rows = slice(i * tq, (i + 1) * tq)
        q = rope(q_ref[0, rows, :], cos_ref[rows, :], sin_ref[rows, :]) * (HEAD_DIM ** -0.5 * LOG2_E)
        qa_s[rows, :] = jnp.where(head_a, q, 0.0).astype(BF16)
        qb_s[rows, :] = jnp.where(head_a, 0.0, q).astype(BF16)

    def stage_kv(i):
        if i in staged_kv:
            return
        staged_kv.add(i)
        rows = slice(i * tq, (i + 1) * tq)
        k_s[rows, :] = rope(k_ref[0, rows, :], cos_ref[rows, :], sin_ref[rows, :]).astype(BF16)
        v = v_ref[0, rows, :]
        va_s[rows, :] = jnp.where(head_a, v, 1.0).astype(BF16)
        vb_s[rows, :] = jnp.where(head_a, 1.0, v).astype(BF16)

    def tile_max(s):
        m = s[:, 0:LANES]
        for t in range(1, s.shape[1] // LANES):
            m = jnp.maximum(m, s[:, t * LANES:(t + 1) * LANES])
        return m

    q_s = (qa_s, qb_s)
    v_s = (va_s, vb_s)
    m_run = {}
    acc = {}
    for kj in range(seq // tk):
        blocks = list(range(ratio * kj, nq))
        for g0 in range(0, len(blocks), group):
            chains = [(qi, hd) for qi in blocks[g0:g0 + group] for hd in range(2)]
            width = {qi: (tq if qi == ratio * kj else tk) for qi, _ in chains}
            for qi, _ in chains:
                stage_q(qi)
                for kb in range(ratio * kj, ratio * kj + width[qi] // tq):
                    stage_kv(kb)
            s = {}
            for qi, hd in chains:
                krows = slice(kj * tk, kj * tk + width[qi])
                s[qi, hd] = (_dot_nt(q_s[hd][qi * tq:(qi + 1) * tq, :], k_s[krows, :]).astype(BF16)
                             + bias_s[qi - ratio * kj, :, 0:width[qi]])
            p = {}
            for ch in chains:
                m_tile = jnp.max(tile_max(s[ch]).astype(F32), axis=-1, keepdims=True)
                if kj == 0:
                    m_new = m_tile
                else:
                    m_new = jnp.maximum(m_run[ch], m_tile)
                    acc[ch] = jnp.exp2(m_run[ch] - m_new) * acc[ch]
                m_run[ch] = m_new
                p[ch] = jnp.exp2(s[ch] - m_new.astype(BF16))
            for ch in chains:
                pv = _dot(p[ch], v_s[ch[1]][kj * tk:kj * tk + width[ch[0]], :])
                acc[ch] = pv if kj == 0 else acc[ch] + pv

    for qi in range(nq):
        o_a = acc[qi, 0] / pltpu.roll(acc[qi, 0], HEAD_DIM, 1)
        o_b = acc[qi, 1] / pltpu.roll(acc[qi, 1], HEAD_DIM, 1)
        o_ref[0, qi * tq:(qi + 1) * tq, :] = jnp.where(head_a, o_a, o_b).astype(o_ref.dtype)


def _attention(z3, cos_t, sin_t):
    b, s, _ = z3.shape
    tk = min(ATT_KV_TILE, s)
    tq = tk // 2
    npair = D_ATT // LANES
    kern = functools.partial(_att_kernel, seq=s, tq=tq, tk=tk, group=ATT_GROUP)
    return pl.pallas_call(
        kern,
        grid=(b, npair),
        in_specs=[
            pl.BlockSpec((1, s, LANES), lambda i, p: (i, 0, COL_ATT // LANES + p)),
            pl.BlockSpec((1, s, LANES), lambda i, p: (i, 0, (COL_ATT + D_ATT) // LANES + p)),
            pl.BlockSpec((1, s, LANES), lambda i, p: (i, 0, (COL_ATT + 2 * D_ATT) // LANES + p)),
            pl.BlockSpec((s, LANES), lambda i, p: (0, 0)),
            pl.BlockSpec((s, LANES), lambda i, p: (0, 0)),
        ],
        out_specs=pl.BlockSpec((1, s, LANES), lambda i, p: (i, 0, p)),
        out_shape=jax.ShapeDtypeStruct((b, s, D_ATT), BF16),
        scratch_shapes=[pltpu.VMEM((s, LANES), BF16)] * 5 + [pltpu.VMEM((s // tq, tq, tk), BF16)],
        compiler_params=pltpu.CompilerParams(
            dimension_semantics=("parallel", "arbitrary"), vmem_limit_bytes=VMEM_LIMIT),
        name="dilated_attention",
    )(z3, z3, z3, cos_t, sin_t)


POOL_PAD = 16


def _pool_kernel(u_ref, w_ref, sc_ref, o_ref, pad_s, *, seq):
    t1 = (lax.broadcasted_iota(jnp.int32, (seq, POOL_GROUP), 0) + 1).astype(F32)
    pad_s[0:POOL_PAD, :] = jnp.zeros((POOL_PAD, POOL_GROUP), F32)
    for gi, w in enumerate(POOL_WINDOWS):
        cols = slice(gi * POOL_GROUP, (gi + 1) * POOL_GROUP)
        u = u_ref[0, :, cols]
        s = u
        k = 1
        while k < w:
            pad_s[POOL_PAD:POOL_PAD + seq, :] = s
            s = s + pad_s[POOL_PAD - k:POOL_PAD - k + seq, :]
            k *= 2
        p = s / jnp.minimum(t1, float(w)) - u
        y = _dot(p.astype(BF16), w_ref[gi])
        o_ref[0, :, cols] = (y * sc_ref[:, cols]).astype(o_ref.dtype)


def _pool(z3, pool_w, pool_scale):
    b, s, _ = z3.shape
    kern = functools.partial(_pool_kernel, seq=s)
    return pl.pallas_call(
        kern,
        grid=(b,),
        in_specs=[
            pl.BlockSpec((1, s, D_POOL), lambda i: (i, 0, COL_POOL // D_POOL)),
            pl.BlockSpec(pool_w.shape, lambda i: (0, 0, 0)),
            pl.BlockSpec((1, D_POOL), lambda i: (0, 0)),
        ],
        out_specs=pl.BlockSpec((1, s, D_POOL), lambda i: (i, 0, 0)),
        out_shape=jax.ShapeDtypeStruct((b, s, D_POOL), BF16),
        scratch_shapes=[pltpu.VMEM((s + POOL_PAD, POOL_GROUP), F32)],
        compiler_params=pltpu.CompilerParams(
            dimension_semantics=("parallel",), vmem_limit_bytes=VMEM_LIMIT),
        name="pool_mixer",
    )(z3, pool_w, pool_scale)


def _split_dot(x, ones_b, terms):
    acc = None
    rem = x
    for _ in range(terms):
        piece = rem.astype(BF16)
        part = _dot(piece, ones_b)
        acc = part if acc is None else acc + part
        rem = rem - piece.astype(F32)
    return acc


N_RWKV_IN = 18


def _rwkv_kernel(*refs, chunk, rows, n_steps, cast_steps):
    (zr_ref, zrp_ref, zl_ref, zlp_ref, mur_ref, mul_ref, wup_ref, w0_ref, aup_ref, a0_ref,
     gup_ref, kk_ref, ka_ref, rk_ref, gnw_ref, gnb_ref, tri_ref, ones_ref) = refs[:N_RWKV_IN]
    n_cast = len(cast_steps)
    cast_in = refs[N_RWKV_IN:N_RWKV_IN + n_cast]
    o_ref = refs[N_RWKV_IN + n_cast]
    cast_out = refs[N_RWKV_IN + n_cast + 1:N_RWKV_IN + 2 * n_cast + 1]
    h_s, sh_s = refs[N_RWKV_IN + 2 * n_cast + 1:]
    c = pl.program_id(1)
    npair = D_RWKV // LANES
    halo = 8
    wr = 3 * D_RWKV

    step = pl.program_id(0) * pl.num_programs(1) + c
    for src, dst, used in zip(cast_in, cast_out, cast_steps):
        if used == n_steps:
            dst[...] = src[...].astype(BF16)
        else:
            @pl.when(step < used)
            def _(src=src, dst=dst):
                dst[...] = src[...].astype(BF16)

    @pl.when(c == 0)
    def _():
        h_s[...] = jnp.zeros_like(h_s)

    row = lax.broadcasted_iota(jnp.int32, (LANES, LANES), 0)
    col = lax.broadcasted_iota(jnp.int32, (LANES, LANES), 1)
    lane = lax.broadcasted_iota(jnp.int32, (chunk, LANES), 1)
    head_a = lane < HEAD_DIM
    strict_lower = row > col
    lower = row >= col
    eye = (row == col).astype(F32)
    tri = tri_ref[...]
    first = c == 0

    def seg_sum(x):
        return jnp.concatenate(
            [_split_dot(x[:, p * LANES:(p + 1) * LANES], ones_ref[...], SEG_SUM_TERMS) for p in range(npair)],
            axis=1)

    def stack2(x):
        return jnp.concatenate([jnp.where(head_a, x, 0.0), jnp.where(head_a, 0.0, x)], axis=0)

    tok = []
    for bi in range(rows):
        zr = zr_ref[bi]
        zl = zl_ref[bi]
        sh_s[bi, 0:halo, 0:wr] = jnp.where(first, 0.0, zrp_ref[bi])
        sh_s[bi, 0:halo, wr:wr + D_LORA_PACK] = jnp.where(first, 0.0, zlp_ref[bi])
        sh_s[bi, halo:halo + chunk, 0:wr] = zr
        sh_s[bi, halo:halo + chunk, wr:wr + D_LORA_PACK] = zl
        zr_prev = sh_s[bi, halo - 1:halo - 1 + chunk, 0:wr]
        zl_prev = sh_s[bi, halo - 1:halo - 1 + chunk, wr:wr + D_LORA_PACK]
        zr = zr + (zr_prev - zr) * mur_ref[...]
        zl = zl + (zl_prev - zl) * mul_ref[...]
        r = zr[:, 0:D_RWKV]
        k = zr[:, D_RWKV:2 * D_RWKV]
        v = zr[:, 2 * D_RWKV:3 * D_RWKV]
        zw = zl[:, 0:LORA_PAD]
        za = zl[:, LORA_PAD:2 * LORA_PAD]
        zg = zl[:, 2 * LORA_PAD:D_LORA_PACK]

        lw = w0_ref[...] + _dot(jnp.tanh(zw).astype(BF16), wup_ref[...])
        ld = -math.exp(-0.5) * jax.nn.sigmoid(lw)
        a = jax.nn.sigmoid(a0_ref[...] + _dot(za.astype(BF16), aup_ref[...]))
        g = _dot(jax.nn.sigmoid(zg).astype(BF16), gup_ref[...])
        kk = k * kk_ref[...]
        k2 = k * (1.0 + (a - 1.0) * ka_ref[...])
        kk = kk / jnp.maximum(jnp.sqrt(seg_sum(kk * kk)), 1e-12)
        bonus = seg_sum(r * k2 * rk_ref[...]) * v
        bb = kk * a

        cum = _split_dot_left(tri, ld, 3)
        cum_end = cum[chunk - 1:chunk, :]
        e_inv = jnp.exp(-cum)
        e_end = jnp.exp(cum_end - cum)
        tok.append(dict(
            r_t=r * jnp.exp(cum), kk_t=kk * jnp.exp(cum - ld), b_h=bb * e_inv, k_h=k2 * e_inv,
            b_a=bb * e_end, k_a=k2 * e_end, a_end=jnp.exp(cum_end), v=v, bonus=bonus, g=g))

    units = [(bi, p) for bi in range(rows) for p in range(npair)]

    def pair(name):
        return [tok[bi][name][:, p * LANES:(p + 1) * LANES] for bi, p in units]

    kks = [stack2(x).astype(BF16) for x in pair("kk_t")]
    bs = [stack2(x).astype(BF16) for x in pair("b_h")]
    ks = [stack2(x).astype(BF16) for x in pair("k_h")]
    rs = [stack2(x).astype(BF16) for x in pair("r_t")]
    vs = [stack2(x).astype(BF16) for x in pair("v")]
    bat = [stack2(x).T.astype(BF16) for x in pair("b_a")]
    kat = [stack2(x).T.astype(BF16) for x in pair("k_a")]
    a_col = [jnp.broadcast_to(x, (LANES, LANES)).T for x in pair("a_end")]
    ids = range(len(units))

    tt = [_dot_nt(jnp.concatenate([kks[i], rs[i]], axis=0), jnp.concatenate([bs[i], ks[i]], axis=0)) for i in ids]
    l_b = [jnp.where(strict_lower, tt[i][0:LANES, 0:LANES], 0.0) for i in ids]
    l_k = [jnp.where(strict_lower, tt[i][0:LANES, LANES:2 * LANES], 0.0).astype(BF16) for i in ids]
    m_b = [jnp.where(lower, tt[i][LANES:2 * LANES, 0:LANES], 0.0).astype(BF16) for i in ids]
    m_k = [jnp.where(lower, tt[i][LANES:2 * LANES, LANES:2 * LANES], 0.0).astype(BF16) for i in ids]

    sel = ((row ^ col) < 2) & ((row & 1) != 0) & ((col & 1) == 0)
    t_inv = [eye - jnp.where(sel, l_b[i], 0.0) for i in ids]
    s = 2
    while s < HEAD_DIM:
        sel = ((row ^ col) < 2 * s) & ((row & s) != 0) & ((col & s) == 0)
        t_b = [t_inv[i].astype(BF16) for i in ids]
        inner = [_dot(jnp.where(sel, l_b[i], 0.0).astype(BF16), t_b[i]).astype(BF16) for i in ids]
        t_inv = [t_inv[i] - _dot(t_b[i], inner[i]) for i in ids]
        s *= 2
    t_b = [t_inv[i].astype(BF16) for i in ids]

    mlv = [_dot(jnp.concatenate([m_k[i], l_k[i]], axis=0), vs[i]) for i in ids]
    y0 = [mlv[i][0:LANES, :] for i in ids]
    lkv = [mlv[i][LANES:2 * LANES, :].astype(BF16) for i in ids]
    w_m = [_dot(t_b[i], kks[i]).astype(BF16) for i in ids]

    h = [h_s[i] for i in ids]
    h_b = [h[i].astype(BF16) for i in ids]
    u_b = [(-_dot(jnp.concatenate([w_m[i], t_b[i]], axis=1),
                  jnp.concatenate([h_b[i], lkv[i]], axis=0))).astype(BF16) for i in ids]
    y = [_dot(jnp.concatenate([rs[i], m_b[i]], axis=1), jnp.concatenate([h_b[i], u_b[i]], axis=0)) + y0[i]
         for i in ids]
    for i in ids:
        h_s[i] = a_col[i] * h[i] + _dot(jnp.concatenate([bat[i], kat[i]], axis=1),
                                        jnp.concatenate([u_b[i], vs[i]], axis=0))

    for bi in range(rows):
        yb = jnp.concatenate(
            [y[bi * npair + p][0:chunk, :] + y[bi * npair + p][chunk:2 * chunk, :] for p in range(npair)], axis=1)
        mean = seg_sum(yb) * (1.0 / HEAD_DIM)
        d = yb - mean
        var = seg_sum(d * d) * (1.0 / HEAD_DIM)
        yn = d * lax.rsqrt(var + RWKV_GN_EPS) * gnw_ref[...] + gnb_ref[...]
        o_ref[bi] = ((yn + tok[bi]["bonus"]) * tok[bi]["g"]).astype(o_ref.dtype)


def _split_dot_left(tri_b, x, terms):
    acc = None
    rem = x
    for _ in range(terms):
        piece = rem.astype(BF16)
        part = _dot(tri_b, piece)
        acc = part if acc is None else acc + part
        rem = rem - piece.astype(F32)
    return acc


BF16_SUBLANES = 16


def _rwkv_ones(chunk):
    t = jnp.arange(chunk)
    tri = (t[:, None] >= t[None, :]).astype(BF16)
    r = jnp.arange(LANES)[:, None]
    c = jnp.arange(LANES)[None, :]
    head_ones = ((r < HEAD_DIM) == (c < HEAD_DIM)).astype(BF16)
    return tri, head_ones


def _rwkv(z3, prm, *, rows, cast=()):
    b, s, _ = z3.shape
    chunk = RWKV_CHUNK
    nc = s // chunk
    nb = b // rows
    wr = 3 * D_RWKV
    rows8 = chunk // 8
    npair = D_RWKV // LANES

    def vec(n):
        return pl.BlockSpec((1, n), lambda i, c: (0, 0))

    def full(a):
        return pl.BlockSpec(a.shape, lambda i, c: (0, 0))

    tri, head_ones = _rwkv_ones(chunk)

    n_steps = nb * nc
    cast_in_specs, cast_out_specs, cast_shapes, cast_steps = [], [], [], []
    for w, layer in cast:
        _, r, cc = w.shape
        assert r % BF16_SUBLANES == 0
        used = max(k for k in range(1, n_steps + 1) if (r // BF16_SUBLANES) % k == 0)
        tr = r // used
        row_blk = lambda i, c, last=used - 1: jnp.minimum(i * nc + c, last)
        cast_in_specs.append(pl.BlockSpec((1, tr, cc), lambda i, c, f=row_blk, l=layer: (l, f(i, c), 0)))
        cast_out_specs.append(pl.BlockSpec((1, tr, cc), lambda i, c, f=row_blk: (0, f(i, c), 0)))
        cast_shapes.append(jax.ShapeDtypeStruct((1, r, cc), BF16))
        cast_steps.append(used)
    kern = functools.partial(_rwkv_kernel, chunk=chunk, rows=rows, n_steps=n_steps, cast_steps=tuple(cast_steps))

    outs = pl.pallas_call(
        kern,
        grid=(nb, nc),
        in_specs=[
            pl.BlockSpec((rows, chunk, wr), lambda i, c: (i, c, COL_RKV // wr)),
            pl.BlockSpec((rows, 8, wr), lambda i, c: (i, jnp.maximum(c * rows8 - 1, 0), COL_RKV // wr)),
            pl.BlockSpec((rows, chunk, D_LORA_PACK), lambda i, c: (i, c, COL_LORA // D_LORA_PACK)),
            pl.BlockSpec((rows, 8, D_LORA_PACK),
                         lambda i, c: (i, jnp.maximum(c * rows8 - 1, 0), COL_LORA // D_LORA_PACK)),
            vec(wr), vec(D_LORA_PACK),
            full(prm["w_up"]), vec(D_RWKV), full(prm["a_up"]), vec(D_RWKV), full(prm["g_up"]),
            vec(D_RWKV), vec(D_RWKV), vec(D_RWKV), vec(D_RWKV), vec(D_RWKV),
            full(tri), full(head_ones),
        ] + cast_in_specs,
        out_specs=[pl.BlockSpec((rows, chunk, D_RWKV), lambda i, c: (i, c, 0))] + cast_out_specs,
        out_shape=[jax.ShapeDtypeStruct((b, s, D_RWKV), BF16)] + cast_shapes,
        scratch_shapes=[
            pltpu.VMEM((rows * npair, LANES, LANES), F32),
            pltpu.VMEM((rows, chunk + 8, wr + D_LORA_PACK), F32),
        ],
        compiler_params=pltpu.CompilerParams(
            dimension_semantics=("arbitrary", "arbitrary"), vmem_limit_bytes=VMEM_LIMIT),
        name="rwkv7",
    )(z3, z3, z3, z3, prm["mu_r"], prm["mu_l"], prm["w_up"], prm["w0"], prm["a_up"], prm["a0"],
      prm["g_up"], prm["k_k"], prm["k_a"], prm["r_k"], prm["gn_w"], prm["gn_b"],
      tri, head_ones, *[w for w, _ in cast])
    return outs[0], list(outs[1:])


def _pad_cols(w, n):
    return jnp.pad(w, ((0, 0),) * (w.ndim - 1) + ((0, n - w.shape[-1]),))


def _pack_w_in_t(wt):
    o = 3 * D_ATT + 3 * D_RWKV
    o2 = o + RWKV_DECAY_LORA
    o3 = o2 + RWKV_AAA_LORA
    pad = ((0, 0), (0, LORA_PAD - RWKV_DECAY_LORA), (0, 0))
    return jnp.concatenate([wt[:, :o], jnp.pad(wt[:, o:o2], pad), jnp.pad(wt[:, o2:o3], pad), wt[:, o3:]], axis=1)


def _pack_mu(mu):
    o = 3 * D_RWKV
    mu_l = jnp.concatenate([
        jnp.pad(mu[o:o + RWKV_DECAY_LORA], (0, LORA_PAD - RWKV_DECAY_LORA)),
        jnp.pad(mu[o + RWKV_DECAY_LORA:o + RWKV_DECAY_LORA + RWKV_AAA_LORA], (0, LORA_PAD - RWKV_AAA_LORA)),
        mu[o + RWKV_DECAY_LORA + RWKV_AAA_LORA:],
    ])
    return mu[None, :o], mu_l[None, :]


def _pad_rows(w, n):
    return jnp.pad(w, ((0, n - w.shape[0]), (0, 0)))


def _rope_tables(seq):
    inv_freq = ROPE_THETA ** (-jnp.arange(0, HEAD_DIM, 2, dtype=F32) / HEAD_DIM)
    ang = jnp.arange(seq, dtype=F32)[:, None] * inv_freq[None, :]
    cos, sin = jnp.cos(ang), jnp.sin(ang)
    reps = LANES // HEAD_DIM
    cos_t = jnp.tile(jnp.concatenate([cos, cos], axis=1), (1, reps))
    sin_t = jnp.tile(jnp.concatenate([-sin, sin], axis=1), (1, reps))
    return cos_t, sin_t


def _tile(n, pref):
    t = min(n, pref)
    assert n % t == 0
    return t


def _row_tile(r, pref):
    return max(k for k in range(BF16_SUBLANES, pref + 1, BF16_SUBLANES) if r % k == 0)


def kernel(x, ffn1_norm_pre, ffn1_norm_post, ffn1_w_gate, ffn1_w_up, ffn1_w_down, mix_norm_pre, mix_norm_post, w_in, w_out, rwkv_mu, rwkv_w_up, rwkv_w0, rwkv_a_up, rwkv_a0, rwkv_g_up, rwkv_k_k, rwkv_k_a, rwkv_r_k, rwkv_gn_w, rwkv_gn_b, pool_w, pool_scale, ffn2_norm_pre, ffn2_norm_post, ffn2_w_gate, ffn2_w_up, ffn2_w_down):
    b, s, d = x.shape
    depth = w_in.shape[0]
    t = b * s
    d_ff = ffn1_w_gate.shape[2]
    assert s % RWKV_CHUNK == 0 and b % RWKV_ROWS == 0 and d == D_ATT + D_RWKV + D_POOL
    tm = _tile(t, 512)
    tm_ffn = _tile(t, 1024)
    tm_proj = _tile(t, 512)
    tf = _tile(d_ff, 512)
    tn = _tile(D_PACK, D_PACK // 2)
    cos_t, sin_t = _rope_tables(s)
    x2 = x.reshape(t, d).astype(F32)

    w_in_t = jnp.swapaxes(w_in, 1, 2)
    big = (ffn1_w_gate, ffn1_w_up, ffn1_w_down, ffn2_w_gate, ffn2_w_up, ffn2_w_down, w_in_t, w_out)
    wb = [_cast_bf16(w, layer=0, tr=_row_tile(w.shape[1], 256)) for w in big]
    ffn_g = [[g[:, None, :] for g in gs] for gs in
             ((ffn1_norm_pre, ffn1_norm_post), (ffn2_norm_pre, ffn2_norm_post))]
    mix_g_pre, mix_g_post = mix_norm_pre[:, None, :], mix_norm_post[:, None, :]

    def gains(gs, i):
        return [g[i:i + 1] for g in gs]

    for i in range(depth):
        x2 = _ffn(x2, *gains(ffn_g[0], i), *wb[0:3], layer=0, tm=tm_ffn, tf=tf)

        z = _proj(x2, mix_g_pre[i:i + 1], _pack_w_in_t(wb[6]), layer=0, tm=tm_proj, tn=tn)
        z3 = z.reshape(b, s, D_PACK)
        y_att = _attention(z3, cos_t, sin_t)
        mu_r, mu_l = _pack_mu(rwkv_mu[i])
        prm = dict(
            mu_r=mu_r, mu_l=mu_l,
            w_up=_pad_rows(rwkv_w_up[i], LORA_PAD).astype(BF16), w0=rwkv_w0[i][None],
            a_up=_pad_rows(rwkv_a_up[i], LORA_PAD).astype(BF16), a0=rwkv_a0[i][None],
            g_up=rwkv_g_up[i].astype(BF16), k_k=rwkv_k_k[i][None], k_a=rwkv_k_a[i][None],
            r_k=rwkv_r_k[i].reshape(1, D_RWKV), gn_w=rwkv_gn_w[i][None], gn_b=rwkv_gn_b[i][None])
        nxt = [(w, i + 1) for w in big] if i + 1 < depth else []
        y_rwkv, wb_next = _rwkv(z3, prm, rows=RWKV_ROWS, cast=nxt)
        y_pool = _pool(z3, pool_w[i].astype(BF16), pool_scale[i][None])

        x2 = _mixout(x2, y_att.reshape(t, D_ATT), y_rwkv.reshape(t, D_RWKV), y_pool.reshape(t, D_POOL),
                     wb[7], mix_g_post[i:i + 1], layer=0, tm=tm)

        x2 = _ffn(x2, *gains(ffn_g[1], i), *wb[3:6], layer=0, tm=tm_ffn, tf=tf)
        wb = wb_next
    return x2.reshape(b, s, d).astype(x.dtype)
```

```python
import functools
import math

import jax
import jax.numpy as jnp
from jax import lax
from jax.experimental import pallas as pl
from jax.experimental.pallas import tpu as pltpu

F32 = jnp.float32
BF16 = jnp.bfloat16

HEAD_DIM = 64
N_ATT_HEADS = 12
N_RWKV_HEADS = 12
D_ATT = N_ATT_HEADS * HEAD_DIM
D_RWKV = N_RWKV_HEADS * HEAD_DIM
D_POOL = 512
N_POOL_GROUPS = 4
POOL_GROUP = D_POOL // N_POOL_GROUPS
POOL_WINDOWS = (2, 4, 8, 16)
DILATED_BRANCHES = ((128, 1), (512, 4), (2048, 16))
ROPE_THETA = 10000.0
RWKV_DECAY_LORA = 96
RWKV_AAA_LORA = 96
RWKV_GATE_LORA = 256
NORM_EPS = 1e-6
RWKV_GN_EPS = 64e-5

LANES = 128
LORA_PAD = 128
COL_ATT = 0
COL_RKV = 3 * D_ATT
COL_LORA = COL_RKV + 3 * D_RWKV
D_LORA_PACK = 2 * LORA_PAD + RWKV_GATE_LORA
COL_POOL = COL_LORA + D_LORA_PACK
D_PACK = COL_POOL + D_POOL

SEG_SUM_TERMS = 1
RWKV_ROWS = 4
RWKV_CHUNK = 64
ATT_KV_TILE = 512
ATT_GROUP = 2
VMEM_LIMIT = 56 * 1024 * 1024
VMEM_LIMIT_FFN = 62 * 1024 * 1024

NEG_BIG = -1e30
LOG2_E = math.log2(math.e)


def _rms(xf, g):
    return xf * lax.rsqrt(jnp.mean(xf * xf, axis=-1, keepdims=True) + NORM_EPS) * g


def _dot(a, b):
    return jnp.dot(a, b, preferred_element_type=F32)


def _dot_nt(a, b):
    return lax.dot_general(a, b, (((1,), (1,)), ((), ())), preferred_element_type=F32)


def _ffn_kernel(x_ref, gpre_ref, gpost_ref, wg_ref, wu_ref, wd_ref, o_ref, h_ref):
    j = pl.program_id(1)

    @pl.when(j == 0)
    def _():
        h_ref[...] = _rms(x_ref[...], gpre_ref[...]).astype(BF16)
        o_ref[...] = jnp.zeros_like(o_ref)

    half = o_ref.shape[0] // 2
    for r0 in (0, half):
        rows = slice(r0, r0 + half)
        h = h_ref[rows, :]
        g = _dot(h, wg_ref[...])
        u = _dot(h, wu_ref[...])
        a = (g * jax.nn.sigmoid(g) * u).astype(BF16)
        o_ref[rows, :] += _dot(a, wd_ref[...])

    @pl.when(j == pl.num_programs(1) - 1)
    def _():
        o_ref[...] = x_ref[...] + 0.5 * _rms(o_ref[...], gpost_ref[...])


def _ffn(x, g_pre, g_post, wg, wu, wd, *, layer, tm, tf):
    t, d = x.shape
    f = wg.shape[2]
    return pl.pallas_call(
        _ffn_kernel,
        grid=(t // tm, f // tf),
        in_specs=[
            pl.BlockSpec((tm, d), lambda i, j: (i, 0)),
            pl.BlockSpec((None, 1, d), lambda i, j: (layer, 0, 0)),
            pl.BlockSpec((None, 1, d), lambda i, j: (layer, 0, 0)),
            pl.BlockSpec((None, d, tf), lambda i, j: (layer, 0, j)),
            pl.BlockSpec((None, d, tf), lambda i, j: (layer, 0, j)),
            pl.BlockSpec((None, tf, d), lambda i, j: (layer, j, 0)),
        ],
        out_specs=pl.BlockSpec((tm, d), lambda i, j: (i, 0)),
        out_shape=jax.ShapeDtypeStruct((t, d), F32),
        scratch_shapes=[pltpu.VMEM((tm, d), BF16)],
        compiler_params=pltpu.CompilerParams(
            dimension_semantics=("parallel", "arbitrary"), vmem_limit_bytes=VMEM_LIMIT_FFN),
        name="ffn",
    )(x, g_pre, g_post, wg, wu, wd)


def _proj_kernel(x_ref, g_ref, wt_ref, o_ref, h_ref):
    @pl.when(pl.program_id(1) == 0)
    def _():
        h_ref[...] = _rms(x_ref[...], g_ref[...]).astype(BF16)

    o_ref[...] = _dot_nt(h_ref[...], wt_ref[...])


def _proj(x, g, wt, *, layer, tm, tn):
    t, d = x.shape
    n = wt.shape[1]
    return pl.pallas_call(
        _proj_kernel,
        grid=(t // tm, n // tn),
        in_specs=[
            pl.BlockSpec((tm, d), lambda i, j: (i, 0)),
            pl.BlockSpec((None, 1, d), lambda i, j: (layer, 0, 0)),
            pl.BlockSpec((None, tn, d), lambda i, j: (layer, j, 0)),
        ],
        out_specs=pl.BlockSpec((tm, tn), lambda i, j: (i, j)),
        out_shape=jax.ShapeDtypeStruct((t, n), F32),
        scratch_shapes=[pltpu.VMEM((tm, d), BF16)],
        compiler_params=pltpu.CompilerParams(
            dimension_semantics=("parallel", "arbitrary"), vmem_limit_bytes=VMEM_LIMIT),
        name="proj_in",
    )(x, g, wt)


def _mixout_kernel(x_ref, ya_ref, yr_ref, yp_ref, wa_ref, wr_ref, wp_ref, g_ref, o_ref):
    mix = _dot(ya_ref[...], wa_ref[...]) + _dot(yr_ref[...], wr_ref[...]) + _dot(yp_ref[...], wp_ref[...])
    o_ref[...] = x_ref[...] + _rms(mix, g_ref[...])


def _mixout(x, ya, yr, yp, wo, g, *, layer, tm):
    t, d = x.shape
    da, dr, dp = ya.shape[1], yr.shape[1], yp.shape[1]
    assert da == dr and (da + dr) % dp == 0
    return pl.pallas_call(
        _mixout_kernel,
        grid=(t // tm,),
        in_specs=[
            pl.BlockSpec((tm, d), lambda i: (i, 0)),
            pl.BlockSpec((tm, da), lambda i: (i, 0)),
            pl.BlockSpec((tm, dr), lambda i: (i, 0)),
            pl.BlockSpec((tm, dp), lambda i: (i, 0)),
            pl.BlockSpec((None, da, d), lambda i: (layer, 0, 0)),
            pl.BlockSpec((None, dr, d), lambda i: (layer, 1, 0)),
            pl.BlockSpec((None, dp, d), lambda i: (layer, (da + dr) // dp, 0)),
            pl.BlockSpec((None, 1, d), lambda i: (layer, 0, 0)),
        ],
        out_specs=pl.BlockSpec((tm, d), lambda i: (i, 0)),
        out_shape=jax.ShapeDtypeStruct((t, d), F32),
        compiler_params=pltpu.CompilerParams(
            dimension_semantics=("parallel",), vmem_limit_bytes=VMEM_LIMIT),
        name="proj_out",
    )(x, ya, yr, yp, wo, wo, wo, g)


def _cast_kernel(w_ref, o_ref):
    o_ref[...] = w_ref[...].astype(BF16)


def _cast_bf16(w, *, layer, tr):
    _, r, c = w.shape
    return pl.pallas_call(
        _cast_kernel,
        grid=(r // tr,),
        in_specs=[pl.BlockSpec((1, tr, c), lambda j: (layer, j, 0))],
        out_specs=pl.BlockSpec((1, tr, c), lambda j: (0, j, 0)),
        out_shape=jax.ShapeDtypeStruct((1, r, c), BF16),
        compiler_params=pltpu.CompilerParams(
            dimension_semantics=("parallel",), vmem_limit_bytes=VMEM_LIMIT),
        name="cast_bf16",
    )(w)


def _att_bias_tile(base, tq, tk):
    row = lax.broadcasted_iota(jnp.int32, (tq, tk), 0)
    col = lax.broadcasted_iota(jnp.int32, (tq, tk), 1)
    delta = base + row - col
    cnt = jnp.zeros((tq, tk), jnp.int32)
    for window, dil in DILATED_BRANCHES:
        hit = (delta >= 0) & (delta <= window) & ((delta & (dil - 1)) == 0)
        cnt = cnt + hit.astype(jnp.int32)
    return jnp.where(cnt == 3, math.log2(3.0),
                     jnp.where(cnt == 2, 1.0, jnp.where(cnt == 1, 0.0, NEG_BIG))).astype(BF16)


def _att_kernel(q_ref, k_ref, v_ref, cos_ref, sin_ref, o_ref,
                qa_s, qb_s, k_s, va_s, vb_s, bias_s, *, seq, tq, tk, group):
    nq = seq // tq
    ratio = tk // tq

    @pl.when(pl.program_id(1) == 0)
    def _():
        for off in range(nq):
            bias_s[off] = _att_bias_tile(off * tq, tq, tk)

    lane = lax.broadcasted_iota(jnp.int32, (tq, LANES), 1)
    first_half = (lane & (HEAD_DIM - 1)) < (HEAD_DIM // 2)
    head_a = lane < HEAD_DIM

    def rope(x, cos, sin):
        swapped = jnp.where(first_half, pltpu.roll(x, LANES - HEAD_DIM // 2, 1), pltpu.roll(x, HEAD_DIM // 2, 1))
        return x * cos + swapped * sin

    staged_q, staged_kv = set(), set()

    def stage_q(i):
        if i in staged_q:
            return
        staged_q.add(i)
        rows = slice(i * tq, (i + 1) * tq)
        q = rope(q_ref[0, rows, :], cos_ref[rows, :], sin_ref[rows, :]) * (HEAD_DIM ** -0.5 * LOG2_E)
        qa_s[rows, :] = jnp.where(head_a, q, 0.0).astype(BF16)
        qb_s[rows, :] = jnp.where(head_a, 0.0, q).astype(BF16)

    def stage_kv(i):
        if i in staged_kv:
            return
        staged_kv.add(i)
        rows = slice(i * tq, (i + 1) * tq)
        k_s[rows, :] = rope(k_ref[0, rows, :], cos_ref[rows, :], sin_ref[rows, :]).astype(BF16)
        v = v_ref[0, rows, :]
        va_s[rows, :] = jnp.where(head_a, v, 1.0).astype(BF16)
        vb_s[rows, :] = jnp.where(head_a, 1.0, v).astype(BF16)

    def tile_max(s):
        m = s[:, 0:LANES]
        for t in range(1, s.shape[1] // LANES):
            m = jnp.maximum(m, s[:, t * LANES:(t + 1) * LANES])
        return m

    q_s = (qa_s, qb_s)
    v_s = (va_s, vb_s)
    m_run = {}
    acc = {}
    for kj in range(seq // tk):
        blocks = list(range(ratio * kj, nq))
        for g0 in range(0, len(blocks), group):
            chains = [(qi, hd) for qi in blocks[g0:g0 + group] for hd in range(2)]
            width = {qi: (tq if qi == ratio * kj else tk) for qi, _ in chains}
            for qi, _ in chains:
                stage_q(qi)
                for kb in range(ratio * kj, ratio * kj + width[qi] // tq):
                    stage_kv(kb)
            s = {}
            for qi, hd in chains:
                krows = slice(kj * tk, kj * tk + width[qi])
                s[qi, hd] = (_dot_nt(q_s[hd][qi * tq:(qi + 1) * tq, :], k_s[krows, :]).astype(BF16)
                             + bias_s[qi - ratio * kj, :, 0:width[qi]])
            p = {}
            for ch in chains:
                m_tile = jnp.max(tile_max(s[ch]).astype(F32), axis=-1, keepdims=True)
                if kj == 0:
                    m_new = m_tile
                else:
                    m_new = jnp.maximum(m_run[ch], m_tile)
                    acc[ch] = jnp.exp2(m_run[ch] - m_new) * acc[ch]
                m_run[ch] = m_new
                p[ch] = jnp.exp2(s[ch] - m_new.astype(BF16))
            for ch in chains:
                pv = _dot(p[ch], v_s[ch[1]][kj * tk:kj * tk + width[ch[0]], :])
                acc[ch] = pv if kj == 0 else acc[ch] + pv

    for qi in range(nq):
        o_a = acc[qi, 0] / pltpu.roll(acc[qi, 0], HEAD_DIM, 1)
        o_b = acc[qi, 1] / pltpu.roll(acc[qi, 1], HEAD_DIM, 1)
        o_ref[0, qi * tq:(qi + 1) * tq, :] = jnp.where(head_a, o_a, o_b).astype(o_ref.dtype)


def _attention(z3, cos_t, sin_t):
    b, s, _ = z3.shape
    tk = min(ATT_KV_TILE, s)
    tq = tk // 2
    npair = D_ATT // LANES
    kern = functools.partial(_att_kernel, seq=s, tq=tq, tk=tk, group=ATT_GROUP)
    return pl.pallas_call(
        kern,
        grid=(b, npair),
        in_specs=[
            pl.BlockSpec((1, s, LANES), lambda i, p: (i, 0, COL_ATT // LANES + p)),
            pl.BlockSpec((1, s, LANES), lambda i, p: (i, 0, (COL_ATT + D_ATT) // LANES + p)),
            pl.BlockSpec((1, s, LANES), lambda i, p: (i, 0, (COL_ATT + 2 * D_ATT) // LANES + p)),
            pl.BlockSpec((s, LANES), lambda i, p: (0, 0)),
            pl.BlockSpec((s, LANES), lambda i, p: (0, 0)),
        ],
        out_specs=pl.BlockSpec((1, s, LANES), lambda i, p: (i, 0, p)),
        out_shape=jax.ShapeDtypeStruct((b, s, D_ATT), BF16),
        scratch_shapes=[pltpu.VMEM((s, LANES), BF16)] * 5 + [pltpu.VMEM((s // tq, tq, tk), BF16)],
        compiler_params=pltpu.CompilerParams(
            dimension_semantics=("parallel", "arbitrary"), vmem_limit_bytes=VMEM_LIMIT),
        name="dilated_attention",
    )(z3, z3, z3, cos_t, sin_t)


POOL_PAD = 16


def _pool_kernel(u_ref, w_ref, sc_ref, o_ref, pad_s, *, seq):
    t1 = (lax.broadcasted_iota(jnp.int32, (seq, POOL_GROUP), 0) + 1).astype(F32)
    pad_s[0:POOL_PAD, :] = jnp.zeros((POOL_PAD, POOL_GROUP), F32)
    for gi, w in enumerate(POOL_WINDOWS):
        cols = slice(gi * POOL_GROUP, (gi + 1) * POOL_GROUP)
        u = u_ref[0, :, cols]
        s = u
        k = 1
        while k < w:
            pad_s[POOL_PAD:POOL_PAD + seq, :] = s
            s = s + pad_s[POOL_PAD - k:POOL_PAD - k + seq, :]
            k *= 2
        p = s / jnp.minimum(t1, float(w)) - u
        y = _dot(p.astype(BF16), w_ref[gi])
        o_ref[0, :, cols] = (y * sc_ref[:, cols]).astype(o_ref.dtype)


def _pool(z3, pool_w, pool_scale):
    b, s, _ = z3.shape
    kern = functools.partial(_pool_kernel, seq=s)
    return pl.pallas_call(
        kern,
        grid=(b,),
        in_specs=[
            pl.BlockSpec((1, s, D_POOL), lambda i: (i, 0, COL_POOL // D_POOL)),
            pl.BlockSpec(pool_w.shape, lambda i: (0, 0, 0)),
            pl.BlockSpec((1, D_POOL), lambda i: (0, 0)),
        ],
        out_specs=pl.BlockSpec((1, s, D_POOL), lambda i: (i, 0, 0)),
        out_shape=jax.ShapeDtypeStruct((b, s, D_POOL), BF16),
        scratch_shapes=[pltpu.VMEM((s + POOL_PAD, POOL_GROUP), F32)],
        compiler_params=pltpu.CompilerParams(
            dimension_semantics=("parallel",), vmem_limit_bytes=VMEM_LIMIT),
        name="pool_mixer",
    )(z3, pool_w, pool_scale)


def _split_dot(x, ones_b, terms):
    acc = None
    rem = x
    for _ in range(terms):
        piece = rem.astype(BF16)
        part = _dot(piece, ones_b)
        acc = part if acc is None else acc + part
        rem = rem - piece.astype(F32)
    return acc


N_RWKV_IN = 18


def _rwkv_kernel(*refs, chunk, rows, n_steps, cast_steps):
    (zr_ref, zrp_ref, zl_ref, zlp_ref, mur_ref, mul_ref, wup_ref, w0_ref, aup_ref, a0_ref,
     gup_ref, kk_ref, ka_ref, rk_ref, gnw_ref, gnb_ref, tri_ref, ones_ref) = refs[:N_RWKV_IN]
    n_cast = len(cast_steps)
    cast_in = refs[N_RWKV_IN:N_RWKV_IN + n_cast]
    o_ref = refs[N_RWKV_IN + n_cast]
    cast_out = refs[N_RWKV_IN + n_cast + 1:N_RWKV_IN + 2 * n_cast + 1]
    h_s, sh_s = refs[N_RWKV_IN + 2 * n_cast + 1:]
    c = pl.program_id(1)
    npair = D_RWKV // LANES
    halo = 8
    wr = 3 * D_RWKV

    step = pl.program_id(0) * pl.num_programs(1) + c
    for src, dst, used in zip(cast_in, cast_out, cast_steps):
        if used == n_steps:
            dst[...] = src[...].astype(BF16)
        else:
            @pl.when(step < used)
            def _(src=src, dst=dst):
                dst[...] = src[...].astype(BF16)

    @pl.when(c == 0)
    def _():
        h_s[...] = jnp.zeros_like(h_s)

    row = lax.broadcasted_iota(jnp.int32, (LANES, LANES), 0)
    col = lax.broadcasted_iota(jnp.int32, (LANES, LANES), 1)
    lane = lax.broadcasted_iota(jnp.int32, (chunk, LANES), 1)
    head_a = lane < HEAD_DIM
    strict_lower = row > col
    lower = row >= col
    eye = (row == col).astype(F32)
    tri = tri_ref[...]
    first = c == 0

    def seg_sum(x):
        m = x.shape[0]
        tall = jnp.concatenate([x[:, p * LANES:(p + 1) * LANES] for p in range(npair)], axis=0)
        summed = _split_dot(tall, ones_ref[...], SEG_SUM_TERMS)
        return jnp.concatenate([summed[p * m:(p + 1) * m] for p in range(npair)], axis=1)

    def stack2(x):
        return jnp.concatenate([jnp.where(head_a, x, 0.0), jnp.where(head_a, 0.0, x)], axis=0)

    zr_rows, zl_rows = [], []
    for bi in range(rows):
        zr = zr_ref[bi]
        zl = zl_ref[bi]
        sh_s[bi, 0:halo, 0:wr] = jnp.where(first, 0.0, zrp_ref[bi])
        sh_s[bi, 0:halo, wr:wr + D_LORA_PACK] = jnp.where(first, 0.0, zlp_ref[bi])
        sh_s[bi, halo:halo + chunk, 0:wr] = zr
        sh_s[bi, halo:halo + chunk, wr:wr + D_LORA_PACK] = zl
        zr_prev = sh_s[bi, halo - 1:halo - 1 + chunk, 0:wr]
        zl_prev = sh_s[bi, halo - 1:halo - 1 + chunk, wr:wr + D_LORA_PACK]
        zr_rows.append(zr + (zr_prev - zr) * mur_ref[...])
        zl_rows.append(zl + (zl_prev - zl) * mul_ref[...])
    zr = jnp.concatenate(zr_rows, axis=0)
    zl = jnp.concatenate(zl_rows, axis=0)
    r = zr[:, 0:D_RWKV]
    k = zr[:, D_RWKV:2 * D_RWKV]
    v = zr[:, 2 * D_RWKV:3 * D_RWKV]
    zw = zl[:, 0:LORA_PAD]
    za = zl[:, LORA_PAD:2 * LORA_PAD]
    zg = zl[:, 2 * LORA_PAD:D_LORA_PACK]

    lw = w0_ref[...] + _dot(jnp.tanh(zw).astype(BF16), wup_ref[...])
    ld = -math.exp(-0.5) * jax.nn.sigmoid(lw)
    a = jax.nn.sigmoid(a0_ref[...] + _dot(za.astype(BF16), aup_ref[...]))
    g = _dot(jax.nn.sigmoid(zg).astype(BF16), gup_ref[...])
    kk = k * kk_ref[...]
    k2 = k * (1.0 + (a - 1.0) * ka_ref[...])
    kk = kk / jnp.maximum(jnp.sqrt(seg_sum(kk * kk)), 1e-12)
    bonus = seg_sum(r * k2 * rk_ref[...]) * v
    bb = kk * a

    cum_rows = [_split_dot_left(tri, ld[bi * chunk:(bi + 1) * chunk], 3) for bi in range(rows)]
    cum = jnp.concatenate(cum_rows, axis=0)
    cum_end = jnp.concatenate([jnp.broadcast_to(cr[chunk - 1:chunk, :], cr.shape) for cr in cum_rows], axis=0)
    e_inv = jnp.exp(-cum)
    e_end = jnp.exp(cum_end - cum)
    tok_all = dict(
        r_t=r * jnp.exp(cum), kk_t=kk * jnp.exp(cum - ld), b_h=bb * e_inv, k_h=k2 * e_inv,
        b_a=bb * e_end, k_a=k2 * e_end, v=v)
    tok = [{name: val[bi * chunk:(bi + 1) * chunk] for name, val in tok_all.items()} for bi in range(rows)]
    for bi in range(rows):
        tok[bi]["a_end"] = jnp.exp(cum_rows[bi][chunk - 1:chunk, :])

    units = [(bi, p) for bi in range(rows) for p in range(npair)]

    def pair(name):
        return [tok[bi][name][:, p * LANES:(p + 1) * LANES] for bi, p in units]

    kks = [stack2(x).astype(BF16) for x in pair("kk_t")]
    bs = [stack2(x).astype(BF16) for x in pair("b_h")]
    ks = [stack2(x).astype(BF16) for x in pair("k_h")]
    rs = [stack2(x).astype(BF16) for x in pair("r_t")]
    vs = [stack2(x).astype(BF16) for x in pair("v")]
    bat = [stack2(x).T.astype(BF16) for x in pair("b_a")]
    kat = [stack2(x).T.astype(BF16) for x in pair("k_a")]
    a_col = [jnp.broadcast_to(x, (LANES, LANES)).T for x in pair("a_end")]
    ids = range(len(units))

    tt = [_dot_nt(jnp.concatenate([kks[i], rs[i]], axis=0), jnp.concatenate([bs[i], ks[i]], axis=0)) for i in ids]
    l_b = [jnp.where(strict_lower, tt[i][0:LANES, 0:LANES], 0.0) for i in ids]
    l_k = [jnp.where(strict_lower, tt[i][0:LANES, LANES:2 * LANES], 0.0).astype(BF16) for i in ids]
    m_b = [jnp.where(lower, tt[i][LANES:2 * LANES, 0:LANES], 0.0).astype(BF16) for i in ids]
    m_k = [jnp.where(lower, tt[i][LANES:2 * LANES, LANES:2 * LANES], 0.0).astype(BF16) for i in ids]

    sel = ((row ^ col) < 2) & ((row & 1) != 0) & ((col & 1) == 0)
    t_inv = [eye - jnp.where(sel, l_b[i], 0.0) for i in ids]
    s = 2
    while s < HEAD_DIM:
        sel = ((row ^ col) < 2 * s) & ((row & s) != 0) & ((col & s) == 0)
        t_b = [t_inv[i].astype(BF16) for i in ids]
        inner = [_dot(jnp.where(sel, l_b[i], 0.0).astype(BF16), t_b[i]).astype(BF16) for i in ids]
        t_inv = [t_inv[i] - _dot(t_b[i], inner[i]) for i in ids]
        s *= 2
    t_b = [t_inv[i].astype(BF16) for i in ids]

    mlv = [_dot(jnp.concatenate([m_k[i], l_k[i]], axis=0), vs[i]) for i in ids]
    y0 = [mlv[i][0:LANES, :] for i in ids]
    lkv = [mlv[i][LANES:2 * LANES, :].astype(BF16) for i in ids]
    w_m = [_dot(t_b[i], kks[i]).astype(BF16) for i in ids]

    h = [h_s[i] for i in ids]
    h_b = [h[i].astype(BF16) for i in ids]
    u_b = [(-_dot(jnp.concatenate([w_m[i], t_b[i]], axis=1),
                  jnp.concatenate([h_b[i], lkv[i]], axis=0))).astype(BF16) for i in ids]
    y = [_dot(jnp.concatenate([rs[i], m_b[i]], axis=1), jnp.concatenate([h_b[i], u_b[i]], axis=0)) + y0[i]
         for i in ids]
    for i in ids:
        h_s[i] = a_col[i] * h[i] + _dot(jnp.concatenate([bat[i], kat[i]], axis=1),
                                        jnp.concatenate([u_b[i], vs[i]], axis=0))

    yb = jnp.concatenate(
        [jnp.concatenate([y[bi * npair + p][0:chunk, :] + y[bi * npair + p][chunk:2 * chunk, :]
                          for p in range(npair)], axis=1) for bi in range(rows)], axis=0)
    mean = seg_sum(yb) * (1.0 / HEAD_DIM)
    d = yb - mean
    var = seg_sum(d * d) * (1.0 / HEAD_DIM)
    yn = d * lax.rsqrt(var + RWKV_GN_EPS) * gnw_ref[...] + gnb_ref[...]
    out = ((yn + bonus) * g).astype(o_ref.dtype)
    for bi in range(rows):
        o_ref[bi] = out[bi * chunk:(bi + 1) * chunk]


def _split_dot_left(tri_b, x, terms):
    acc = None
    rem = x
    for _ in range(terms):
        piece = rem.astype(BF16)
        part = _dot(tri_b, piece)
        acc = part if acc is None else acc + part
        rem = rem - piece.astype(F32)
    return acc


BF16_SUBLANES = 16


def _rwkv_ones(chunk):
    t = jnp.arange(chunk)
    tri = (t[:, None] >= t[None, :]).astype(BF16)
    r = jnp.arange(LANES)[:, None]
    c = jnp.arange(LANES)[None, :]
    head_ones = ((r < HEAD_DIM) == (c < HEAD_DIM)).astype(BF16)
    return tri, head_ones


def _rwkv(z3, prm, *, rows, cast=()):
    b, s, _ = z3.shape
    chunk = RWKV_CHUNK
    nc = s // chunk
    nb = b // rows
    wr = 3 * D_RWKV
    rows8 = chunk // 8
    npair = D_RWKV // LANES

    def vec(n):
        return pl.BlockSpec((1, n), lambda i, c: (0, 0))

    def full(a):
        return pl.BlockSpec(a.shape, lambda i, c: (0, 0))

    tri, head_ones = _rwkv_ones(chunk)

    n_steps = nb * nc
    cast_in_specs, cast_out_specs, cast_shapes, cast_steps = [], [], [], []
    for w, layer in cast:
        _, r, cc = w.shape
        assert r % BF16_SUBLANES == 0
        used = max(k for k in range(1, n_steps + 1) if (r // BF16_SUBLANES) % k == 0)
        tr = r // used
        row_blk = lambda i, c, last=used - 1: jnp.minimum(i * nc + c, last)
        cast_in_specs.append(pl.BlockSpec((1, tr, cc), lambda i, c, f=row_blk, l=layer: (l, f(i, c), 0)))
        cast_out_specs.append(pl.BlockSpec((1, tr, cc), lambda i, c, f=row_blk: (0, f(i, c), 0)))
        cast_shapes.append(jax.ShapeDtypeStruct((1, r, cc), BF16))
        cast_steps.append(used)
    kern = functools.partial(_rwkv_kernel, chunk=chunk, rows=rows, n_steps=n_steps, cast_steps=tuple(cast_steps))

    outs = pl.pallas_call(
        kern,
        grid=(nb, nc),
        in_specs=[
            pl.BlockSpec((rows, chunk, wr), lambda i, c: (i, c, COL_RKV // wr)),
            pl.BlockSpec((rows, 8, wr), lambda i, c: (i, jnp.maximum(c * rows8 - 1, 0), COL_RKV // wr)),
            pl.BlockSpec((rows, chunk, D_LORA_PACK), lambda i, c: (i, c, COL_LORA // D_LORA_PACK)),
            pl.BlockSpec((rows, 8, D_LORA_PACK),
                         lambda i, c: (i, jnp.maximum(c * rows8 - 1, 0), COL_LORA // D_LORA_PACK)),
            vec(wr), vec(D_LORA_PACK),
            full(prm["w_up"]), vec(D_RWKV), full(prm["a_up"]), vec(D_RWKV), full(prm["g_up"]),
            vec(D_RWKV), vec(D_RWKV), vec(D_RWKV), vec(D_RWKV), vec(D_RWKV),
            full(tri), full(head_ones),
        ] + cast_in_specs,
        out_specs=[pl.BlockSpec((rows, chunk, D_RWKV), lambda i, c: (i, c, 0))] + cast_out_specs,
        out_shape=[jax.ShapeDtypeStruct((b, s, D_RWKV), BF16)] + cast_shapes,
        scratch_shapes=[
            pltpu.VMEM((rows * npair, LANES, LANES), F32),
            pltpu.VMEM((rows, chunk + 8, wr + D_LORA_PACK), F32),
        ],
        compiler_params=pltpu.CompilerParams(
            dimension_semantics=("arbitrary", "arbitrary"), vmem_limit_bytes=VMEM_LIMIT),
        name="rwkv7",
    )(z3, z3, z3, z3, prm["mu_r"], prm["mu_l"], prm["w_up"], prm["w0"], prm["a_up"], prm["a0"],
      prm["g_up"], prm["k_k"], prm["k_a"], prm["r_k"], prm["gn_w"], prm["gn_b"],
      tri, head_ones, *[w for w, _ in cast])
    return outs[0], list(outs[1:])


def _pad_cols(w, n):
    return jnp.pad(w, ((0, 0),) * (w.ndim - 1) + ((0, n - w.shape[-1]),))


def _pack_w_in_t(wt):
    o = 3 * D_ATT + 3 * D_RWKV
    o2 = o + RWKV_DECAY_LORA
    o3 = o2 + RWKV_AAA_LORA
    pad = ((0, 0), (0, LORA_PAD - RWKV_DECAY_LORA), (0, 0))
    return jnp.concatenate([wt[:, :o], jnp.pad(wt[:, o:o2], pad), jnp.pad(wt[:, o2:o3], pad), wt[:, o3:]], axis=1)


def _pack_mu(mu):
    o = 3 * D_RWKV
    mu_l = jnp.concatenate([
        jnp.pad(mu[o:o + RWKV_DECAY_LORA], (0, LORA_PAD - RWKV_DECAY_LORA)),
        jnp.pad(mu[o + RWKV_DECAY_LORA:o + RWKV_DECAY_LORA + RWKV_AAA_LORA], (0, LORA_PAD - RWKV_AAA_LORA)),
        mu[o + RWKV_DECAY_LORA + RWKV_AAA_LORA:],
    ])
    return mu[None, :o], mu_l[None, :]


def _pad_rows(w, n):
    return jnp.pad(w, ((0, n - w.shape[0]), (0, 0)))


def _rope_tables(seq):
    inv_freq = ROPE_THETA ** (-jnp.arange(0, HEAD_DIM, 2, dtype=F32) / HEAD_DIM)
    ang = jnp.arange(seq, dtype=F32)[:, None] * inv_freq[None, :]
    cos, sin = jnp.cos(ang), jnp.sin(ang)
    reps = LANES // HEAD_DIM
    cos_t = jnp.tile(jnp.concatenate([cos, cos], axis=1), (1, reps))
    sin_t = jnp.tile(jnp.concatenate([-sin, sin], axis=1), (1, reps))
    return cos_t, sin_t


def _tile(n, pref):
    t = min(n, pref)
    assert n % t == 0
    return t


def _row_tile(r, pref):
    return max(k for k in range(BF16_SUBLANES, pref + 1, BF16_SUBLANES) if r % k == 0)


def kernel(x, ffn1_norm_pre, ffn1_norm_post, ffn1_w_gate, ffn1_w_up, ffn1_w_down, mix_norm_pre, mix_norm_post, w_in, w_out, rwkv_mu, rwkv_w_up, rwkv_w0, rwkv_a_up, rwkv_a0, rwkv_g_up, rwkv_k_k, rwkv_k_a, rwkv_r_k, rwkv_gn_w, rwkv_gn_b, pool_w, pool_scale, ffn2_norm_pre, ffn2_norm_post, ffn2_w_gate, ffn2_w_up, ffn2_w_down):
    b, s, d = x.shape
    depth = w_in.shape[0]
    t = b * s
    d_ff = ffn1_w_gate.shape[2]
    assert s % RWKV_CHUNK == 0 and b % RWKV_ROWS == 0 and d == D_ATT + D_RWKV + D_POOL
    tm = _tile(t, 512)
    tm_ffn = _tile(t, 1024)
    tm_proj = _tile(t, 512)
    tf = _tile(d_ff, 512)
    tn = _tile(D_PACK, D_PACK // 2)
    cos_t, sin_t = _rope_tables(s)
    x2 = x.reshape(t, d).astype(F32)

    w_in_t = jnp.swapaxes(w_in, 1, 2)
    big = (ffn1_w_gate, ffn1_w_up, ffn1_w_down, ffn2_w_gate, ffn2_w_up, ffn2_w_down, w_in_t, w_out)
    wb = [_cast_bf16(w, layer=0, tr=_row_tile(w.shape[1], 256)) for w in big]
    ffn_g = [[g[:, None, :] for g in gs] for gs in
             ((ffn1_norm_pre, ffn1_norm_post), (ffn2_norm_pre, ffn2_norm_post))]
    mix_g_pre, mix_g_post = mix_norm_pre[:, None, :], mix_norm_post[:, None, :]

    def gains(gs, i):
        return [g[i:i + 1] for g in gs]

    for i in range(depth):
        x2 = _ffn(x2, *gains(ffn_g[0], i), *wb[0:3], layer=0, tm=tm_ffn, tf=tf)

        z = _proj(x2, mix_g_pre[i:i + 1], _pack_w_in_t(wb[6]), layer=0, tm=tm_proj, tn=tn)
        z3 = z.reshape(b, s, D_PACK)
        y_att = _attention(z3, cos_t, sin_t)
        mu_r, mu_l = _pack_mu(rwkv_mu[i])
        prm = dict(
            mu_r=mu_r, mu_l=mu_l,
            w_up=_pad_rows(rwkv_w_up[i], LORA_PAD).astype(BF16), w0=rwkv_w0[i][None],
            a_up=_pad_rows(rwkv_a_up[i], LORA_PAD).astype(BF16), a0=rwkv_a0[i][None],
            g_up=rwkv_g_up[i].astype(BF16), k_k=rwkv_k_k[i][None], k_a=rwkv_k_a[i][None],
            r_k=rwkv_r_k[i].reshape(1, D_RWKV), gn_w=rwkv_gn_w[i][None], gn_b=rwkv_gn_b[i][None])
        nxt = [(w, i + 1) for w in big] if i + 1 < depth else []
        y_rwkv, wb_next = _rwkv(z3, prm, rows=RWKV_ROWS, cast=nxt)
        y_pool = _pool(z3, pool_w[i].astype(BF16), pool_scale[i][None])

        x2 = _mixout(x2, y_att.reshape(t, D_ATT), y_rwkv.reshape(t, D_RWKV), y_pool.reshape(t, D_POOL),
                     wb[7], mix_g_post[i:i + 1], layer=0, tm=tm)

        x2 = _ffn(x2, *gains(ffn_g[1], i), *wb[3:6], layer=0, tm=tm_ffn, tf=tf)
        wb = wb_next
    return x2.reshape(b, s, d).astype(x.dtype)
```

```python
import functools
import math

import jax
import jax.numpy as jnp
from jax import lax
from jax.experimental import pallas as pl
from jax.experimental.pallas import tpu as pltpu

F32 = jnp.float32
BF16 = jnp.bfloat16

HEAD_DIM = 64
N_ATT_HEADS = 12
N_RWKV_HEADS = 12
D_ATT = N_ATT_HEADS * HEAD_DIM
D_RWKV = N_RWKV_HEADS * HEAD_DIM
D_POOL = 512
N_POOL_GROUPS = 4
POOL_GROUP = D_POOL // N_POOL_GROUPS
POOL_WINDOWS = (2, 4, 8, 16)
DILATED_BRANCHES = ((128, 1), (512, 4), (2048, 16))
ROPE_THETA = 10000.0
RWKV_DECAY_LORA = 96
RWKV_AAA_LORA = 96
RWKV_GATE_LORA = 256
NORM_EPS = 1e-6
RWKV_GN_EPS = 64e-5

LANES = 128
LORA_PAD = 128
COL_ATT = 0
COL_RKV = 3 * D_ATT
COL_LORA = COL_RKV + 3 * D_RWKV
D_LORA_PACK = 2 * LORA_PAD + RWKV_GATE_LORA
COL_POOL = COL_LORA + D_LORA_PACK
D_PACK = COL_POOL + D_POOL

SEG_SUM_TERMS = 1
RWKV_ROWS = 4
RWKV_CHUNK = 64
ATT_KV_TILE = 512
ATT_GROUP = 2
VMEM_LIMIT = 56 * 1024 * 1024
VMEM_LIMIT_FFN = 62 * 1024 * 1024

NEG_BIG = -1e30
LOG2_E = math.log2(math.e)


def _rms(xf, g):
    return xf * lax.rsqrt(jnp.mean(xf * xf, axis=-1, keepdims=True) + NORM_EPS) * g


def _dot(a, b):
    return jnp.dot(a, b, preferred_element_type=F32)


def _dot_nt(a, b):
    return lax.dot_general(a, b, (((1,), (1,)), ((), ())), preferred_element_type=F32)


def _ffn_kernel(x_ref, gpre_ref, gpost_ref, wg_ref, wu_ref, wd_ref, o_ref, h_ref, ssq_ref):
    j = pl.program_id(1)

    @pl.when(j == 0)
    def _():
        h_ref[...] = _rms(x_ref[...], gpre_ref[...]).astype(BF16)
        o_ref[...] = jnp.zeros_like(o_ref)

    half = o_ref.shape[0] // 2
    for r0 in (0, half):
        rows = slice(r0, r0 + half)
        h = h_ref[rows, :]
        g = _dot(h, wg_ref[...])
        u = _dot(h, wu_ref[...])
        a = (g * jax.nn.sigmoid(g) * u).astype(BF16)
        acc = o_ref[rows, :] + _dot(a, wd_ref[...])
        o_ref[rows, :] = acc
        ssq_ref[rows, :] = jnp.sum(acc * acc, axis=-1, keepdims=True)

    @pl.when(j == pl.num_programs(1) - 1)
    def _():
        inv = lax.rsqrt(ssq_ref[...] * (1.0 / o_ref.shape[1]) + NORM_EPS) * 0.5
        o_ref[...] = x_ref[...] + o_ref[...] * inv * gpost_ref[...]


def _ffn(x, g_pre, g_post, wg, wu, wd, *, layer, tm, tf):
    t, d = x.shape
    f = wg.shape[2]
    return pl.pallas_call(
        _ffn_kernel,
        grid=(t // tm, f // tf),
        in_specs=[
            pl.BlockSpec((tm, d), lambda i, j: (i, 0)),
            pl.BlockSpec((None, 1, d), lambda i, j: (layer, 0, 0)),
            pl.BlockSpec((None, 1, d), lambda i, j: (layer, 0, 0)),
            pl.BlockSpec((None, d, tf), lambda i, j: (layer, 0, j)),
            pl.BlockSpec((None, d, tf), lambda i, j: (layer, 0, j)),
            pl.BlockSpec((None, tf, d), lambda i, j: (layer, j, 0)),
        ],
        out_specs=pl.BlockSpec((tm, d), lambda i, j: (i, 0)),
        out_shape=jax.ShapeDtypeStruct((t, d), F32),
        scratch_shapes=[pltpu.VMEM((tm, d), BF16), pltpu.VMEM((tm, 1), F32)],
        compiler_params=pltpu.CompilerParams(
            dimension_semantics=("parallel", "arbitrary"), vmem_limit_bytes=VMEM_LIMIT_FFN),
        name="ffn",
    )(x, g_pre, g_post, wg, wu, wd)


def _proj_kernel(x_ref, g_ref, wt_ref, o_ref, h_ref):
    @pl.when(pl.program_id(1) == 0)
    def _():
        h_ref[...] = _rms(x_ref[...], g_ref[...]).astype(BF16)

    o_ref[...] = _dot_nt(h_ref[...], wt_ref[...])


def _proj(x, g, wt, *, layer, tm, tn):
    t, d = x.shape
    n = wt.shape[1]
    return pl.pallas_call(
        _proj_kernel,
        grid=(t // tm, n // tn),
        in_specs=[
            pl.BlockSpec((tm, d), lambda i, j: (i, 0)),
            pl.BlockSpec((None, 1, d), lambda i, j: (layer, 0, 0)),
            pl.BlockSpec((None, tn, d), lambda i, j: (layer, j, 0)),
        ],
        out_specs=pl.BlockSpec((tm, tn), lambda i, j: (i, j)),
        out_shape=jax.ShapeDtypeStruct((t, n), F32),
        scratch_shapes=[pltpu.VMEM((tm, d), BF16)],
        compiler_params=pltpu.CompilerParams(
            dimension_semantics=("parallel", "arbitrary"), vmem_limit_bytes=VMEM_LIMIT),
        name="proj_in",
    )(x, g, wt)


def _mixout_kernel(x_ref, ya_ref, yr_ref, yp_ref, wa_ref, wr_ref, wp_ref, g_ref, o_ref):
    mix = _dot(ya_ref[...], wa_ref[...]) + _dot(yr_ref[...], wr_ref[...]) + _dot(yp_ref[...], wp_ref[...])
    o_ref[...] = x_ref[...] + _rms(mix, g_ref[...])


def _mixout(x, ya, yr, yp, wo, g, *, layer, tm):
    t, d = x.shape
    da, dr, dp = ya.shape[1], yr.shape[1], yp.shape[1]
    assert da == dr and (da + dr) % dp == 0
    return pl.pallas_call(
        _mixout_kernel,
        grid=(t // tm,),
        in_specs=[
            pl.BlockSpec((tm, d), lambda i: (i, 0)),
            pl.BlockSpec((tm, da), lambda i: (i, 0)),
            pl.BlockSpec((tm, dr), lambda i: (i, 0)),
            pl.BlockSpec((tm, dp), lambda i: (i, 0)),
            pl.BlockSpec((None, da, d), lambda i: (layer, 0, 0)),
            pl.BlockSpec((None, dr, d), lambda i: (layer, 1, 0)),
            pl.BlockSpec((None, dp, d), lambda i: (layer, (da + dr) // dp, 0)),
            pl.BlockSpec((None, 1, d), lambda i: (layer, 0, 0)),
        ],
        out_specs=pl.BlockSpec((tm, d), lambda i: (i, 0)),
        out_shape=jax.ShapeDtypeStruct((t, d), F32),
        compiler_params=pltpu.CompilerParams(
            dimension_semantics=("parallel",), vmem_limit_bytes=VMEM_LIMIT),
        name="proj_out",
    )(x, ya, yr, yp, wo, wo, wo, g)


def _cast_kernel(w_ref, o_ref):
    o_ref[...] = w_ref[...].astype(BF16)


def _cast_bf16(w, *, layer, tr):
    _, r, c = w.shape
    return pl.pallas_call(
        _cast_kernel,
        grid=(r // tr,),
        in_specs=[pl.BlockSpec((1, tr, c), lambda j: (layer, j, 0))],
        out_specs=pl.BlockSpec((1, tr, c), lambda j: (0, j, 0)),
        out_shape=jax.ShapeDtypeStruct((1, r, c), BF16),
        compiler_params=pltpu.CompilerParams(
            dimension_semantics=("parallel",), vmem_limit_bytes=VMEM_LIMIT),
        name="cast_bf16",
    )(w)


def _att_bias_tile(base, tq, tk):
    row = lax.broadcasted_iota(jnp.int32, (tq, tk), 0)
    col = lax.broadcasted_iota(jnp.int32, (tq, tk), 1)
    delta = base + row - col
    cnt = jnp.zeros((tq, tk), jnp.int32)
    for window, dil in DILATED_BRANCHES:
        hit = (delta >= 0) & (delta <= window) & ((delta & (dil - 1)) == 0)
        cnt = cnt + hit.astype(jnp.int32)
    return jnp.where(cnt == 3, math.log2(3.0),
                     jnp.where(cnt == 2, 1.0, jnp.where(cnt == 1, 0.0, NEG_BIG))).astype(BF16)


def _att_kernel(q_ref, k_ref, v_ref, cos_ref, sin_ref, o_ref,
                qa_s, qb_s, k_s, va_s, vb_s, bias_s, *, seq, tq, tk, group):
    nq = seq // tq
    ratio = tk // tq

    @pl.when(pl.program_id(1) == 0)
    def _():
        for off in range(nq):
            bias_s[off] = _att_bias_tile(off * tq, tq, tk)

    lane = lax.broadcasted_iota(jnp.int32, (tq, LANES), 1)
    first_half = (lane & (HEAD_DIM - 1)) < (HEAD_DIM // 2)
    head_a = lane < HEAD_DIM

    def rope(x, cos, sin):
        swapped = jnp.where(first_half, pltpu.roll(x, LANES - HEAD_DIM // 2, 1), pltpu.roll(x, HEAD_DIM // 2, 1))
        return x * cos + swapped * sin

    staged_q, staged_kv = set(), set()

    def stage_q(i):
        if i in staged_q:
            return
        staged_q.add(i)
        rows = slice(i * tq, (i + 1) * tq)
        q = rope(q_ref[0, rows, :], cos_ref[rows, :], sin_ref[rows, :]) * (HEAD_DIM ** -0.5 * LOG2_E)
        qa_s[rows, :] = jnp.where(head_a, q, 0.0).astype(BF16)
        qb_s[rows, :] = jnp.where(head_a, 0.0, q).astype(BF16)

    def stage_kv(i):
        if i in staged_kv:
            return
        staged_kv.add(i)
        rows = slice(i * tq, (i + 1) * tq)
        k_s[rows, :] = rope(k_ref[0, rows, :], cos_ref[rows, :], sin_ref[rows, :]).astype(BF16)
        v = v_ref[0, rows, :]
        va_s[rows, :] = jnp.where(head_a, v, 1.0).astype(BF16)
        vb_s[rows, :] = jnp.where(head_a, 1.0, v).astype(BF16)

    def tile_max(s):
        m = s[:, 0:LANES]
        for t in range(1, s.shape[1] // LANES):
            m = jnp.maximum(m, s[:, t * LANES:(t + 1) * LANES])
        return m

    q_s = (qa_s, qb_s)
    v_s = (va_s, vb_s)
    m_run = {}
    acc = {}
    for kj in range(seq // tk):
        blocks = list(range(ratio * kj, nq))
        for g0 in range(0, len(blocks), group):
            chains = [(qi, hd) for qi in blocks[g0:g0 + group] for hd in range(2)]
            width = {qi: (tq if qi == ratio * kj else tk) for qi, _ in chains}
            for qi, _ in chains:
                stage_q(qi)
                for kb in range(ratio * kj, ratio * kj + width[qi] // tq):
                    stage_kv(kb)
            s = {}
            for qi, hd in chains:
                krows = slice(kj * tk, kj * tk + width[qi])
                s[qi, hd] = (_dot_nt(q_s[hd][qi * tq:(qi + 1) * tq, :], k_s[krows, :]).astype(BF16)
                             + bias_s[qi - ratio * kj, :, 0:width[qi]])
            p = {}
            for ch in chains:
                m_tile = jnp.max(tile_max(s[ch]).astype(F32), axis=-1, keepdims=True)
                if kj == 0:
                    m_new = m_tile
                else:
                    m_new = jnp.maximum(m_run[ch], m_tile)
                    acc[ch] = jnp.exp2(m_run[ch] - m_new) * acc[ch]
                m_run[ch] = m_new
                p[ch] = jnp.exp2(s[ch] - m_new.astype(BF16))
            for ch in chains:
                pv = _dot(p[ch], v_s[ch[1]][kj * tk:kj * tk + width[ch[0]], :])
                acc[ch] = pv if kj == 0 else acc[ch] + pv

    for qi in range(nq):
        o_a = acc[qi, 0] / pltpu.roll(acc[qi, 0], HEAD_DIM, 1)
        o_b = acc[qi, 1] / pltpu.roll(acc[qi, 1], HEAD_DIM, 1)
        o_ref[0, qi * tq:(qi + 1) * tq, :] = jnp.where(head_a, o_a, o_b).astype(o_ref.dtype)


def _attention(z3, cos_t, sin_t):
    b, s, _ = z3.shape
    tk = min(ATT_KV_TILE, s)
    tq = tk // 2
    npair = D_ATT // LANES
    kern = functools.partial(_att_kernel, seq=s, tq=tq, tk=tk, group=ATT_GROUP)
    return pl.pallas_call(
        kern,
        grid=(b, npair),
        in_specs=[
            pl.BlockSpec((1, s, LANES), lambda i, p: (i, 0, COL_ATT // LANES + p)),
            pl.BlockSpec((1, s, LANES), lambda i, p: (i, 0, (COL_ATT + D_ATT) // LANES + p)),
            pl.BlockSpec((1, s, LANES), lambda i, p: (i, 0, (COL_ATT + 2 * D_ATT) // LANES + p)),
            pl.BlockSpec((s, LANES), lambda i, p: (0, 0)),
            pl.BlockSpec((s, LANES), lambda i, p: (0, 0)),
        ],
        out_specs=pl.BlockSpec((1, s, LANES), lambda i, p: (i, 0, p)),
        out_shape=jax.ShapeDtypeStruct((b, s, D_ATT), BF16),
        scratch_shapes=[pltpu.VMEM((s, LANES), BF16)] * 5 + [pltpu.VMEM((s // tq, tq, tk), BF16)],
        compiler_params=pltpu.CompilerParams(
            dimension_semantics=("parallel", "arbitrary"), vmem_limit_bytes=VMEM_LIMIT),
        name="dilated_attention",
    )(z3, z3, z3, cos_t, sin_t)


POOL_PAD = 16


def _pool_kernel(u_ref, w_ref, sc_ref, o_ref, pad_s, *, seq):
    t1 = (lax.broadcasted_iota(jnp.int32, (seq, POOL_GROUP), 0) + 1).astype(F32)
    pad_s[0:POOL_PAD, :] = jnp.zeros((POOL_PAD, POOL_GROUP), F32)
    for gi, w in enumerate(POOL_WINDOWS):
        cols = slice(gi * POOL_GROUP, (gi + 1) * POOL_GROUP)
        u = u_ref[0, :, cols]
        s = u
        k = 1
        while k < w:
            pad_s[POOL_PAD:POOL_PAD + seq, :] = s
            s = s + pad_s[POOL_PAD - k:POOL_PAD - k + seq, :]
            k *= 2
        p = s / jnp.minimum(t1, float(w)) - u
        y = _dot(p.astype(BF16), w_ref[gi])
        o_ref[0, :, cols] = (y * sc_ref[:, cols]).astype(o_ref.dtype)


def _pool(z3, pool_w, pool_scale):
    b, s, _ = z3.shape
    kern = functools.partial(_pool_kernel, seq=s)
    return pl.pallas_call(
        kern,
        grid=(b,),
        in_specs=[
            pl.BlockSpec((1, s, D_POOL), lambda i: (i, 0, COL_POOL // D_POOL)),
            pl.BlockSpec(pool_w.shape, lambda i: (0, 0, 0)),
            pl.BlockSpec((1, D_POOL), lambda i: (0, 0)),
        ],
        out_specs=pl.BlockSpec((1, s, D_POOL), lambda i: (i, 0, 0)),
        out_shape=jax.ShapeDtypeStruct((b, s, D_POOL), BF16),
        scratch_shapes=[pltpu.VMEM((s + POOL_PAD, POOL_GROUP), F32)],
        compiler_params=pltpu.CompilerParams(
            dimension_semantics=("parallel",), vmem_limit_bytes=VMEM_LIMIT),
        name="pool_mixer",
    )(z3, pool_w, pool_scale)


def _split_dot(x, ones_b, terms):
    acc = None
    rem = x
    for _ in range(terms):
        piece = rem.astype(BF16)
        part = _dot(piece, ones_b)
        acc = part if acc is None else acc + part
        rem = rem - piece.astype(F32)
    return acc


N_RWKV_IN = 18


def _rwkv_kernel(*refs, chunk, rows, n_steps, cast_steps):
    (zr_ref, zrp_ref, zl_ref, zlp_ref, mur_ref, mul_ref, wup_ref, w0_ref, aup_ref, a0_ref,
     gup_ref, kk_ref, ka_ref, rk_ref, gnw_ref, gnb_ref, tri_ref, ones_ref) = refs[:N_RWKV_IN]
    n_cast = len(cast_steps)
    cast_in = refs[N_RWKV_IN:N_RWKV_IN + n_cast]
    o_ref = refs[N_RWKV_IN + n_cast]
    cast_out = refs[N_RWKV_IN + n_cast + 1:N_RWKV_IN + 2 * n_cast + 1]
    h_s, sh_s = refs[N_RWKV_IN + 2 * n_cast + 1:]
    c = pl.program_id(1)
    npair = D_RWKV // LANES
    halo = 8
    wr = 3 * D_RWKV

    step = pl.program_id(0) * pl.num_programs(1) + c
    for src, dst, used in zip(cast_in, cast_out, cast_steps):
        if used == n_steps:
            dst[...] = src[...].astype(BF16)
        else:
            @pl.when(step < used)
            def _(src=src, dst=dst):
                dst[...] = src[...].astype(BF16)

    @pl.when(c == 0)
    def _():
        h_s[...] = jnp.zeros_like(h_s)

    row = lax.broadcasted_iota(jnp.int32, (LANES, LANES), 0)
    col = lax.broadcasted_iota(jnp.int32, (LANES, LANES), 1)
    lane = lax.broadcasted_iota(jnp.int32, (chunk, LANES), 1)
    head_a = lane < HEAD_DIM
    strict_lower = row > col
    lower = row >= col
    eye = (row == col).astype(F32)
    tri = tri_ref[...]
    first = c == 0

    def seg_sum(x):
        m = x.shape[0]
        tall = jnp.concatenate([x[:, p * LANES:(p + 1) * LANES] for p in range(npair)], axis=0)
        summed = _split_dot(tall, ones_ref[...], SEG_SUM_TERMS)
        return jnp.concatenate([summed[p * m:(p + 1) * m] for p in range(npair)], axis=1)

    def stack2(x):
        return jnp.concatenate([jnp.where(head_a, x, 0.0), jnp.where(head_a, 0.0, x)], axis=0)

    zr_rows, zl_rows = [], []
    for bi in range(rows):
        zr = zr_ref[bi]
        zl = zl_ref[bi]
        sh_s[bi, 0:halo, 0:wr] = jnp.where(first, 0.0, zrp_ref[bi])
        sh_s[bi, 0:halo, wr:wr + D_LORA_PACK] = jnp.where(first, 0.0, zlp_ref[bi])
        sh_s[bi, halo:halo + chunk, 0:wr] = zr
        sh_s[bi, halo:halo + chunk, wr:wr + D_LORA_PACK] = zl
        zr_prev = sh_s[bi, halo - 1:halo - 1 + chunk, 0:wr]
        zl_prev = sh_s[bi, halo - 1:halo - 1 + chunk, wr:wr + D_LORA_PACK]
        zr_rows.append(zr + (zr_prev - zr) * mur_ref[...])
        zl_rows.append(zl + (zl_prev - zl) * mul_ref[...])
    zr = jnp.concatenate(zr_rows, axis=0)
    zl = jnp.concatenate(zl_rows, axis=0)
    r = zr[:, 0:D_RWKV]
    k = zr[:, D_RWKV:2 * D_RWKV]
    v = zr[:, 2 * D_RWKV:3 * D_RWKV]
    zw = zl[:, 0:LORA_PAD]
    za = zl[:, LORA_PAD:2 * LORA_PAD]
    zg = zl[:, 2 * LORA_PAD:D_LORA_PACK]

    lw = w0_ref[...] + _dot(jnp.tanh(zw).astype(BF16), wup_ref[...])
    ld = -math.exp(-0.5) * jax.nn.sigmoid(lw)
    a = jax.nn.sigmoid(a0_ref[...] + _dot(za.astype(BF16), aup_ref[...]))
    g = _dot(jax.nn.sigmoid(zg).astype(BF16), gup_ref[...])
    kk = k * kk_ref[...]
    k2 = k * (1.0 + (a - 1.0) * ka_ref[...])
    kk = kk / jnp.maximum(jnp.sqrt(seg_sum(kk * kk)), 1e-12)
    bonus = seg_sum(r * k2 * rk_ref[...]) * v
    bb = kk * a

    cum_rows = [_split_dot_left(tri, ld[bi * chunk:(bi + 1) * chunk], 3) for bi in range(rows)]
    cum = jnp.concatenate(cum_rows, axis=0)
    cum_end = jnp.concatenate([jnp.broadcast_to(cr[chunk - 1:chunk, :], cr.shape) for cr in cum_rows], axis=0)
    e_inv = jnp.exp(-cum)
    e_end = jnp.exp(cum_end - cum)
    tok_all = dict(
        r_t=r * jnp.exp(cum), kk_t=kk * jnp.exp(cum - ld), b_h=bb * e_inv, k_h=k2 * e_inv,
        b_a=bb * e_end, k_a=k2 * e_end, v=v)
    tok = [{name: val[bi * chunk:(bi + 1) * chunk] for name, val in tok_all.items()} for bi in range(rows)]
    for bi in range(rows):
        tok[bi]["a_end"] = jnp.exp(cum_rows[bi][chunk - 1:chunk, :])

    units = [(bi, p) for bi in range(rows) for p in range(npair)]

    def pair(name):
        return [tok[bi][name][:, p * LANES:(p + 1) * LANES] for bi, p in units]

    kks = [stack2(x).astype(BF16) for x in pair("kk_t")]
    bs = [stack2(x).astype(BF16) for x in pair("b_h")]
    ks = [stack2(x).astype(BF16) for x in pair("k_h")]
    rs = [stack2(x).astype(BF16) for x in pair("r_t")]
    vs = [stack2(x).astype(BF16) for x in pair("v")]
    bat = [stack2(x).T.astype(BF16) for x in pair("b_a")]
    kat = [stack2(x).T.astype(BF16) for x in pair("k_a")]
    a_col = [jnp.broadcast_to(x, (LANES, LANES)).T for x in pair("a_end")]
    ids = range(len(units))

    tt = [_dot_nt(jnp.concatenate([kks[i], rs[i]], axis=0), jnp.concatenate([bs[i], ks[i]], axis=0)) for i in ids]
    l_b = [jnp.where(strict_lower, tt[i][0:LANES, 0:LANES], 0.0) for i in ids]
    l_k = [jnp.where(strict_lower, tt[i][0:LANES, LANES:2 * LANES], 0.0).astype(BF16) for i in ids]
    m_b = [jnp.where(lower, tt[i][LANES:2 * LANES, 0:LANES], 0.0).astype(BF16) for i in ids]
    m_k = [jnp.where(lower, tt[i][LANES:2 * LANES, LANES:2 * LANES], 0.0).astype(BF16) for i in ids]

    sel = ((row ^ col) < 2) & ((row & 1) != 0) & ((col & 1) == 0)
    t_inv = [eye - jnp.where(sel, l_b[i], 0.0) for i in ids]
    s = 2
    while s < HEAD_DIM:
        sel = ((row ^ col) < 2 * s) & ((row & s) != 0) & ((col & s) == 0)
        t_b = [t_inv[i].astype(BF16) for i in ids]
        inner = [_dot(jnp.where(sel, l_b[i], 0.0).astype(BF16), t_b[i]).astype(BF16) for i in ids]
        t_inv = [t_inv[i] - _dot(t_b[i], inner[i]) for i in ids]
        s *= 2
    t_b = [t_inv[i].astype(BF16) for i in ids]

    mlv = [_dot(jnp.concatenate([m_k[i], l_k[i]], axis=0), vs[i]) for i in ids]
    y0 = [mlv[i][0:LANES, :] for i in ids]
    lkv = [mlv[i][LANES:2 * LANES, :].astype(BF16) for i in ids]
    w_m = [_dot(t_b[i], kks[i]).astype(BF16) for i in ids]

    h = [h_s[i] for i in ids]
    h_b = [h[i].astype(BF16) for i in ids]
    u_b = [(-_dot(jnp.concatenate([w_m[i], t_b[i]], axis=1),
                  jnp.concatenate([h_b[i], lkv[i]], axis=0))).astype(BF16) for i in ids]
    y = [_dot(jnp.concatenate([rs[i], m_b[i]], axis=1), jnp.concatenate([h_b[i], u_b[i]], axis=0)) + y0[i]
         for i in ids]
    for i in ids:
        h_s[i] = a_col[i] * h[i] + _dot(jnp.concatenate([bat[i], kat[i]], axis=1),
                                        jnp.concatenate([u_b[i], vs[i]], axis=0))

    yb = jnp.concatenate(
        [jnp.concatenate([y[bi * npair + p][0:chunk, :] + y[bi * npair + p][chunk:2 * chunk, :]
                          for p in range(npair)], axis=1) for bi in range(rows)], axis=0)
    mean = seg_sum(yb) * (1.0 / HEAD_DIM)
    d = yb - mean
    var = seg_sum(d * d) * (1.0 / HEAD_DIM)
    yn = d * lax.rsqrt(var + RWKV_GN_EPS) * gnw_ref[...] + gnb_ref[...]
    out = ((yn + bonus) * g).astype(o_ref.dtype)
    for bi in range(rows):
        o_ref[bi] = out[bi * chunk:(bi + 1) * chunk]


def _split_dot_left(tri_b, x, terms):
    acc = None
    rem = x
    for _ in range(terms):
        piece = rem.astype(BF16)
        part = _dot(tri_b, piece)
        acc = part if acc is None else acc + part
        rem = rem - piece.astype(F32)
    return acc


BF16_SUBLANES = 16


def _rwkv_ones(chunk):
    t = jnp.arange(chunk)
    tri = (t[:, None] >= t[None, :]).astype(BF16)
    r = jnp.arange(LANES)[:, None]
    c = jnp.arange(LANES)[None, :]
    head_ones = ((r < HEAD_DIM) == (c < HEAD_DIM)).astype(BF16)
    return tri, head_ones


def _rwkv(z3, prm, *, rows, cast=()):
    b, s, _ = z3.shape
    chunk = RWKV_CHUNK
    nc = s // chunk
    nb = b // rows
    wr = 3 * D_RWKV
    rows8 = chunk // 8
    npair = D_RWKV // LANES

    def vec(n):
        return pl.BlockSpec((1, n), lambda i, c: (0, 0))

    def full(a):
        return pl.BlockSpec(a.shape, lambda i, c: (0, 0))

    tri, head_ones = _rwkv_ones(chunk)

    n_steps = nb * nc
    cast_in_specs, cast_out_specs, cast_shapes, cast_steps = [], [], [], []
    for w, layer in cast:
        _, r, cc = w.shape
        assert r % BF16_SUBLANES == 0
        used = max(k for k in range(1, n_steps + 1) if (r // BF16_SUBLANES) % k == 0)
        tr = r // used
        row_blk = lambda i, c, last=used - 1: jnp.minimum(i * nc + c, last)
        cast_in_specs.append(pl.BlockSpec((1, tr, cc), lambda i, c, f=row_blk, l=layer: (l, f(i, c), 0)))
        cast_out_specs.append(pl.BlockSpec((1, tr, cc), lambda i, c, f=row_blk: (0, f(i, c), 0)))
        cast_shapes.append(jax.ShapeDtypeStruct((1, r, cc), BF16))
        cast_steps.append(used)
    kern = functools.partial(_rwkv_kernel, chunk=chunk, rows=rows, n_steps=n_steps, cast_steps=tuple(cast_steps))

    outs = pl.pallas_call(
        kern,
        grid=(nb, nc),
        in_specs=[
            pl.BlockSpec((rows, chunk, wr), lambda i, c: (i, c, COL_RKV // wr)),
            pl.BlockSpec((rows, 8, wr), lambda i, c: (i, jnp.maximum(c * rows8 - 1, 0), COL_RKV // wr)),
            pl.BlockSpec((rows, chunk, D_LORA_PACK), lambda i, c: (i, c, COL_LORA // D_LORA_PACK)),
            pl.BlockSpec((rows, 8, D_LORA_PACK),
                         lambda i, c: (i, jnp.maximum(c * rows8 - 1, 0), COL_LORA // D_LORA_PACK)),
            vec(wr), vec(D_LORA_PACK),
            full(prm["w_up"]), vec(D_RWKV), full(prm["a_up"]), vec(D_RWKV), full(prm["g_up"]),
            vec(D_RWKV), vec(D_RWKV), vec(D_RWKV), vec(D_RWKV), vec(D_RWKV),
            full(tri), full(head_ones),
        ] + cast_in_specs,
        out_specs=[pl.BlockSpec((rows, chunk, D_RWKV), lambda i, c: (i, c, 0))] + cast_out_specs,
        out_shape=[jax.ShapeDtypeStruct((b, s, D_RWKV), BF16)] + cast_shapes,
        scratch_shapes=[
            pltpu.VMEM((rows * npair, LANES, LANES), F32),
            pltpu.VMEM((rows, chunk + 8, wr + D_LORA_PACK), F32),
        ],
        compiler_params=pltpu.CompilerParams(
            dimension_semantics=("arbitrary", "arbitrary"), vmem_limit_bytes=VMEM_LIMIT),
        name="rwkv7",
    )(z3, z3, z3, z3, prm["mu_r"], prm["mu_l"], prm["w_up"], prm["w0"], prm["a_up"], prm["a0"],
      prm["g_up"], prm["k_k"], prm["k_a"], prm["r_k"], prm["gn_w"], prm["gn_b"],
      tri, head_ones, *[w for w, _ in cast])
    return outs[0], list(outs[1:])


def _pad_cols(w, n):
    return jnp.pad(w, ((0, 0),) * (w.ndim - 1) + ((0, n - w.shape[-1]),))


def _pack_w_in_t(wt):
    o = 3 * D_ATT + 3 * D_RWKV
    o2 = o + RWKV_DECAY_LORA
    o3 = o2 + RWKV_AAA_LORA
    pad = ((0, 0), (0, LORA_PAD - RWKV_DECAY_LORA), (0, 0))
    return jnp.concatenate([wt[:, :o], jnp.pad(wt[:, o:o2], pad), jnp.pad(wt[:, o2:o3], pad), wt[:, o3:]], axis=1)


def _pack_mu(mu):
    o = 3 * D_RWKV
    mu_l = jnp.concatenate([
        jnp.pad(mu[o:o + RWKV_DECAY_LORA], (0, LORA_PAD - RWKV_DECAY_LORA)),
        jnp.pad(mu[o + RWKV_DECAY_LORA:o + RWKV_DECAY_LORA + RWKV_AAA_LORA], (0, LORA_PAD - RWKV_AAA_LORA)),
        mu[o + RWKV_DECAY_LORA + RWKV_AAA_LORA:],
    ])
    return mu[None, :o], mu_l[None, :]


def _pad_rows(w, n):
    return jnp.pad(w, ((0, n - w.shape[0]), (0, 0)))


def _rope_tables(seq):
    inv_freq = ROPE_THETA ** (-jnp.arange(0, HEAD_DIM, 2, dtype=F32) / HEAD_DIM)
    ang = jnp.arange(seq, dtype=F32)[:, None] * inv_freq[None, :]
    cos, sin = jnp.cos(ang), jnp.sin(ang)
    reps = LANES // HEAD_DIM
    cos_t = jnp.tile(jnp.concatenate([cos, cos], axis=1), (1, reps))
    sin_t = jnp.tile(jnp.concatenate([-sin, sin], axis=1), (1, reps))
    return cos_t, sin_t


def _tile(n, pref):
    t = min(n, pref)
    assert n % t == 0
    return t


def _row_tile(r, pref):
    return max(k for k in range(BF16_SUBLANES, pref + 1, BF16_SUBLANES) if r % k == 0)


def kernel(x, ffn1_norm_pre, ffn1_norm_post, ffn1_w_gate, ffn1_w_up, ffn1_w_down, mix_norm_pre, mix_norm_post, w_in, w_out, rwkv_mu, rwkv_w_up, rwkv_w0, rwkv_a_up, rwkv_a0, rwkv_g_up, rwkv_k_k, rwkv_k_a, rwkv_r_k, rwkv_gn_w, rwkv_gn_b, pool_w, pool_scale, ffn2_norm_pre, ffn2_norm_post, ffn2_w_gate, ffn2_w_up, ffn2_w_down):
    b, s, d = x.shape
    depth = w_in.shape[0]
    t = b * s
    d_ff = ffn1_w_gate.shape[2]
    assert s % RWKV_CHUNK == 0 and b % RWKV_ROWS == 0 and d == D_ATT + D_RWKV + D_POOL
    tm = _tile(t, 512)
    tm_ffn = _tile(t, 1024)
    tm_proj = _tile(t, 512)
    tf = _tile(d_ff, 512)
    tn = _tile(D_PACK, D_PACK // 2)
    cos_t, sin_t = _rope_tables(s)
    x2 = x.reshape(t, d).astype(F32)

    w_in_t = jnp.swapaxes(w_in, 1, 2)
    big = (ffn1_w_gate, ffn1_w_up, ffn1_w_down, ffn2_w_gate, ffn2_w_up, ffn2_w_down, w_in_t, w_out)
    early = (0, 1, 2, 6)
    wb = {k: _cast_bf16(big[k], layer=0, tr=_row_tile(big[k].shape[1], 256)) for k in early}
    ffn_g = [[g[:, None, :] for g in gs] for gs in
             ((ffn1_norm_pre, ffn1_norm_post), (ffn2_norm_pre, ffn2_norm_post))]
    mix_g_pre, mix_g_post = mix_norm_pre[:, None, :], mix_norm_post[:, None, :]

    def gains(gs, i):
        return [g[i:i + 1] for g in gs]

    for i in range(depth):
        x2 = _ffn(x2, *gains(ffn_g[0], i), wb[0], wb[1], wb[2], layer=0, tm=tm_ffn, tf=tf)

        z = _proj(x2, mix_g_pre[i:i + 1], _pack_w_in_t(wb[6]), layer=0, tm=tm_proj, tn=tn)
        z3 = z.reshape(b, s, D_PACK)
        y_att = _attention(z3, cos_t, sin_t)
        mu_r, mu_l = _pack_mu(rwkv_mu[i])
        prm = dict(
            mu_r=mu_r, mu_l=mu_l,
            w_up=_pad_rows(rwkv_w_up[i], LORA_PAD).astype(BF16), w0=rwkv_w0[i][None],
            a_up=_pad_rows(rwkv_a_up[i], LORA_PAD).astype(BF16), a0=rwkv_a0[i][None],
            g_up=rwkv_g_up[i].astype(BF16), k_k=rwkv_k_k[i][None], k_a=rwkv_k_a[i][None],
            r_k=rwkv_r_k[i].reshape(1, D_RWKV), gn_w=rwkv_gn_w[i][None], gn_b=rwkv_gn_b[i][None])
        late = [k for k in range(len(big)) if k not in wb]
        nxt = list(range(len(big))) if i + 1 < depth else []
        y_rwkv, cast_out = _rwkv(z3, prm, rows=RWKV_ROWS,
                                 cast=[(big[k], i) for k in late] + [(big[k], i + 1) for k in nxt])
        wb.update(zip(late, cast_out[:len(late)]))
        wb_next = dict(zip(nxt, cast_out[len(late):]))
        y_pool = _pool(z3, pool_w[i].astype(BF16), pool_scale[i][None])

        x2 = _mixout(x2, y_att.reshape(t, D_ATT), y_rwkv.reshape(t, D_RWKV), y_pool.reshape(t, D_POOL),
                     wb[7], mix_g_post[i:i + 1], layer=0, tm=tm)

        x2 = _ffn(x2, *gains(ffn_g[1], i), wb[3], wb[4], wb[5], layer=0, tm=tm_ffn, tf=tf)
        wb = wb_next
    return x2.reshape(b, s, d).astype(x.dtype)
```

```python
import functools
import math

import jax
import jax.numpy as jnp
from jax import lax
from jax.experimental import pallas as pl
from jax.experimental.pallas import tpu as pltpu

F32 = jnp.float32
BF16 = jnp.bfloat16

HEAD_DIM = 64
N_ATT_HEADS = 12
N_RWKV_HEADS = 12
D_ATT = N_ATT_HEADS * HEAD_DIM
D_RWKV = N_RWKV_HEADS * HEAD_DIM
D_POOL = 512
N_POOL_GROUPS = 4
POOL_GROUP = D_POOL // N_POOL_GROUPS
POOL_WINDOWS = (2, 4, 8, 16)
DILATED_BRANCHES = ((128, 1), (512, 4), (2048, 16))
ROPE_THETA = 10000.0
RWKV_DECAY_LORA = 96
RWKV_AAA_LORA = 96
RWKV_GATE_LORA = 256
NORM_EPS = 1e-6
RWKV_GN_EPS = 64e-5

LANES = 128
LORA_PAD = 128
COL_ATT = 0
COL_RKV = 3 * D_ATT
COL_LORA = COL_RKV + 3 * D_RWKV
D_LORA_PACK = 2 * LORA_PAD + RWKV_GATE_LORA
COL_POOL = COL_LORA + D_LORA_PACK
D_PACK = COL_POOL + D_POOL

SEG_SUM_TERMS = 1
RWKV_ROWS = 4
RWKV_CHUNK = 64
ATT_KV_TILE = 512
ATT_GROUP = 2
VMEM_LIMIT = 56 * 1024 * 1024
VMEM_LIMIT_FFN = 62 * 1024 * 1024

NEG_BIG = -1e30
LOG2_E = math.log2(math.e)


def _rms(xf, g):
    return xf * lax.rsqrt(jnp.mean(xf * xf, axis=-1, keepdims=True) + NORM_EPS) * g


def _dot(a, b):
    return jnp.dot(a, b, preferred_element_type=F32)


def _dot_nt(a, b):
    return lax.dot_general(a, b, (((1,), (1,)), ((), ())), preferred_element_type=F32)


def _ffn_kernel(x_ref, gpre_ref, gpost_ref, wg_ref, wu_ref, wd_ref, o_ref, h_ref, ssq_ref):
    j = pl.program_id(1)

    @pl.when(j == 0)
    def _():
        h_ref[...] = _rms(x_ref[...], gpre_ref[...]).astype(BF16)
        o_ref[...] = jnp.zeros_like(o_ref)

    half = o_ref.shape[0] // 2
    for r0 in (0, half):
        rows = slice(r0, r0 + half)
        h = h_ref[rows, :]
        g = _dot(h, wg_ref[...])
        u = _dot(h, wu_ref[...])
        a = (g * jax.nn.sigmoid(g) * u).astype(BF16)
        acc = o_ref[rows, :] + _dot(a, wd_ref[...])
        o_ref[rows, :] = acc
        ssq_ref[rows, :] = jnp.sum(acc * acc, axis=-1, keepdims=True)

    @pl.when(j == pl.num_programs(1) - 1)
    def _():
        inv = lax.rsqrt(ssq_ref[...] * (1.0 / o_ref.shape[1]) + NORM_EPS) * 0.5
        o_ref[...] = x_ref[...] + o_ref[...] * inv * gpost_ref[...]


def _ffn(x, g_pre, g_post, wg, wu, wd, *, layer, tm, tf):
    t, d = x.shape
    f = wg.shape[2]
    return pl.pallas_call(
        _ffn_kernel,
        grid=(t // tm, f // tf),
        in_specs=[
            pl.BlockSpec((tm, d), lambda i, j: (i, 0)),
            pl.BlockSpec((None, 1, d), lambda i, j: (layer, 0, 0)),
            pl.BlockSpec((None, 1, d), lambda i, j: (layer, 0, 0)),
            pl.BlockSpec((None, d, tf), lambda i, j: (layer, 0, j)),
            pl.BlockSpec((None, d, tf), lambda i, j: (layer, 0, j)),
            pl.BlockSpec((None, tf, d), lambda i, j: (layer, j, 0)),
        ],
        out_specs=pl.BlockSpec((tm, d), lambda i, j: (i, 0)),
        out_shape=jax.ShapeDtypeStruct((t, d), F32),
        scratch_shapes=[pltpu.VMEM((tm, d), BF16), pltpu.VMEM((tm, 1), F32)],
        compiler_params=pltpu.CompilerParams(
            dimension_semantics=("parallel", "arbitrary"), vmem_limit_bytes=VMEM_LIMIT_FFN),
        name="ffn",
    )(x, g_pre, g_post, wg, wu, wd)


def _proj_kernel(x_ref, g_ref, wt_ref, o_ref):
    o_ref[...] = _dot_nt(_rms(x_ref[...], g_ref[...]).astype(BF16), wt_ref[...])


def _proj(x, g, wt, *, layer, tm, tn):
    t, d = x.shape
    n = wt.shape[1]
    return pl.pallas_call(
        _proj_kernel,
        grid=(n // tn, t // tm),
        in_specs=[
            pl.BlockSpec((tm, d), lambda j, i: (i, 0)),
            pl.BlockSpec((None, 1, d), lambda j, i: (layer, 0, 0)),
            pl.BlockSpec((None, tn, d), lambda j, i: (layer, j, 0), pipeline_mode=pl.Buffered(1)),
        ],
        out_specs=pl.BlockSpec((tm, tn), lambda j, i: (i, j)),
        out_shape=jax.ShapeDtypeStruct((t, n), F32),
        compiler_params=pltpu.CompilerParams(
            dimension_semantics=("arbitrary", "arbitrary"), vmem_limit_bytes=VMEM_LIMIT_FFN),
        name="proj_in",
    )(x, g, wt)


def _mixout_kernel(x_ref, ya_ref, yr_ref, yp_ref, wa_ref, wr_ref, wp_ref, g_ref, o_ref):
    mix = _dot(ya_ref[...], wa_ref[...]) + _dot(yr_ref[...], wr_ref[...]) + _dot(yp_ref[...], wp_ref[...])
    o_ref[...] = x_ref[...] + _rms(mix, g_ref[...])


def _mixout(x, ya, yr, yp, wo, g, *, layer, tm):
    t, d = x.shape
    da, dr, dp = ya.shape[1], yr.shape[1], yp.shape[1]
    assert da == dr and (da + dr) % dp == 0
    return pl.pallas_call(
        _mixout_kernel,
        grid=(t // tm,),
        in_specs=[
            pl.BlockSpec((tm, d), lambda i: (i, 0)),
            pl.BlockSpec((tm, da), lambda i: (i, 0)),
            pl.BlockSpec((tm, dr), lambda i: (i, 0)),
            pl.BlockSpec((tm, dp), lambda i: (i, 0)),
            pl.BlockSpec((None, da, d), lambda i: (layer, 0, 0)),
            pl.BlockSpec((None, dr, d), lambda i: (layer, 1, 0)),
            pl.BlockSpec((None, dp, d), lambda i: (layer, (da + dr) // dp, 0)),
            pl.BlockSpec((None, 1, d), lambda i: (layer, 0, 0)),
        ],
        out_specs=pl.BlockSpec((tm, d), lambda i: (i, 0)),
        out_shape=jax.ShapeDtypeStruct((t, d), F32),
        compiler_params=pltpu.CompilerParams(
            dimension_semantics=("parallel",), vmem_limit_bytes=VMEM_LIMIT),
        name="proj_out",
    )(x, ya, yr, yp, wo, wo, wo, g)


def _cast_kernel(w_ref, o_ref):
    o_ref[...] = w_ref[...].astype(BF16)


def _cast_bf16(w, *, layer, tr):
    _, r, c = w.shape
    return pl.pallas_call(
        _cast_kernel,
        grid=(r // tr,),
        in_specs=[pl.BlockSpec((1, tr, c), lambda j: (layer, j, 0))],
        out_specs=pl.BlockSpec((1, tr, c), lambda j: (0, j, 0)),
        out_shape=jax.ShapeDtypeStruct((1, r, c), BF16),
        compiler_params=pltpu.CompilerParams(
            dimension_semantics=("parallel",), vmem_limit_bytes=VMEM_LIMIT),
        name="cast_bf16",
    )(w)


def _att_bias_tile(base, tq, tk):
    row = lax.broadcasted_iota(jnp.int32, (tq, tk), 0)
    col = lax.broadcasted_iota(jnp.int32, (tq, tk), 1)
    delta = base + row - col
    cnt = jnp.zeros((tq, tk), jnp.int32)
    for window, dil in DILATED_BRANCHES:
        hit = (delta >= 0) & (delta <= window) & ((delta & (dil - 1)) == 0)
        cnt = cnt + hit.astype(jnp.int32)
    return jnp.where(cnt == 3, math.log2(3.0),
                     jnp.where(cnt == 2, 1.0, jnp.where(cnt == 1, 0.0, NEG_BIG))).astype(BF16)


def _att_kernel(q_ref, k_ref, v_ref, cos_ref, sin_ref, o_ref,
                qa_s, qb_s, k_s, va_s, vb_s, bias_s, *, seq, tq, tk, group):
    nq = seq // tq
    ratio = tk // tq

    @pl.when(pl.program_id(1) == 0)
    def _():
        for off in range(nq):
            bias_s[off] = _att_bias_tile(off * tq, tq, tk)

    lane = lax.broadcasted_iota(jnp.int32, (tq, LANES), 1)
    first_half = (lane & (HEAD_DIM - 1)) < (HEAD_DIM // 2)
    head_a = lane < HEAD_DIM

    def rope(x, cos, sin):
        swapped = jnp.where(first_half, pltpu.roll(x, LANES - HEAD_DIM // 2, 1), pltpu.roll(x, HEAD_DIM // 2, 1))
        return x * cos + swapped * sin

    staged_q, staged_kv = set(), set()

    def stage_q(i):
        if i in staged_q:
            return
        staged_q.add(i)
        rows = slice(i * tq, (i + 1) * tq)
        q = rope(q_ref[0, rows, :], cos_ref[rows, :], sin_ref[rows, :]) * (HEAD_DIM ** -0.5 * LOG2_E)
        qa_s[rows, :] = jnp.where(head_a, q, 0.0).astype(BF16)
        qb_s[rows, :] = jnp.where(head_a, 0.0, q).astype(BF16)

    def stage_kv(i):
        if i in staged_kv:
            return
        staged_kv.add(i)
        rows = slice(i * tq, (i + 1) * tq)
        k_s[rows, :] = rope(k_ref[0, rows, :], cos_ref[rows, :], sin_ref[rows, :]).astype(BF16)
        v = v_ref[0, rows, :]
        va_s[rows, :] = jnp.where(head_a, v, 1.0).astype(BF16)
        vb_s[rows, :] = jnp.where(head_a, 1.0, v).astype(BF16)

    def tile_max(s):
        m = s[:, 0:LANES]
        for t in range(1, s.shape[1] // LANES):
            m = jnp.maximum(m, s[:, t * LANES:(t + 1) * LANES])
        return m

    q_s = (qa_s, qb_s)
    v_s = (va_s, vb_s)
    m_run = {}
    acc = {}
    for kj in range(seq // tk):
        blocks = list(range(ratio * kj, nq))
        for g0 in range(0, len(blocks), group):
            chains = [(qi, hd) for qi in blocks[g0:g0 + group] for hd in range(2)]
            width = {qi: (tq if qi == ratio * kj else tk) for qi, _ in chains}
            for qi, _ in chains:
                stage_q(qi)
                for kb in range(ratio * kj, ratio * kj + width[qi] // tq):
                    stage_kv(kb)
            s = {}
            for qi, hd in chains:
                krows = slice(kj * tk, kj * tk + width[qi])
                s[qi, hd] = (_dot_nt(q_s[hd][qi * tq:(qi + 1) * tq, :], k_s[krows, :]).astype(BF16)
                             + bias_s[qi - ratio * kj, :, 0:width[qi]])
            p = {}
            for ch in chains:
                m_tile = jnp.max(tile_max(s[ch]).astype(F32), axis=-1, keepdims=True)
                if kj == 0:
                    m_new = m_tile
                else:
                    m_new = jnp.maximum(m_run[ch], m_tile)
                    acc[ch] = jnp.exp2(m_run[ch] - m_new) * acc[ch]
                m_run[ch] = m_new
                p[ch] = jnp.exp2(s[ch] - m_new.astype(BF16))
            for ch in chains:
                pv = _dot(p[ch], v_s[ch[1]][kj * tk:kj * tk + width[ch[0]], :])
                acc[ch] = pv if kj == 0 else acc[ch] + pv

    for qi in range(nq):
        o_a = acc[qi, 0] / pltpu.roll(acc[qi, 0], HEAD_DIM, 1)
        o_b = acc[qi, 1] / pltpu.roll(acc[qi, 1], HEAD_DIM, 1)
        o_ref[0, qi * tq:(qi + 1) * tq, :] = jnp.where(head_a, o_a, o_b).astype(o_ref.dtype)


def _attention(z3, cos_t, sin_t):
    b, s, _ = z3.shape
    tk = min(ATT_KV_TILE, s)
    tq = tk // 2
    npair = D_ATT // LANES
    kern = functools.partial(_att_kernel, seq=s, tq=tq, tk=tk, group=ATT_GROUP)
    return pl.pallas_call(
        kern,
        grid=(b, npair),
        in_specs=[
            pl.BlockSpec((1, s, LANES), lambda i, p: (i, 0, COL_ATT // LANES + p)),
            pl.BlockSpec((1, s, LANES), lambda i, p: (i, 0, (COL_ATT + D_ATT) // LANES + p)),
            pl.BlockSpec((1, s, LANES), lambda i, p: (i, 0, (COL_ATT + 2 * D_ATT) // LANES + p)),
            pl.BlockSpec((s, LANES), lambda i, p: (0, 0)),
            pl.BlockSpec((s, LANES), lambda i, p: (0, 0)),
        ],
        out_specs=pl.BlockSpec((1, s, LANES), lambda i, p: (i, 0, p)),
        out_shape=jax.ShapeDtypeStruct((b, s, D_ATT), BF16),
        scratch_shapes=[pltpu.VMEM((s, LANES), BF16)] * 5 + [pltpu.VMEM((s // tq, tq, tk), BF16)],
        compiler_params=pltpu.CompilerParams(
            dimension_semantics=("parallel", "arbitrary"), vmem_limit_bytes=VMEM_LIMIT),
        name="dilated_attention",
    )(z3, z3, z3, cos_t, sin_t)


POOL_PAD = 16


def _pool_kernel(u_ref, w_ref, sc_ref, o_ref, pad_s, *, seq):
    t1 = (lax.broadcasted_iota(jnp.int32, (seq, POOL_GROUP), 0) + 1).astype(F32)
    pad_s[0:POOL_PAD, :] = jnp.zeros((POOL_PAD, POOL_GROUP), F32)
    for gi, w in enumerate(POOL_WINDOWS):
        cols = slice(gi * POOL_GROUP, (gi + 1) * POOL_GROUP)
        u = u_ref[0, :, cols]
        s = u
        k = 1
        while k < w:
            pad_s[POOL_PAD:POOL_PAD + seq, :] = s
            s = s + pad_s[POOL_PAD - k:POOL_PAD - k + seq, :]
            k *= 2
        p = s / jnp.minimum(t1, float(w)) - u
        y = _dot(p.astype(BF16), w_ref[gi])
        o_ref[0, :, cols] = (y * sc_ref[:, cols]).astype(o_ref.dtype)


def _pool(z3, pool_w, pool_scale):
    b, s, _ = z3.shape
    kern = functools.partial(_pool_kernel, seq=s)
    return pl.pallas_call(
        kern,
        grid=(b,),
        in_specs=[
            pl.BlockSpec((1, s, D_POOL), lambda i: (i, 0, COL_POOL // D_POOL)),
            pl.BlockSpec(pool_w.shape, lambda i: (0, 0, 0)),
            pl.BlockSpec((1, D_POOL), lambda i: (0, 0)),
        ],
        out_specs=pl.BlockSpec((1, s, D_POOL), lambda i: (i, 0, 0)),
        out_shape=jax.ShapeDtypeStruct((b, s, D_POOL), BF16),
        scratch_shapes=[pltpu.VMEM((s + POOL_PAD, POOL_GROUP), F32)],
        compiler_params=pltpu.CompilerParams(
            dimension_semantics=("parallel",), vmem_limit_bytes=VMEM_LIMIT),
        name="pool_mixer",
    )(z3, pool_w, pool_scale)


def _split_dot(x, ones_b, terms):
    acc = None
    rem = x
    for _ in range(terms):
        piece = rem.astype(BF16)
        part = _dot(piece, ones_b)
        acc = part if acc is None else acc + part
        rem = rem - piece.astype(F32)
    return acc


N_RWKV_IN = 18


def _rwkv_kernel(*refs, chunk, rows, n_steps, cast_steps):
    (zr_ref, zrp_ref, zl_ref, zlp_ref, mur_ref, mul_ref, wup_ref, w0_ref, aup_ref, a0_ref,
     gup_ref, kk_ref, ka_ref, rk_ref, gnw_ref, gnb_ref, tri_ref, ones_ref) = refs[:N_RWKV_IN]
    n_cast = len(cast_steps)
    cast_in = refs[N_RWKV_IN:N_RWKV_IN + n_cast]
    o_ref = refs[N_RWKV_IN + n_cast]
    cast_out = refs[N_RWKV_IN + n_cast + 1:N_RWKV_IN + 2 * n_cast + 1]
    h_s, sh_s = refs[N_RWKV_IN + 2 * n_cast + 1:]
    c = pl.program_id(1)
    npair = D_RWKV // LANES
    halo = 8
    wr = 3 * D_RWKV

    step = pl.program_id(0) * pl.num_programs(1) + c
    for src, dst, used in zip(cast_in, cast_out, cast_steps):
        if used == n_steps:
            dst[...] = src[...].astype(BF16)
        else:
            @pl.when(step < used)
            def _(src=src, dst=dst):
                dst[...] = src[...].astype(BF16)

    @pl.when(c == 0)
    def _():
        h_s[...] = jnp.zeros_like(h_s)

    row = lax.broadcasted_iota(jnp.int32, (LANES, LANES), 0)
    col = lax.broadcasted_iota(jnp.int32, (LANES, LANES), 1)
    lane = lax.broadcasted_iota(jnp.int32, (chunk, LANES), 1)
    head_a = lane < HEAD_DIM
    strict_lower = row > col
    lower = row >= col
    eye = (row == col).astype(F32)
    tri = tri_ref[...]
    first = c == 0

    def seg_sum(x):
        m = x.shape[0]
        tall = jnp.concatenate([x[:, p * LANES:(p + 1) * LANES] for p in range(npair)], axis=0)
        summed = _split_dot(tall, ones_ref[...], SEG_SUM_TERMS)
        return jnp.concatenate([summed[p * m:(p + 1) * m] for p in range(npair)], axis=1)

    def stack2(x):
        return jnp.concatenate([jnp.where(head_a, x, 0.0), jnp.where(head_a, 0.0, x)], axis=0)

    zr_rows, zl_rows = [], []
    for bi in range(rows):
        zr = zr_ref[bi]
        zl = zl_ref[bi]
        sh_s[bi, 0:halo, 0:wr] = jnp.where(first, 0.0, zrp_ref[bi])
        sh_s[bi, 0:halo, wr:wr + D_LORA_PACK] = jnp.where(first, 0.0, zlp_ref[bi])
        sh_s[bi, halo:halo + chunk, 0:wr] = zr
        sh_s[bi, halo:halo + chunk, wr:wr + D_LORA_PACK] = zl
        zr_prev = sh_s[bi, halo - 1:halo - 1 + chunk, 0:wr]
        zl_prev = sh_s[bi, halo - 1:halo - 1 + chunk, wr:wr + D_LORA_PACK]
        zr_rows.append(zr + (zr_prev - zr) * mur_ref[...])
        zl_rows.append(zl + (zl_prev - zl) * mul_ref[...])
    zr = jnp.concatenate(zr_rows, axis=0)
    zl = jnp.concatenate(zl_rows, axis=0)
    r = zr[:, 0:D_RWKV]
    k = zr[:, D_RWKV:2 * D_RWKV]
    v = zr[:, 2 * D_RWKV:3 * D_RWKV]
    zw = zl[:, 0:LORA_PAD]
    za = zl[:, LORA_PAD:2 * LORA_PAD]
    zg = zl[:, 2 * LORA_PAD:D_LORA_PACK]

    lw = w0_ref[...] + _dot(jnp.tanh(zw).astype(BF16), wup_ref[...])
    ld = -math.exp(-0.5) * jax.nn.sigmoid(lw)
    a = jax.nn.sigmoid(a0_ref[...] + _dot(za.astype(BF16), aup_ref[...]))
    g = _dot(jax.nn.sigmoid(zg).astype(BF16), gup_ref[...])
    kk = k * kk_ref[...]
    k2 = k * (1.0 + (a - 1.0) * ka_ref[...])
    kk = kk / jnp.maximum(jnp.sqrt(seg_sum(kk * kk)), 1e-12)
    bonus = seg_sum(r * k2 * rk_ref[...]) * v
    bb = kk * a

    cum_rows = [_split_dot_left(tri, ld[bi * chunk:(bi + 1) * chunk], 3) for bi in range(rows)]
    cum = jnp.concatenate(cum_rows, axis=0)
    cum_end = jnp.concatenate([jnp.broadcast_to(cr[chunk - 1:chunk, :], cr.shape) for cr in cum_rows], axis=0)
    e_inv = jnp.exp(-cum)
    e_end = jnp.exp(cum_end - cum)
    tok_all = dict(
        r_t=r * jnp.exp(cum), kk_t=kk * jnp.exp(cum - ld), b_h=bb * e_inv, k_h=k2 * e_inv,
        b_a=bb * e_end, k_a=k2 * e_end, v=v)
    tok = [{name: val[bi * chunk:(bi + 1) * chunk] for name, val in tok_all.items()} for bi in range(rows)]
    for bi in range(rows):
        tok[bi]["a_end"] = jnp.exp(cum_rows[bi][chunk - 1:chunk, :])

    units = [(bi, p) for bi in range(rows) for p in range(npair)]

    def pair(name):
        return [tok[bi][name][:, p * LANES:(p + 1) * LANES] for bi, p in units]

    kks = [stack2(x).astype(BF16) for x in pair("kk_t")]
    bs = [stack2(x).astype(BF16) for x in pair("b_h")]
    ks = [stack2(x).astype(BF16) for x in pair("k_h")]
    rs = [stack2(x).astype(BF16) for x in pair("r_t")]
    vs = [stack2(x).astype(BF16) for x in pair("v")]
    bat = [stack2(x).T.astype(BF16) for x in pair("b_a")]
    kat = [stack2(x).T.astype(BF16) for x in pair("k_a")]
    a_col = [jnp.broadcast_to(x, (LANES, LANES)).T for x in pair("a_end")]
    ids = range(len(units))

    tt = [_dot_nt(jnp.concatenate([kks[i], rs[i]], axis=0), jnp.concatenate([bs[i], ks[i]], axis=0)) for i in ids]
    l_b = [jnp.where(strict_lower, tt[i][0:LANES, 0:LANES], 0.0) for i in ids]
    l_k = [jnp.where(strict_lower, tt[i][0:LANES, LANES:2 * LANES], 0.0).astype(BF16) for i in ids]
    m_b = [jnp.where(lower, tt[i][LANES:2 * LANES, 0:LANES], 0.0).astype(BF16) for i in ids]
    m_k = [jnp.where(lower, tt[i][LANES:2 * LANES, LANES:2 * LANES], 0.0).astype(BF16) for i in ids]

    sel = ((row ^ col) < 2) & ((row & 1) != 0) & ((col & 1) == 0)
    t_inv = [eye - jnp.where(sel, l_b[i], 0.0) for i in ids]
    s = 2
    while s < HEAD_DIM:
        sel = ((row ^ col) < 2 * s) & ((row & s) != 0) & ((col & s) == 0)
        t_b = [t_inv[i].astype(BF16) for i in ids]
        inner = [_dot(jnp.where(sel, l_b[i], 0.0).astype(BF16), t_b[i]).astype(BF16) for i in ids]
        t_inv = [t_inv[i] - _dot(t_b[i], inner[i]) for i in ids]
        s *= 2
    t_b = [t_inv[i].astype(BF16) for i in ids]

    mlv = [_dot(jnp.concatenate([m_k[i], l_k[i]], axis=0), vs[i]) for i in ids]
    y0 = [mlv[i][0:LANES, :] for i in ids]
    lkv = [mlv[i][LANES:2 * LANES, :].astype(BF16) for i in ids]
    w_m = [_dot(t_b[i], kks[i]).astype(BF16) for i in ids]

    h = [h_s[i] for i in ids]
    h_b = [h[i].astype(BF16) for i in ids]
    u_b = [(-_dot(jnp.concatenate([w_m[i], t_b[i]], axis=1),
                  jnp.concatenate([h_b[i], lkv[i]], axis=0))).astype(BF16) for i in ids]
    y = [_dot(jnp.concatenate([rs[i], m_b[i]], axis=1), jnp.concatenate([h_b[i], u_b[i]], axis=0)) + y0[i]
         for i in ids]
    for i in ids:
        h_s[i] = a_col[i] * h[i] + _dot(jnp.concatenate([bat[i], kat[i]], axis=1),
                                        jnp.concatenate([u_b[i], vs[i]], axis=0))

    yb = jnp.concatenate(
        [jnp.concatenate([y[bi * npair + p][0:chunk, :] + y[bi * npair + p][chunk:2 * chunk, :]
                          for p in range(npair)], axis=1) for bi in range(rows)], axis=0)
    mean = seg_sum(yb) * (1.0 / HEAD_DIM)
    d = yb - mean
    var = seg_sum(d * d) * (1.0 / HEAD_DIM)
    yn = d * lax.rsqrt(var + RWKV_GN_EPS) * gnw_ref[...] + gnb_ref[...]
    out = ((yn + bonus) * g).astype(o_ref.dtype)
    for bi in range(rows):
        o_ref[bi] = out[bi * chunk:(bi + 1) * chunk]


def _split_dot_left(tri_b, x, terms):
    acc = None
    rem = x
    for _ in range(terms):
        piece = rem.astype(BF16)
        part = _dot(tri_b, piece)
        acc = part if acc is None else acc + part
        rem = rem - piece.astype(F32)
    return acc


BF16_SUBLANES = 16


def _rwkv_ones(chunk):
    t = jnp.arange(chunk)
    tri = (t[:, None] >= t[None, :]).astype(BF16)
    r = jnp.arange(LANES)[:, None]
    c = jnp.arange(LANES)[None, :]
    head_ones = ((r < HEAD_DIM) == (c < HEAD_DIM)).astype(BF16)
    return tri, head_ones


def _rwkv(z3, prm, *, rows, cast=()):
    b, s, _ = z3.shape
    chunk = RWKV_CHUNK
    nc = s // chunk
    nb = b // rows
    wr = 3 * D_RWKV
    rows8 = chunk // 8
    npair = D_RWKV // LANES

    def vec(n):
        return pl.BlockSpec((1, n), lambda i, c: (0, 0))

    def full(a):
        return pl.BlockSpec(a.shape, lambda i, c: (0, 0))

    tri, head_ones = _rwkv_ones(chunk)

    n_steps = nb * nc
    cast_in_specs, cast_out_specs, cast_shapes, cast_steps = [], [], [], []
    for w, layer in cast:
        _, r, cc = w.shape
        assert r % BF16_SUBLANES == 0
        used = max(k for k in range(1, n_steps + 1) if (r // BF16_SUBLANES) % k == 0)
        tr = r // used
        row_blk = lambda i, c, last=used - 1: jnp.minimum(i * nc + c, last)
        cast_in_specs.append(pl.BlockSpec((1, tr, cc), lambda i, c, f=row_blk, l=layer: (l, f(i, c), 0)))
        cast_out_specs.append(pl.BlockSpec((1, tr, cc), lambda i, c, f=row_blk: (0, f(i, c), 0)))
        cast_shapes.append(jax.ShapeDtypeStruct((1, r, cc), BF16))
        cast_steps.append(used)
    kern = functools.partial(_rwkv_kernel, chunk=chunk, rows=rows, n_steps=n_steps, cast_steps=tuple(cast_steps))

    outs = pl.pallas_call(
        kern,
        grid=(nb, nc),
        in_specs=[
            pl.BlockSpec((rows, chunk, wr), lambda i, c: (i, c, COL_RKV // wr)),
            pl.BlockSpec((rows, 8, wr), lambda i, c: (i, jnp.maximum(c * rows8 - 1, 0), COL_RKV // wr)),
            pl.BlockSpec((rows, chunk, D_LORA_PACK), lambda i, c: (i, c, COL_LORA // D_LORA_PACK)),
            pl.BlockSpec((rows, 8, D_LORA_PACK),
                         lambda i, c: (i, jnp.maximum(c * rows8 - 1, 0), COL_LORA // D_LORA_PACK)),
            vec(wr), vec(D_LORA_PACK),
            full(prm["w_up"]), vec(D_RWKV), full(prm["a_up"]), vec(D_RWKV), full(prm["g_up"]),
            vec(D_RWKV), vec(D_RWKV), vec(D_RWKV), vec(D_RWKV), vec(D_RWKV),
            full(tri), full(head_ones),
        ] + cast_in_specs,
        out_specs=[pl.BlockSpec((rows, chunk, D_RWKV), lambda i, c: (i, c, 0))] + cast_out_specs,
        out_shape=[jax.ShapeDtypeStruct((b, s, D_RWKV), BF16)] + cast_shapes,
        scratch_shapes=[
            pltpu.VMEM((rows * npair, LANES, LANES), F32),
            pltpu.VMEM((rows, chunk + 8, wr + D_LORA_PACK), F32),
        ],
        compiler_params=pltpu.CompilerParams(
            dimension_semantics=("arbitrary", "arbitrary"), vmem_limit_bytes=VMEM_LIMIT),
        name="rwkv7",
    )(z3, z3, z3, z3, prm["mu_r"], prm["mu_l"], prm["w_up"], prm["w0"], prm["a_up"], prm["a0"],
      prm["g_up"], prm["k_k"], prm["k_a"], prm["r_k"], prm["gn_w"], prm["gn_b"],
      tri, head_ones, *[w for w, _ in cast])
    return outs[0], list(outs[1:])


def _pad_cols(w, n):
    return jnp.pad(w, ((0, 0),) * (w.ndim - 1) + ((0, n - w.shape[-1]),))


def _pack_w_in_t(wt):
    o = 3 * D_ATT + 3 * D_RWKV
    o2 = o + RWKV_DECAY_LORA
    o3 = o2 + RWKV_AAA_LORA
    pad = ((0, 0), (0, LORA_PAD - RWKV_DECAY_LORA), (0, 0))
    return jnp.concatenate([wt[:, :o], jnp.pad(wt[:, o:o2], pad), jnp.pad(wt[:, o2:o3], pad), wt[:, o3:]], axis=1)


def _pack_mu(mu):
    o = 3 * D_RWKV
    mu_l = jnp.concatenate([
        jnp.pad(mu[o:o + RWKV_DECAY_LORA], (0, LORA_PAD - RWKV_DECAY_LORA)),
        jnp.pad(mu[o + RWKV_DECAY_LORA:o + RWKV_DECAY_LORA + RWKV_AAA_LORA], (0, LORA_PAD - RWKV_AAA_LORA)),
        mu[o + RWKV_DECAY_LORA + RWKV_AAA_LORA:],
    ])
    return mu[None, :o], mu_l[None, :]


def _pad_rows(w, n):
    return jnp.pad(w, ((0, n - w.shape[0]), (0, 0)))


def _rope_tables(seq):
    inv_freq = ROPE_THETA ** (-jnp.arange(0, HEAD_DIM, 2, dtype=F32) / HEAD_DIM)
    ang = jnp.arange(seq, dtype=F32)[:, None] * inv_freq[None, :]
    cos, sin = jnp.cos(ang), jnp.sin(ang)
    reps = LANES // HEAD_DIM
    cos_t = jnp.tile(jnp.concatenate([cos, cos], axis=1), (1, reps))
    sin_t = jnp.tile(jnp.concatenate([-sin, sin], axis=1), (1, reps))
    return cos_t, sin_t


def _tile(n, pref):
    t = min(n, pref)
    assert n % t == 0
    return t


def _row_tile(r, pref):
    return max(k for k in range(BF16_SUBLANES, pref + 1, BF16_SUBLANES) if r % k == 0)


def kernel(x, ffn1_norm_pre, ffn1_norm_post, ffn1_w_gate, ffn1_w_up, ffn1_w_down, mix_norm_pre, mix_norm_post, w_in, w_out, rwkv_mu, rwkv_w_up, rwkv_w0, rwkv_a_up, rwkv_a0, rwkv_g_up, rwkv_k_k, rwkv_k_a, rwkv_r_k, rwkv_gn_w, rwkv_gn_b, pool_w, pool_scale, ffn2_norm_pre, ffn2_norm_post, ffn2_w_gate, ffn2_w_up, ffn2_w_down):
    b, s, d = x.shape
    depth = w_in.shape[0]
    t = b * s
    d_ff = ffn1_w_gate.shape[2]
    assert s % RWKV_CHUNK == 0 and b % RWKV_ROWS == 0 and d == D_ATT + D_RWKV + D_POOL
    tm = _tile(t, 512)
    tm_ffn = _tile(t, 1024)
    tm_proj = _tile(t, 1024)
    tf = _tile(d_ff, 512)
    tn = _tile(D_PACK, D_PACK // 2)
    cos_t, sin_t = _rope_tables(s)
    x2 = x.reshape(t, d).astype(F32)

    w_in_t = jnp.swapaxes(w_in, 1, 2)
    big = (ffn1_w_gate, ffn1_w_up, ffn1_w_down, ffn2_w_gate, ffn2_w_up, ffn2_w_down, w_in_t, w_out)
    early = (0, 1, 2, 6)
    wb = {k: _cast_bf16(big[k], layer=0, tr=_row_tile(big[k].shape[1], 256)) for k in early}
    ffn_g = [[g[:, None, :] for g in gs] for gs in
             ((ffn1_norm_pre, ffn1_norm_post), (ffn2_norm_pre, ffn2_norm_post))]
    mix_g_pre, mix_g_post = mix_norm_pre[:, None, :], mix_norm_post[:, None, :]

    def gains(gs, i):
        return [g[i:i + 1] for g in gs]

    for i in range(depth):
        x2 = _ffn(x2, *gains(ffn_g[0], i), wb[0], wb[1], wb[2], layer=0, tm=tm_ffn, tf=tf)

        z = _proj(x2, mix_g_pre[i:i + 1], _pack_w_in_t(wb[6]), layer=0, tm=tm_proj, tn=tn)
        z3 = z.reshape(b, s, D_PACK)
        y_att = _attention(z3, cos_t, sin_t)
        mu_r, mu_l = _pack_mu(rwkv_mu[i])
        prm = dict(
            mu_r=mu_r, mu_l=mu_l,
            w_up=_pad_rows(rwkv_w_up[i], LORA_PAD).astype(BF16), w0=rwkv_w0[i][None],
            a_up=_pad_rows(rwkv_a_up[i], LORA_PAD).astype(BF16), a0=rwkv_a0[i][None],
            g_up=rwkv_g_up[i].astype(BF16), k_k=rwkv_k_k[i][None], k_a=rwkv_k_a[i][None],
            r_k=rwkv_r_k[i].reshape(1, D_RWKV), gn_w=rwkv_gn_w[i][None], gn_b=rwkv_gn_b[i][None])
        late = [k for k in range(len(big)) if k not in wb]
        nxt = list(range(len(big))) if i + 1 < depth else []
        y_rwkv, cast_out = _rwkv(z3, prm, rows=RWKV_ROWS,
                                 cast=[(big[k], i) for k in late] + [(big[k], i + 1) for k in nxt])
        wb.update(zip(late, cast_out[:len(late)]))
        wb_next = dict(zip(nxt, cast_out[len(late):]))
        y_pool = _pool(z3, pool_w[i].astype(BF16), pool_scale[i][None])

        x2 = _mixout(x2, y_att.reshape(t, D_ATT), y_rwkv.reshape(t, D_RWKV), y_pool.reshape(t, D_POOL),
                     wb[7], mix_g_post[i:i + 1], layer=0, tm=tm)

        x2 = _ffn(x2, *gains(ffn_g[1], i), wb[3], wb[4], wb[5], layer=0, tm=tm_ffn, tf=tf)
        wb = wb_next
    return x2.reshape(b, s, d).astype(x.dtype)
```

```python
import functools
import math

import jax
import jax.numpy as jnp
from jax import lax
from jax.experimental import pallas as pl
from jax.experimental.pallas import tpu as pltpu

F32 = jnp.float32
BF16 = jnp.bfloat16

HEAD_DIM = 64
N_ATT_HEADS = 12
N_RWKV_HEADS = 12
D_ATT = N_ATT_HEADS * HEAD_DIM
D_RWKV = N_RWKV_HEADS * HEAD_DIM
D_POOL = 512
N_POOL_GROUPS = 4
POOL_GROUP = D_POOL // N_POOL_GROUPS
POOL_WINDOWS = (2, 4, 8, 16)
DILATED_BRANCHES = ((128, 1), (512, 4), (2048, 16))
ROPE_THETA = 10000.0
RWKV_DECAY_LORA = 96
RWKV_AAA_LORA = 96
RWKV_GATE_LORA = 256
NORM_EPS = 1e-6
RWKV_GN_EPS = 64e-5

LANES = 128
LORA_PAD = 128
COL_ATT = 0
COL_RKV = 3 * D_ATT
COL_LORA = COL_RKV + 3 * D_RWKV
D_LORA_PACK = 2 * LORA_PAD + RWKV_GATE_LORA
COL_POOL = COL_LORA + D_LORA_PACK
D_PACK = COL_POOL + D_POOL

SEG_SUM_TERMS = 1
RWKV_ROWS = 4
RWKV_CHUNK = 64
ATT_KV_TILE = 512
ATT_GROUP = 2
VMEM_LIMIT = 56 * 1024 * 1024
MIXOUT_ROW_BLOCKS = 4
VMEM_LIMIT_FFN = 62 * 1024 * 1024

NEG_BIG = -1e30
LOG2_E = math.log2(math.e)


def _rms(xf, g):
    return xf * lax.rsqrt(jnp.mean(xf * xf, axis=-1, keepdims=True) + NORM_EPS) * g


def _dot(a, b):
    return jnp.dot(a, b, preferred_element_type=F32)


def _dot_nt(a, b):
    return lax.dot_general(a, b, (((1,), (1,)), ((), ())), preferred_element_type=F32)


def _ffn_kernel(x_ref, gpre_ref, gpost_ref, wg_ref, wu_ref, wd_ref, o_ref, h_ref, ssq_ref):
    j = pl.program_id(1)

    @pl.when(j == 0)
    def _():
        h_ref[...] = _rms(x_ref[...], gpre_ref[...]).astype(BF16)
        o_ref[...] = jnp.zeros_like(o_ref)

    half = o_ref.shape[0] // 2
    for r0 in (0, half):
        rows = slice(r0, r0 + half)
        h = h_ref[rows, :]
        g = _dot(h, wg_ref[...])
        u = _dot(h, wu_ref[...])
        a = (g * jax.nn.sigmoid(g) * u).astype(BF16)
        acc = o_ref[rows, :] + _dot(a, wd_ref[...])
        o_ref[rows, :] = acc
        ssq_ref[rows, :] = jnp.sum(acc * acc, axis=-1, keepdims=True)

    @pl.when(j == pl.num_programs(1) - 1)
    def _():
        inv = lax.rsqrt(ssq_ref[...] * (1.0 / o_ref.shape[1]) + NORM_EPS) * 0.5
        o_ref[...] = x_ref[...] + o_ref[...] * inv * gpost_ref[...]


def _ffn(x, g_pre, g_post, wg, wu, wd, *, layer, tm, tf):
    t, d = x.shape
    f = wg.shape[2]
    return pl.pallas_call(
        _ffn_kernel,
        grid=(t // tm, f // tf),
        in_specs=[
            pl.BlockSpec((tm, d), lambda i, j: (i, 0)),
            pl.BlockSpec((None, 1, d), lambda i, j: (layer, 0, 0)),
            pl.BlockSpec((None, 1, d), lambda i, j: (layer, 0, 0)),
            pl.BlockSpec((None, d, tf), lambda i, j: (layer, 0, j)),
            pl.BlockSpec((None, d, tf), lambda i, j: (layer, 0, j)),
            pl.BlockSpec((None, tf, d), lambda i, j: (layer, j, 0)),
        ],
        out_specs=pl.BlockSpec((tm, d), lambda i, j: (i, 0)),
        out_shape=jax.ShapeDtypeStruct((t, d), F32),
        scratch_shapes=[pltpu.VMEM((tm, d), BF16), pltpu.VMEM((tm, 1), F32)],
        compiler_params=pltpu.CompilerParams(
            dimension_semantics=("parallel", "arbitrary"), vmem_limit_bytes=VMEM_LIMIT_FFN),
        name="ffn",
    )(x, g_pre, g_post, wg, wu, wd)


def _proj_kernel(x_ref, g_ref, wt_ref, o_ref):
    o_ref[...] = _dot_nt(_rms(x_ref[...], g_ref[...]).astype(BF16), wt_ref[...])


def _proj(x, g, wt, *, layer, tm, tn):
    t, d = x.shape
    n = wt.shape[1]
    return pl.pallas_call(
        _proj_kernel,
        grid=(n // tn, t // tm),
        in_specs=[
            pl.BlockSpec((tm, d), lambda j, i: (i, 0)),
            pl.BlockSpec((None, 1, d), lambda j, i: (layer, 0, 0)),
            pl.BlockSpec((None, tn, d), lambda j, i: (layer, j, 0), pipeline_mode=pl.Buffered(1)),
        ],
        out_specs=pl.BlockSpec((tm, tn), lambda j, i: (i, j)),
        out_shape=jax.ShapeDtypeStruct((t, n), F32),
        compiler_params=pltpu.CompilerParams(
            dimension_semantics=("arbitrary", "arbitrary"), vmem_limit_bytes=VMEM_LIMIT_FFN),
        name="proj_in",
    )(x, g, wt)


def _mixout_kernel(x_ref, ya_ref, yr_ref, yp_ref, wa_ref, wr_ref, wp_ref, g_ref, o_ref):
    blk = o_ref.shape[0] // MIXOUT_ROW_BLOCKS
    for r0 in range(0, o_ref.shape[0], blk):
        rows = slice(r0, r0 + blk)
        mix = (_dot(ya_ref[rows, :], wa_ref[...]) + _dot(yr_ref[rows, :], wr_ref[...])
               + _dot(yp_ref[rows, :], wp_ref[...]))
        o_ref[rows, :] = x_ref[rows, :] + _rms(mix, g_ref[...])


def _mixout(x, ya, yr, yp, wo, g, *, layer, tm):
    t, d = x.shape
    da, dr, dp = ya.shape[1], yr.shape[1], yp.shape[1]
    assert da == dr and (da + dr) % dp == 0
    return pl.pallas_call(
        _mixout_kernel,
        grid=(t // tm,),
        in_specs=[
            pl.BlockSpec((tm, d), lambda i: (i, 0)),
            pl.BlockSpec((tm, da), lambda i: (i, 0)),
            pl.BlockSpec((tm, dr), lambda i: (i, 0)),
            pl.BlockSpec((tm, dp), lambda i: (i, 0)),
            pl.BlockSpec((None, da, d), lambda i: (layer, 0, 0), pipeline_mode=pl.Buffered(1)),
            pl.BlockSpec((None, dr, d), lambda i: (layer, 1, 0), pipeline_mode=pl.Buffered(1)),
            pl.BlockSpec((None, dp, d), lambda i: (layer, (da + dr) // dp, 0), pipeline_mode=pl.Buffered(1)),
            pl.BlockSpec((None, 1, d), lambda i: (layer, 0, 0)),
        ],
        out_specs=pl.BlockSpec((tm, d), lambda i: (i, 0)),
        out_shape=jax.ShapeDtypeStruct((t, d), F32),
        compiler_params=pltpu.CompilerParams(
            dimension_semantics=("parallel",), vmem_limit_bytes=VMEM_LIMIT_FFN),
        name="proj_out",
    )(x, ya, yr, yp, wo, wo, wo, g)


def _cast_kernel(w_ref, o_ref):
    o_ref[...] = w_ref[...].astype(BF16)


def _cast_bf16(w, *, layer, tr):
    _, r, c = w.shape
    return pl.pallas_call(
        _cast_kernel,
        grid=(r // tr,),
        in_specs=[pl.BlockSpec((1, tr, c), lambda j: (layer, j, 0))],
        out_specs=pl.BlockSpec((1, tr, c), lambda j: (0, j, 0)),
        out_shape=jax.ShapeDtypeStruct((1, r, c), BF16),
        compiler_params=pltpu.CompilerParams(
            dimension_semantics=("parallel",), vmem_limit_bytes=VMEM_LIMIT),
        name="cast_bf16",
    )(w)


def _att_bias_tile(base, tq, tk):
    row = lax.broadcasted_iota(jnp.int32, (tq, tk), 0)
    col = lax.broadcasted_iota(jnp.int32, (tq, tk), 1)
    delta = base + row - col
    cnt = jnp.zeros((tq, tk), jnp.int32)
    for window, dil in DILATED_BRANCHES:
        hit = (delta >= 0) & (delta <= window) & ((delta & (dil - 1)) == 0)
        cnt = cnt + hit.astype(jnp.int32)
    return jnp.where(cnt == 3, math.log2(3.0),
                     jnp.where(cnt == 2, 1.0, jnp.where(cnt == 1, 0.0, NEG_BIG))).astype(BF16)


def _att_kernel(q_ref, k_ref, v_ref, cos_ref, sin_ref, o_ref,
                qa_s, qb_s, k_s, va_s, vb_s, bias_s, *, seq, tq, tk, group):
    nq = seq // tq
    ratio = tk // tq

    @pl.when(pl.program_id(1) == 0)
    def _():
        for off in range(nq):
            bias_s[off] = _att_bias_tile(off * tq, tq, tk)

    lane = lax.broadcasted_iota(jnp.int32, (tq, LANES), 1)
    first_half = (lane & (HEAD_DIM - 1)) < (HEAD_DIM // 2)
    head_a = lane < HEAD_DIM

    def rope(x, cos, sin):
        swapped = jnp.where(first_half, pltpu.roll(x, LANES - HEAD_DIM // 2, 1), pltpu.roll(x, HEAD_DIM // 2, 1))
        return x * cos + swapped * sin

    staged_q, staged_kv = set(), set()

    def stage_q(i):
        if i in staged_q:
            return
        staged_q.add(i)
        rows = slice(i * tq, (i + 1) * tq)
        q = rope(q_ref[0, rows, :], cos_ref[rows, :], sin_ref[rows, :]) * (HEAD_DIM ** -0.5 * LOG2_E)
        qa_s[rows, :] = jnp.where(head_a, q, 0.0).astype(BF16)
        qb_s[rows, :] = jnp.where(head_a, 0.0, q).astype(BF16)

    def stage_kv(i):
        if i in staged_kv:
            return
        staged_kv.add(i)
        rows = slice(i * tq, (i + 1) * tq)
        k_s[rows, :] = rope(k_ref[0, rows, :], cos_ref[rows, :], sin_ref[rows, :]).astype(BF16)
        v = v_ref[0, rows, :]
        va_s[rows, :] = jnp.where(head_a, v, 1.0).astype(BF16)
        vb_s[rows, :] = jnp.where(head_a, 1.0, v).astype(BF16)

    def tile_max(s):
        m = s[:, 0:LANES]
        for t in range(1, s.shape[1] // LANES):
            m = jnp.maximum(m, s[:, t * LANES:(t + 1) * LANES])
        return m

    q_s = (qa_s, qb_s)
    v_s = (va_s, vb_s)
    m_run = {}
    acc = {}
    for kj in range(seq // tk):
        blocks = list(range(ratio * kj, nq))
        for g0 in range(0, len(blocks), group):
            chains = [(qi, hd) for qi in blocks[g0:g0 + group] for hd in range(2)]
            width = {qi: (tq if qi == ratio * kj else tk) for qi, _ in chains}
            for qi, _ in chains:
                stage_q(qi)
                for kb in range(ratio * kj, ratio * kj + width[qi] // tq):
                    stage_kv(kb)
            s = {}
            for qi, hd in chains:
                krows = slice(kj * tk, kj * tk + width[qi])
                s[qi, hd] = (_dot_nt(q_s[hd][qi * tq:(qi + 1) * tq, :], k_s[krows, :]).astype(BF16)
                             + bias_s[qi - ratio * kj, :, 0:width[qi]])
            p = {}
            for ch in chains:
                m_tile = jnp.max(tile_max(s[ch]).astype(F32), axis=-1, keepdims=True)
                if kj == 0:
                    m_new = m_tile
                else:
                    m_new = jnp.maximum(m_run[ch], m_tile)
                    acc[ch] = jnp.exp2(m_run[ch] - m_new) * acc[ch]
                m_run[ch] = m_new
                p[ch] = jnp.exp2(s[ch] - m_new.astype(BF16))
            for ch in chains:
                pv = _dot(p[ch], v_s[ch[1]][kj * tk:kj * tk + width[ch[0]], :])
                acc[ch] = pv if kj == 0 else acc[ch] + pv

    for qi in range(nq):
        o_a = acc[qi, 0] / pltpu.roll(acc[qi, 0], HEAD_DIM, 1)
        o_b = acc[qi, 1] / pltpu.roll(acc[qi, 1], HEAD_DIM, 1)
        o_ref[0, qi * tq:(qi + 1) * tq, :] = jnp.where(head_a, o_a, o_b).astype(o_ref.dtype)


def _attention(z3, cos_t, sin_t):
    b, s, _ = z3.shape
    tk = min(ATT_KV_TILE, s)
    tq = tk // 2
    npair = D_ATT // LANES
    kern = functools.partial(_att_kernel, seq=s, tq=tq, tk=tk, group=ATT_GROUP)
    return pl.pallas_call(
        kern,
        grid=(b, npair),
        in_specs=[
            pl.BlockSpec((1, s, LANES), lambda i, p: (i, 0, COL_ATT // LANES + p)),
            pl.BlockSpec((1, s, LANES), lambda i, p: (i, 0, (COL_ATT + D_ATT) // LANES + p)),
            pl.BlockSpec((1, s, LANES), lambda i, p: (i, 0, (COL_ATT + 2 * D_ATT) // LANES + p)),
            pl.BlockSpec((s, LANES), lambda i, p: (0, 0)),
            pl.BlockSpec((s, LANES), lambda i, p: (0, 0)),
        ],
        out_specs=pl.BlockSpec((1, s, LANES), lambda i, p: (i, 0, p)),
        out_shape=jax.ShapeDtypeStruct((b, s, D_ATT), BF16),
        scratch_shapes=[pltpu.VMEM((s, LANES), BF16)] * 5 + [pltpu.VMEM((s // tq, tq, tk), BF16)],
        compiler_params=pltpu.CompilerParams(
            dimension_semantics=("parallel", "arbitrary"), vmem_limit_bytes=VMEM_LIMIT),
        name="dilated_attention",
    )(z3, z3, z3, cos_t, sin_t)


POOL_PAD = 16


def _pool_kernel(u_ref, w_ref, sc_ref, o_ref, pad_s, *, seq):
    t1 = (lax.broadcasted_iota(jnp.int32, (seq, POOL_GROUP), 0) + 1).astype(F32)
    pad_s[0:POOL_PAD, :] = jnp.zeros((POOL_PAD, POOL_GROUP), F32)
    for gi, w in enumerate(POOL_WINDOWS):
        cols = slice(gi * POOL_GROUP, (gi + 1) * POOL_GROUP)
        u = u_ref[0, :, cols]
        s = u
        k = 1
        while k < w:
            pad_s[POOL_PAD:POOL_PAD + seq, :] = s
            s = s + pad_s[POOL_PAD - k:POOL_PAD - k + seq, :]
            k *= 2
        p = s / jnp.minimum(t1, float(w)) - u
        y = _dot(p.astype(BF16), w_ref[gi])
        o_ref[0, :, cols] = (y * sc_ref[:, cols]).astype(o_ref.dtype)


def _pool(z3, pool_w, pool_scale):
    b, s, _ = z3.shape
    kern = functools.partial(_pool_kernel, seq=s)
    return pl.pallas_call(
        kern,
        grid=(b,),
        in_specs=[
            pl.BlockSpec((1, s, D_POOL), lambda i: (i, 0, COL_POOL // D_POOL)),
            pl.BlockSpec(pool_w.shape, lambda i: (0, 0, 0)),
            pl.BlockSpec((1, D_POOL), lambda i: (0, 0)),
        ],
        out_specs=pl.BlockSpec((1, s, D_POOL), lambda i: (i, 0, 0)),
        out_shape=jax.ShapeDtypeStruct((b, s, D_POOL), BF16),
        scratch_shapes=[pltpu.VMEM((s + POOL_PAD, POOL_GROUP), F32)],
        compiler_params=pltpu.CompilerParams(
            dimension_semantics=("parallel",), vmem_limit_bytes=VMEM_LIMIT),
        name="pool_mixer",
    )(z3, pool_w, pool_scale)


def _split_dot(x, ones_b, terms):
    acc = None
    rem = x
    for _ in range(terms):
        piece = rem.astype(BF16)
        part = _dot(piece, ones_b)
        acc = part if acc is None else acc + part
        rem = rem - piece.astype(F32)
    return acc


N_RWKV_IN = 18


def _rwkv_kernel(*refs, chunk, rows, n_steps, cast_steps):
    (zr_ref, zrp_ref, zl_ref, zlp_ref, mur_ref, mul_ref, wup_ref, w0_ref, aup_ref, a0_ref,
     gup_ref, kk_ref, ka_ref, rk_ref, gnw_ref, gnb_ref, tri_ref, ones_ref) = refs[:N_RWKV_IN]
    n_cast = len(cast_steps)
    cast_in = refs[N_RWKV_IN:N_RWKV_IN + n_cast]
    o_ref = refs[N_RWKV_IN + n_cast]
    cast_out = refs[N_RWKV_IN + n_cast + 1:N_RWKV_IN + 2 * n_cast + 1]
    h_s, sh_s = refs[N_RWKV_IN + 2 * n_cast + 1:]
    c = pl.program_id(1)
    npair = D_RWKV // LANES
    halo = 8
    wr = 3 * D_RWKV

    step = pl.program_id(0) * pl.num_programs(1) + c
    for src, dst, used in zip(cast_in, cast_out, cast_steps):
        if used == n_steps:
            dst[...] = src[...].astype(BF16)
        else:
            @pl.when(step < used)
            def _(src=src, dst=dst):
                dst[...] = src[...].astype(BF16)

    @pl.when(c == 0)
    def _():
        h_s[...] = jnp.zeros_like(h_s)

    row = lax.broadcasted_iota(jnp.int32, (LANES, LANES), 0)
    col = lax.broadcasted_iota(jnp.int32, (LANES, LANES), 1)
    lane = lax.broadcasted_iota(jnp.int32, (chunk, LANES), 1)
    head_a = lane < HEAD_DIM
    strict_lower = row > col
    lower = row >= col
    eye = (row == col).astype(F32)
    tri = tri_ref[...]
    first = c == 0

    def seg_sum(x):
        m = x.shape[0]
        tall = jnp.concatenate([x[:, p * LANES:(p + 1) * LANES] for p in range(npair)], axis=0)
        summed = _split_dot(tall, ones_ref[...], SEG_SUM_TERMS)
        return jnp.concatenate([summed[p * m:(p + 1) * m] for p in range(npair)], axis=1)

    def stack2(x):
        return jnp.concatenate([jnp.where(head_a, x, 0.0), jnp.where(head_a, 0.0, x)], axis=0)

    zr_rows, zl_rows = [], []
    for bi in range(rows):
        zr = zr_ref[bi]
        zl = zl_ref[bi]
        sh_s[bi, 0:halo, 0:wr] = jnp.where(first, 0.0, zrp_ref[bi])
        sh_s[bi, 0:halo, wr:wr + D_LORA_PACK] = jnp.where(first, 0.0, zlp_ref[bi])
        sh_s[bi, halo:halo + chunk, 0:wr] = zr
        sh_s[bi, halo:halo + chunk, wr:wr + D_LORA_PACK] = zl
        zr_prev = sh_s[bi, halo - 1:halo - 1 + chunk, 0:wr]
        zl_prev = sh_s[bi, halo - 1:halo - 1 + chunk, wr:wr + D_LORA_PACK]
        zr_rows.append(zr + (zr_prev - zr) * mur_ref[...])
        zl_rows.append(zl + (zl_prev - zl) * mul_ref[...])
    zr = jnp.concatenate(zr_rows, axis=0)
    zl = jnp.concatenate(zl_rows, axis=0)
    r = zr[:, 0:D_RWKV]
    k = zr[:, D_RWKV:2 * D_RWKV]
    v = zr[:, 2 * D_RWKV:3 * D_RWKV]
    zw = zl[:, 0:LORA_PAD]
    za = zl[:, LORA_PAD:2 * LORA_PAD]
    zg = zl[:, 2 * LORA_PAD:D_LORA_PACK]

    lw = w0_ref[...] + _dot(jnp.tanh(zw).astype(BF16), wup_ref[...])
    ld = -math.exp(-0.5) * jax.nn.sigmoid(lw)
    a = jax.nn.sigmoid(a0_ref[...] + _dot(za.astype(BF16), aup_ref[...]))
    g = _dot(jax.nn.sigmoid(zg).astype(BF16), gup_ref[...])
    kk = k * kk_ref[...]
    k2 = k * (1.0 + (a - 1.0) * ka_ref[...])
    kk = kk / jnp.maximum(jnp.sqrt(seg_sum(kk * kk)), 1e-12)
    bonus = seg_sum(r * k2 * rk_ref[...]) * v
    bb = kk * a

    cum_rows = [_split_dot_left(tri, ld[bi * chunk:(bi + 1) * chunk], 3) for bi in range(rows)]
    cum = jnp.concatenate(cum_rows, axis=0)
    cum_end = jnp.concatenate([jnp.broadcast_to(cr[chunk - 1:chunk, :], cr.shape) for cr in cum_rows], axis=0)
    e_inv = jnp.exp(-cum)
    e_end = jnp.exp(cum_end - cum)
    tok_all = dict(
        r_t=r * jnp.exp(cum), kk_t=kk * jnp.exp(cum - ld), b_h=bb * e_inv, k_h=k2 * e_inv,
        b_a=bb * e_end, k_a=k2 * e_end, v=v)
    tok = [{name: val[bi * chunk:(bi + 1) * chunk] for name, val in tok_all.items()} for bi in range(rows)]
    for bi in range(rows):
        tok[bi]["a_end"] = jnp.exp(cum_rows[bi][chunk - 1:chunk, :])

    units = [(bi, p) for bi in range(rows) for p in range(npair)]

    def pair(name):
        return [tok[bi][name][:, p * LANES:(p + 1) * LANES] for bi, p in units]

    kks = [stack2(x).astype(BF16) for x in pair("kk_t")]
    bs = [stack2(x).astype(BF16) for x in pair("b_h")]
    ks = [stack2(x).astype(BF16) for x in pair("k_h")]
    rs = [stack2(x).astype(BF16) for x in pair("r_t")]
    vs = [stack2(x).astype(BF16) for x in pair("v")]
    bat = [stack2(x).T.astype(BF16) for x in pair("b_a")]
    kat = [stack2(x).T.astype(BF16) for x in pair("k_a")]
    a_col = [jnp.broadcast_to(x, (LANES, LANES)).T for x in pair("a_end")]
    ids = range(len(units))

    tt = [_dot_nt(jnp.concatenate([kks[i], rs[i]], axis=0), jnp.concatenate([bs[i], ks[i]], axis=0)) for i in ids]
    l_b = [jnp.where(strict_lower, tt[i][0:LANES, 0:LANES], 0.0) for i in ids]
    l_k = [jnp.where(strict_lower, tt[i][0:LANES, LANES:2 * LANES], 0.0).astype(BF16) for i in ids]
    m_b = [jnp.where(lower, tt[i][LANES:2 * LANES, 0:LANES], 0.0).astype(BF16) for i in ids]
    m_k = [jnp.where(lower, tt[i][LANES:2 * LANES, LANES:2 * LANES], 0.0).astype(BF16) for i in ids]

    sel = ((row ^ col) < 2) & ((row & 1) != 0) & ((col & 1) == 0)
    t_inv = [eye - jnp.where(sel, l_b[i], 0.0) for i in ids]
    s = 2
    while s < HEAD_DIM:
        sel = ((row ^ col) < 2 * s) & ((row & s) != 0) & ((col & s) == 0)
        t_b = [t_inv[i].astype(BF16) for i in ids]
        inner = [_dot(jnp.where(sel, l_b[i], 0.0).astype(BF16), t_b[i]).astype(BF16) for i in ids]
        t_inv = [t_inv[i] - _dot(t_b[i], inner[i]) for i in ids]
        s *= 2
    t_b = [t_inv[i].astype(BF16) for i in ids]

    mlv = [_dot(jnp.concatenate([m_k[i], l_k[i]], axis=0), vs[i]) for i in ids]
    y0 = [mlv[i][0:LANES, :] for i in ids]
    lkv = [mlv[i][LANES:2 * LANES, :].astype(BF16) for i in ids]
    w_m = [_dot(t_b[i], kks[i]).astype(BF16) for i in ids]

    h = [h_s[i] for i in ids]
    h_b = [h[i].astype(BF16) for i in ids]
    u_b = [(-_dot(jnp.concatenate([w_m[i], t_b[i]], axis=1),
                  jnp.concatenate([h_b[i], lkv[i]], axis=0))).astype(BF16) for i in ids]
    y = [_dot(jnp.concatenate([rs[i], m_b[i]], axis=1), jnp.concatenate([h_b[i], u_b[i]], axis=0)) + y0[i]
         for i in ids]
    for i in ids:
        h_s[i] = a_col[i] * h[i] + _dot(jnp.concatenate([bat[i], kat[i]], axis=1),
                                        jnp.concatenate([u_b[i], vs[i]], axis=0))

    yb = jnp.concatenate(
        [jnp.concatenate([y[bi * npair + p][0:chunk, :] + y[bi * npair + p][chunk:2 * chunk, :]
                          for p in range(npair)], axis=1) for bi in range(rows)], axis=0)
    mean = seg_sum(yb) * (1.0 / HEAD_DIM)
    d = yb - mean
    var = seg_sum(d * d) * (1.0 / HEAD_DIM)
    yn = d * lax.rsqrt(var + RWKV_GN_EPS) * gnw_ref[...] + gnb_ref[...]
    out = ((yn + bonus) * g).astype(o_ref.dtype)
    for bi in range(rows):
        o_ref[bi] = out[bi * chunk:(bi + 1) * chunk]


def _split_dot_left(tri_b, x, terms):
    acc = None
    rem = x
    for _ in range(terms):
        piece = rem.astype(BF16)
        part = _dot(tri_b, piece)
        acc = part if acc is None else acc + part
        rem = rem - piece.astype(F32)
    return acc


BF16_SUBLANES = 16


def _rwkv_ones(chunk):
    t = jnp.arange(chunk)
    tri = (t[:, None] >= t[None, :]).astype(BF16)
    r = jnp.arange(LANES)[:, None]
    c = jnp.arange(LANES)[None, :]
    head_ones = ((r < HEAD_DIM) == (c < HEAD_DIM)).astype(BF16)
    return tri, head_ones


def _rwkv(z3, prm, *, rows, cast=()):
    b, s, _ = z3.shape
    chunk = RWKV_CHUNK
    nc = s // chunk
    nb = b // rows
    wr = 3 * D_RWKV
    rows8 = chunk // 8
    npair = D_RWKV // LANES

    def vec(n):
        return pl.BlockSpec((1, n), lambda i, c: (0, 0))

    def full(a):
        return pl.BlockSpec(a.shape, lambda i, c: (0, 0))

    tri, head_ones = _rwkv_ones(chunk)

    n_steps = nb * nc
    cast_in_specs, cast_out_specs, cast_shapes, cast_steps = [], [], [], []
    for w, layer in cast:
        _, r, cc = w.shape
        assert r % BF16_SUBLANES == 0
        used = max(k for k in range(1, n_steps + 1) if (r // BF16_SUBLANES) % k == 0)
        tr = r // used
        row_blk = lambda i, c, last=used - 1: jnp.minimum(i * nc + c, last)
        cast_in_specs.append(pl.BlockSpec((1, tr, cc), lambda i, c, f=row_blk, l=layer: (l, f(i, c), 0)))
        cast_out_specs.append(pl.BlockSpec((1, tr, cc), lambda i, c, f=row_blk: (0, f(i, c), 0)))
        cast_shapes.append(jax.ShapeDtypeStruct((1, r, cc), BF16))
        cast_steps.append(used)
    kern = functools.partial(_rwkv_kernel, chunk=chunk, rows=rows, n_steps=n_steps, cast_steps=tuple(cast_steps))

    outs = pl.pallas_call(
        kern,
        grid=(nb, nc),
        in_specs=[
            pl.BlockSpec((rows, chunk, wr), lambda i, c: (i, c, COL_RKV // wr)),
            pl.BlockSpec((rows, 8, wr), lambda i, c: (i, jnp.maximum(c * rows8 - 1, 0), COL_RKV // wr)),
            pl.BlockSpec((rows, chunk, D_LORA_PACK), lambda i, c: (i, c, COL_LORA // D_LORA_PACK)),
            pl.BlockSpec((rows, 8, D_LORA_PACK),
                         lambda i, c: (i, jnp.maximum(c * rows8 - 1, 0), COL_LORA // D_LORA_PACK)),
            vec(wr), vec(D_LORA_PACK),
            full(prm["w_up"]), vec(D_RWKV), full(prm["a_up"]), vec(D_RWKV), full(prm["g_up"]),
            vec(D_RWKV), vec(D_RWKV), vec(D_RWKV), vec(D_RWKV), vec(D_RWKV),
            full(tri), full(head_ones),
        ] + cast_in_specs,
        out_specs=[pl.BlockSpec((rows, chunk, D_RWKV), lambda i, c: (i, c, 0))] + cast_out_specs,
        out_shape=[jax.ShapeDtypeStruct((b, s, D_RWKV), BF16)] + cast_shapes,
        scratch_shapes=[
            pltpu.VMEM((rows * npair, LANES, LANES), F32),
            pltpu.VMEM((rows, chunk + 8, wr + D_LORA_PACK), F32),
        ],
        compiler_params=pltpu.CompilerParams(
            dimension_semantics=("arbitrary", "arbitrary"), vmem_limit_bytes=VMEM_LIMIT),
        name="rwkv7",
    )(z3, z3, z3, z3, prm["mu_r"], prm["mu_l"], prm["w_up"], prm["w0"], prm["a_up"], prm["a0"],
      prm["g_up"], prm["k_k"], prm["k_a"], prm["r_k"], prm["gn_w"], prm["gn_b"],
      tri, head_ones, *[w for w, _ in cast])
    return outs[0], list(outs[1:])


def _pad_cols(w, n):
    return jnp.pad(w, ((0, 0),) * (w.ndim - 1) + ((0, n - w.shape[-1]),))


def _pack_w_in_t(wt):
    o = 3 * D_ATT + 3 * D_RWKV
    o2 = o + RWKV_DECAY_LORA
    o3 = o2 + RWKV_AAA_LORA
    pad = ((0, 0), (0, LORA_PAD - RWKV_DECAY_LORA), (0, 0))
    return jnp.concatenate([wt[:, :o], jnp.pad(wt[:, o:o2], pad), jnp.pad(wt[:, o2:o3], pad), wt[:, o3:]], axis=1)


def _pack_mu(mu):
    o = 3 * D_RWKV
    mu_l = jnp.concatenate([
        jnp.pad(mu[o:o + RWKV_DECAY_LORA], (0, LORA_PAD - RWKV_DECAY_LORA)),
        jnp.pad(mu[o + RWKV_DECAY_LORA:o + RWKV_DECAY_LORA + RWKV_AAA_LORA], (0, LORA_PAD - RWKV_AAA_LORA)),
        mu[o + RWKV_DECAY_LORA + RWKV_AAA_LORA:],
    ])
    return mu[None, :o], mu_l[None, :]


def _pad_rows(w, n):
    return jnp.pad(w, ((0, n - w.shape[0]), (0, 0)))


def _rope_tables(seq):
    inv_freq = ROPE_THETA ** (-jnp.arange(0, HEAD_DIM, 2, dtype=F32) / HEAD_DIM)
    ang = jnp.arange(seq, dtype=F32)[:, None] * inv_freq[None, :]
    cos, sin = jnp.cos(ang), jnp.sin(ang)
    reps = LANES // HEAD_DIM
    cos_t = jnp.tile(jnp.concatenate([cos, cos], axis=1), (1, reps))
    sin_t = jnp.tile(jnp.concatenate([-sin, sin], axis=1), (1, reps))
    return cos_t, sin_t


def _tile(n, pref):
    t = min(n, pref)
    assert n % t == 0
    return t


def _row_tile(r, pref):
    return max(k for k in range(BF16_SUBLANES, pref + 1, BF16_SUBLANES) if r % k == 0)


def kernel(x, ffn1_norm_pre, ffn1_norm_post, ffn1_w_gate, ffn1_w_up, ffn1_w_down, mix_norm_pre, mix_norm_post, w_in, w_out, rwkv_mu, rwkv_w_up, rwkv_w0, rwkv_a_up, rwkv_a0, rwkv_g_up, rwkv_k_k, rwkv_k_a, rwkv_r_k, rwkv_gn_w, rwkv_gn_b, pool_w, pool_scale, ffn2_norm_pre, ffn2_norm_post, ffn2_w_gate, ffn2_w_up, ffn2_w_down):
    b, s, d = x.shape
    depth = w_in.shape[0]
    t = b * s
    d_ff = ffn1_w_gate.shape[2]
    assert s % RWKV_CHUNK == 0 and b % RWKV_ROWS == 0 and d == D_ATT + D_RWKV + D_POOL
    tm = _tile(t, 1024)
    tm_ffn = _tile(t, 1024)
    tm_proj = _tile(t, 1024)
    tf = _tile(d_ff, 512)
    tn = _tile(D_PACK, D_PACK // 2)
    cos_t, sin_t = _rope_tables(s)
    x2 = x.reshape(t, d).astype(F32)

    w_in_t = jnp.swapaxes(w_in, 1, 2)
    big = (ffn1_w_gate, ffn1_w_up, ffn1_w_down, ffn2_w_gate, ffn2_w_up, ffn2_w_down, w_in_t, w_out)
    early = (0, 1, 2, 6)
    wb = {k: _cast_bf16(big[k], layer=0, tr=_row_tile(big[k].shape[1], 256)) for k in early}
    ffn_g = [[g[:, None, :] for g in gs] for gs in
             ((ffn1_norm_pre, ffn1_norm_post), (ffn2_norm_pre, ffn2_norm_post))]
    mix_g_pre, mix_g_post = mix_norm_pre[:, None, :], mix_norm_post[:, None, :]

    def gains(gs, i):
        return [g[i:i + 1] for g in gs]

    for i in range(depth):
        x2 = _ffn(x2, *gains(ffn_g[0], i), wb[0], wb[1], wb[2], layer=0, tm=tm_ffn, tf=tf)

        z = _proj(x2, mix_g_pre[i:i + 1], _pack_w_in_t(wb[6]), layer=0, tm=tm_proj, tn=tn)
        z3 = z.reshape(b, s, D_PACK)
        y_att = _attention(z3, cos_t, sin_t)
        mu_r, mu_l = _pack_mu(rwkv_mu[i])
        prm = dict(
            mu_r=mu_r, mu_l=mu_l,
            w_up=_pad_rows(rwkv_w_up[i], LORA_PAD).astype(BF16), w0=rwkv_w0[i][None],
            a_up=_pad_rows(rwkv_a_up[i], LORA_PAD).astype(BF16), a0=rwkv_a0[i][None],
            g_up=rwkv_g_up[i].astype(BF16), k_k=rwkv_k_k[i][None], k_a=rwkv_k_a[i][None],
            r_k=rwkv_r_k[i].reshape(1, D_RWKV), gn_w=rwkv_gn_w[i][None], gn_b=rwkv_gn_b[i][None])
        late = [k for k in range(len(big)) if k not in wb]
        nxt = list(range(len(big))) if i + 1 < depth else []
        y_rwkv, cast_out = _rwkv(z3, prm, rows=RWKV_ROWS,
                                 cast=[(big[k], i) for k in late] + [(big[k], i + 1) for k in nxt])
        wb.update(zip(late, cast_out[:len(late)]))
        wb_next = dict(zip(nxt, cast_out[len(late):]))
        y_pool = _pool(z3, pool_w[i].astype(BF16), pool_scale[i][None])

        x2 = _mixout(x2, y_att.reshape(t, D_ATT), y_rwkv.reshape(t, D_RWKV), y_pool.reshape(t, D_POOL),
                     wb[7], mix_g_post[i:i + 1], layer=0, tm=tm)

        x2 = _ffn(x2, *gains(ffn_g[1], i), wb[3], wb[4], wb[5], layer=0, tm=tm_ffn, tf=tf)
        wb = wb_next
    return x2.reshape(b, s, d).astype(x.dtype)
```

```python
import functools
import math

import jax
import jax.numpy as jnp
from jax import lax
from jax.experimental import pallas as pl
from jax.experimental.pallas import tpu as pltpu

F32 = jnp.float32
BF16 = jnp.bfloat16

HEAD_DIM = 64
N_ATT_HEADS = 12
N_RWKV_HEADS = 12
D_ATT = N_ATT_HEADS * HEAD_DIM
D_RWKV = N_RWKV_HEADS * HEAD_DIM
D_POOL = 512
N_POOL_GROUPS = 4
POOL_GROUP = D_POOL // N_POOL_GROUPS
POOL_WINDOWS = (2, 4, 8, 16)
DILATED_BRANCHES = ((128, 1), (512, 4), (2048, 16))
ROPE_THETA = 10000.0
RWKV_DECAY_LORA = 96
RWKV_AAA_LORA = 96
RWKV_GATE_LORA = 256
NORM_EPS = 1e-6
RWKV_GN_EPS = 64e-5

LANES = 128
F32_SUBLANES = 8
LORA_PAD = 128
COL_ATT = 0
COL_RKV = 3 * D_ATT
COL_LORA = COL_RKV + 3 * D_RWKV
D_LORA_PACK = 2 * LORA_PAD + RWKV_GATE_LORA
COL_POOL = COL_LORA + D_LORA_PACK
D_PACK = COL_POOL + D_POOL

SEG_SUM_TERMS = 1
RWKV_ROWS = 4
RWKV_CHUNK = 64
ATT_KV_TILE = 512
ATT_GROUP = 2
FFN_TM, FFN_TF = 1024, 512
PROJ_TM, PROJ_COL_BLOCKS = 1024, 2
MIXOUT_TM = 512
CAST_ROWS = 256
V7X_VMEM_BYTES = 64 * 1024 * 1024
VMEM_LIMIT = V7X_VMEM_BYTES - 8 * 1024 * 1024
VMEM_LIMIT_BIG = V7X_VMEM_BYTES - 2 * 1024 * 1024

NEG_BIG = -1e30
LOG2_E = math.log2(math.e)


def _rms(xf, g):
    return xf * lax.rsqrt(jnp.mean(xf * xf, axis=-1, keepdims=True) + NORM_EPS) * g


def _dot(a, b):
    return jnp.dot(a, b, preferred_element_type=F32)


def _dot_nt(a, b):
    return lax.dot_general(a, b, (((1,), (1,)), ((), ())), preferred_element_type=F32)


def _ffn_kernel(x_ref, gpre_ref, gpost_ref, wg_ref, wu_ref, wd_ref, o_ref, h_ref, ssq_ref):
    j = pl.program_id(1)

    @pl.when(j == 0)
    def _():
        h_ref[...] = _rms(x_ref[...], gpre_ref[...]).astype(BF16)
        o_ref[...] = jnp.zeros_like(o_ref)

    half = o_ref.shape[0] // 2
    for r0 in (0, half):
        rows = slice(r0, r0 + half)
        h = h_ref[rows, :]
        g = _dot(h, wg_ref[...])
        u = _dot(h, wu_ref[...])
        a = (g * jax.nn.sigmoid(g) * u).astype(BF16)
        acc = o_ref[rows, :] + _dot(a, wd_ref[...])
        o_ref[rows, :] = acc
        ssq_ref[rows, :] = jnp.sum(acc * acc, axis=-1, keepdims=True)

    @pl.when(j == pl.num_programs(1) - 1)
    def _():
        inv = lax.rsqrt(ssq_ref[...] * (1.0 / o_ref.shape[1]) + NORM_EPS) * 0.5
        o_ref[...] = x_ref[...] + o_ref[...] * inv * gpost_ref[...]


def _ffn(x, g_pre, g_post, wg, wu, wd, *, layer, tm, tf):
    t, d = x.shape
    f = wg.shape[2]
    return pl.pallas_call(
        _ffn_kernel,
        grid=(t // tm, f // tf),
        in_specs=[
            pl.BlockSpec((tm, d), lambda i, j: (i, 0)),
            pl.BlockSpec((None, 1, d), lambda i, j: (layer, 0, 0)),
            pl.BlockSpec((None, 1, d), lambda i, j: (layer, 0, 0)),
            pl.BlockSpec((None, d, tf), lambda i, j: (0, 0, j)),
            pl.BlockSpec((None, d, tf), lambda i, j: (0, 0, j)),
            pl.BlockSpec((None, tf, d), lambda i, j: (0, j, 0)),
        ],
        out_specs=pl.BlockSpec((tm, d), lambda i, j: (i, 0)),
        out_shape=jax.ShapeDtypeStruct((t, d), F32),
        scratch_shapes=[pltpu.VMEM((tm, d), BF16), pltpu.VMEM((tm, 1), F32)],
        compiler_params=pltpu.CompilerParams(
            dimension_semantics=("parallel", "arbitrary"), vmem_limit_bytes=VMEM_LIMIT_BIG),
        name="ffn",
    )(x, g_pre, g_post, wg, wu, wd)


def _proj_kernel(x_ref, g_ref, wt_ref, o_ref):
    o_ref[...] = _dot_nt(_rms(x_ref[...], g_ref[...]).astype(BF16), wt_ref[...])


def _proj(x, g, wt, *, layer, tm, tn):
    t, d = x.shape
    n = wt.shape[1]
    return pl.pallas_call(
        _proj_kernel,
        grid=(n // tn, t // tm),
        in_specs=[
            pl.BlockSpec((tm, d), lambda j, i: (i, 0)),
            pl.BlockSpec((None, 1, d), lambda j, i: (layer, 0, 0)),
            pl.BlockSpec((None, tn, d), lambda j, i: (0, j, 0), pipeline_mode=pl.Buffered(1)),
        ],
        out_specs=pl.BlockSpec((tm, tn), lambda j, i: (i, j)),
        out_shape=jax.ShapeDtypeStruct((t, n), F32),
        compiler_params=pltpu.CompilerParams(
            dimension_semantics=("arbitrary", "arbitrary"), vmem_limit_bytes=VMEM_LIMIT_BIG),
        name="proj_in",
    )(x, g, wt)


def _mixout_kernel(x_ref, ya_ref, yr_ref, yp_ref, wa_ref, wr_ref, wp_ref, g_ref, o_ref):
    mix = _dot(ya_ref[...], wa_ref[...]) + _dot(yr_ref[...], wr_ref[...]) + _dot(yp_ref[...], wp_ref[...])
    o_ref[...] = x_ref[...] + _rms(mix, g_ref[...])


def _mixout(x, ya, yr, yp, wo, g, *, layer, tm):
    t, d = x.shape
    da, dr, dp = ya.shape[1], yr.shape[1], yp.shape[1]
    assert da == dr and (da + dr) % dp == 0
    return pl.pallas_call(
        _mixout_kernel,
        grid=(t // tm,),
        in_specs=[
            pl.BlockSpec((tm, d), lambda i: (i, 0)),
            pl.BlockSpec((tm, da), lambda i: (i, 0)),
            pl.BlockSpec((tm, dr), lambda i: (i, 0)),
            pl.BlockSpec((tm, dp), lambda i: (i, 0)),
            pl.BlockSpec((None, da, d), lambda i: (0, 0, 0)),
            pl.BlockSpec((None, dr, d), lambda i: (0, 1, 0)),
            pl.BlockSpec((None, dp, d), lambda i: (0, (da + dr) // dp, 0)),
            pl.BlockSpec((None, 1, d), lambda i: (layer, 0, 0)),
        ],
        out_specs=pl.BlockSpec((tm, d), lambda i: (i, 0)),
        out_shape=jax.ShapeDtypeStruct((t, d), F32),
        compiler_params=pltpu.CompilerParams(
            dimension_semantics=("parallel",), vmem_limit_bytes=VMEM_LIMIT),
        name="proj_out",
    )(x, ya, yr, yp, wo, wo, wo, g)


def _cast_kernel(w_ref, o_ref):
    o_ref[...] = w_ref[...].astype(BF16)


def _cast_bf16(w, *, layer, tr):
    _, r, c = w.shape
    return pl.pallas_call(
        _cast_kernel,
        grid=(r // tr,),
        in_specs=[pl.BlockSpec((1, tr, c), lambda j: (layer, j, 0))],
        out_specs=pl.BlockSpec((1, tr, c), lambda j: (0, j, 0)),
        out_shape=jax.ShapeDtypeStruct((1, r, c), BF16),
        compiler_params=pltpu.CompilerParams(
            dimension_semantics=("parallel",), vmem_limit_bytes=VMEM_LIMIT),
        name="cast_bf16",
    )(w)


def _att_bias_tile(base, tq, tk):
    row = lax.broadcasted_iota(jnp.int32, (tq, tk), 0)
    col = lax.broadcasted_iota(jnp.int32, (tq, tk), 1)
    delta = base + row - col
    cnt = jnp.zeros((tq, tk), jnp.int32)
    for window, dil in DILATED_BRANCHES:
        hit = (delta >= 0) & (delta <= window) & ((delta & (dil - 1)) == 0)
        cnt = cnt + hit.astype(jnp.int32)
    return jnp.where(cnt == 3, math.log2(3.0),
                     jnp.where(cnt == 2, 1.0, jnp.where(cnt == 1, 0.0, NEG_BIG))).astype(BF16)


def _att_kernel(q_ref, k_ref, v_ref, cos_ref, sin_ref, o_ref,
                qa_s, qb_s, k_s, va_s, vb_s, bias_s, *, seq, tq, tk, group):
    nq = seq // tq
    ratio = tk // tq

    @pl.when(pl.program_id(1) == 0)
    def _():
        for off in range(nq):
            bias_s[off] = _att_bias_tile(off * tq, tq, tk)

    lane = lax.broadcasted_iota(jnp.int32, (tq, LANES), 1)
    first_half = (lane & (HEAD_DIM - 1)) < (HEAD_DIM // 2)
    head_a = lane < HEAD_DIM

    def rope(x, cos, sin):
        swapped = jnp.where(first_half, pltpu.roll(x, LANES - HEAD_DIM // 2, 1), pltpu.roll(x, HEAD_DIM // 2, 1))
        return x * cos + swapped * sin

    staged_q, staged_kv = set(), set()

    def stage_q(i):
        if i in staged_q:
            return
        staged_q.add(i)
        rows = slice(i * tq, (i + 1) * tq)
        q = rope(q_ref[0, rows, :], cos_ref[rows, :], sin_ref[rows, :]) * (HEAD_DIM ** -0.5 * LOG2_E)
        qa_s[rows, :] = jnp.where(head_a, q, 0.0).astype(BF16)
        qb_s[rows, :] = jnp.where(head_a, 0.0, q).astype(BF16)

    def stage_kv(i):
        if i in staged_kv:
            return
        staged_kv.add(i)
        rows = slice(i * tq, (i + 1) * tq)
        k_s[rows, :] = rope(k_ref[0, rows, :], cos_ref[rows, :], sin_ref[rows, :]).astype(BF16)
        v = v_ref[0, rows, :]
        va_s[rows, :] = jnp.where(head_a, v, 1.0).astype(BF16)
        vb_s[rows, :] = jnp.where(head_a, 1.0, v).astype(BF16)

    def tile_max(s):
        m = s[:, 0:LANES]
        for t in range(1, s.shape[1] // LANES):
            m = jnp.maximum(m, s[:, t * LANES:(t + 1) * LANES])
        return m

    q_s = (qa_s, qb_s)
    v_s = (va_s, vb_s)
    m_run = {}
    acc = {}
    for kj in range(seq // tk):
        blocks = list(range(ratio * kj, nq))
        for g0 in range(0, len(blocks), group):
            chains = [(qi, hd) for qi in blocks[g0:g0 + group] for hd in range(2)]
            width = {qi: (tq if qi == ratio * kj else tk) for qi, _ in chains}
            for qi, _ in chains:
                stage_q(qi)
                for kb in range(ratio * kj, ratio * kj + width[qi] // tq):
                    stage_kv(kb)
            s = {}
            for qi, hd in chains:
                krows = slice(kj * tk, kj * tk + width[qi])
                s[qi, hd] = (_dot_nt(q_s[hd][qi * tq:(qi + 1) * tq, :], k_s[krows, :]).astype(BF16)
                             + bias_s[qi - ratio * kj, :, 0:width[qi]])
            p = {}
            for ch in chains:
                m_tile = jnp.max(tile_max(s[ch]).astype(F32), axis=-1, keepdims=True)
                if kj == 0:
                    m_new = m_tile
                else:
                    m_new = jnp.maximum(m_run[ch], m_tile)
                    acc[ch] = jnp.exp2(m_run[ch] - m_new) * acc[ch]
                m_run[ch] = m_new
                p[ch] = jnp.exp2(s[ch] - m_new.astype(BF16))
            for ch in chains:
                pv = _dot(p[ch], v_s[ch[1]][kj * tk:kj * tk + width[ch[0]], :])
                acc[ch] = pv if kj == 0 else acc[ch] + pv

    for qi in range(nq):
        o_a = acc[qi, 0] / pltpu.roll(acc[qi, 0], HEAD_DIM, 1)
        o_b = acc[qi, 1] / pltpu.roll(acc[qi, 1], HEAD_DIM, 1)
        o_ref[0, qi * tq:(qi + 1) * tq, :] = jnp.where(head_a, o_a, o_b).astype(o_ref.dtype)


def _attention(z3, cos_t, sin_t):
    b, s, _ = z3.shape
    tk = min(ATT_KV_TILE, s)
    tq = tk // 2
    npair = D_ATT // LANES
    kern = functools.partial(_att_kernel, seq=s, tq=tq, tk=tk, group=ATT_GROUP)
    return pl.pallas_call(
        kern,
        grid=(b, npair),
        in_specs=[
            pl.BlockSpec((1, s, LANES), lambda i, p: (i, 0, COL_ATT // LANES + p)),
            pl.BlockSpec((1, s, LANES), lambda i, p: (i, 0, (COL_ATT + D_ATT) // LANES + p)),
            pl.BlockSpec((1, s, LANES), lambda i, p: (i, 0, (COL_ATT + 2 * D_ATT) // LANES + p)),
            pl.BlockSpec((s, LANES), lambda i, p: (0, 0)),
            pl.BlockSpec((s, LANES), lambda i, p: (0, 0)),
        ],
        out_specs=pl.BlockSpec((1, s, LANES), lambda i, p: (i, 0, p)),
        out_shape=jax.ShapeDtypeStruct((b, s, D_ATT), BF16),
        scratch_shapes=[pltpu.VMEM((s, LANES), BF16)] * 5 + [pltpu.VMEM((s // tq, tq, tk), BF16)],
        compiler_params=pltpu.CompilerParams(
            dimension_semantics=("parallel", "arbitrary"), vmem_limit_bytes=VMEM_LIMIT),
        name="dilated_attention",
    )(z3, z3, z3, cos_t, sin_t)


POOL_PAD = max(POOL_WINDOWS)


def _pool_kernel(u_ref, w_ref, sc_ref, o_ref, pad_s, *, seq):
    t1 = (lax.broadcasted_iota(jnp.int32, (seq, POOL_GROUP), 0) + 1).astype(F32)
    pad_s[0:POOL_PAD, :] = jnp.zeros((POOL_PAD, POOL_GROUP), F32)
    for gi, w in enumerate(POOL_WINDOWS):
        cols = slice(gi * POOL_GROUP, (gi + 1) * POOL_GROUP)
        u = u_ref[0, :, cols]
        s = u
        k = 1
        while k < w:
            pad_s[POOL_PAD:POOL_PAD + seq, :] = s
            s = s + pad_s[POOL_PAD - k:POOL_PAD - k + seq, :]
            k *= 2
        p = s / jnp.minimum(t1, float(w)) - u
        y = _dot(p.astype(BF16), w_ref[gi])
        o_ref[0, :, cols] = (y * sc_ref[:, cols]).astype(o_ref.dtype)


def _pool(z3, pool_w, pool_scale):
    b, s, _ = z3.shape
    kern = functools.partial(_pool_kernel, seq=s)
    return pl.pallas_call(
        kern,
        grid=(b,),
        in_specs=[
            pl.BlockSpec((1, s, D_POOL), lambda i: (i, 0, COL_POOL // D_POOL)),
            pl.BlockSpec(pool_w.shape, lambda i: (0, 0, 0)),
            pl.BlockSpec((1, D_POOL), lambda i: (0, 0)),
        ],
        out_specs=pl.BlockSpec((1, s, D_POOL), lambda i: (i, 0, 0)),
        out_shape=jax.ShapeDtypeStruct((b, s, D_POOL), BF16),
        scratch_shapes=[pltpu.VMEM((s + POOL_PAD, POOL_GROUP), F32)],
        compiler_params=pltpu.CompilerParams(
            dimension_semantics=("parallel",), vmem_limit_bytes=VMEM_LIMIT),
        name="pool_mixer",
    )(z3, pool_w, pool_scale)


def _split_dot(x, ones_b, terms):
    acc = None
    rem = x
    for _ in range(terms):
        piece = rem.astype(BF16)
        part = _dot(piece, ones_b)
        acc = part if acc is None else acc + part
        rem = rem - piece.astype(F32)
    return acc


N_RWKV_IN = 18


def _rwkv_kernel(*refs, chunk, rows, n_steps, cast_steps):
    (zr_ref, zrp_ref, zl_ref, zlp_ref, mur_ref, mul_ref, wup_ref, w0_ref, aup_ref, a0_ref,
     gup_ref, kk_ref, ka_ref, rk_ref, gnw_ref, gnb_ref, tri_ref, ones_ref) = refs[:N_RWKV_IN]
    n_cast = len(cast_steps)
    cast_in = refs[N_RWKV_IN:N_RWKV_IN + n_cast]
    o_ref = refs[N_RWKV_IN + n_cast]
    cast_out = refs[N_RWKV_IN + n_cast + 1:N_RWKV_IN + 2 * n_cast + 1]
    h_s, sh_s = refs[N_RWKV_IN + 2 * n_cast + 1:]
    c = pl.program_id(1)
    npair = D_RWKV // LANES
    halo = F32_SUBLANES
    wr = 3 * D_RWKV

    step = pl.program_id(0) * pl.num_programs(1) + c
    for src, dst, used in zip(cast_in, cast_out, cast_steps):
        if used == n_steps:
            dst[...] = src[...].astype(BF16)
        else:
            @pl.when(step < used)
            def _(src=src, dst=dst):
                dst[...] = src[...].astype(BF16)

    @pl.when(c == 0)
    def _():
        h_s[...] = jnp.zeros_like(h_s)

    row = lax.broadcasted_iota(jnp.int32, (LANES, LANES), 0)
    col = lax.broadcasted_iota(jnp.int32, (LANES, LANES), 1)
    lane = lax.broadcasted_iota(jnp.int32, (chunk, LANES), 1)
    head_a = lane < HEAD_DIM
    strict_lower = row > col
    lower = row >= col
    eye = (row == col).astype(F32)
    tri = tri_ref[...]
    first = c == 0

    def seg_sum(x):
        m = x.shape[0]
        tall = jnp.concatenate([x[:, p * LANES:(p + 1) * LANES] for p in range(npair)], axis=0)
        summed = _split_dot(tall, ones_ref[...], SEG_SUM_TERMS)
        return jnp.concatenate([summed[p * m:(p + 1) * m] for p in range(npair)], axis=1)

    def stack2(x):
        return jnp.concatenate([jnp.where(head_a, x, 0.0), jnp.where(head_a, 0.0, x)], axis=0)

    zr_rows, zl_rows = [], []
    for bi in range(rows):
        zr = zr_ref[bi]
        zl = zl_ref[bi]
        sh_s[bi, 0:halo, 0:wr] = jnp.where(first, 0.0, zrp_ref[bi])
        sh_s[bi, 0:halo, wr:wr + D_LORA_PACK] = jnp.where(first, 0.0, zlp_ref[bi])
        sh_s[bi, halo:halo + chunk, 0:wr] = zr
        sh_s[bi, halo:halo + chunk, wr:wr + D_LORA_PACK] = zl
        zr_prev = sh_s[bi, halo - 1:halo - 1 + chunk, 0:wr]
        zl_prev = sh_s[bi, halo - 1:halo - 1 + chunk, wr:wr + D_LORA_PACK]
        zr_rows.append(zr + (zr_prev - zr) * mur_ref[...])
        zl_rows.append(zl + (zl_prev - zl) * mul_ref[...])
    zr = jnp.concatenate(zr_rows, axis=0)
    zl = jnp.concatenate(zl_rows, axis=0)
    r = zr[:, 0:D_RWKV]
    k = zr[:, D_RWKV:2 * D_RWKV]
    v = zr[:, 2 * D_RWKV:3 * D_RWKV]
    zw = zl[:, 0:LORA_PAD]
    za = zl[:, LORA_PAD:2 * LORA_PAD]
    zg = zl[:, 2 * LORA_PAD:D_LORA_PACK]

    lw = w0_ref[...] + _dot(jnp.tanh(zw).astype(BF16), wup_ref[...])
    ld = -math.exp(-0.5) * jax.nn.sigmoid(lw)
    a = jax.nn.sigmoid(a0_ref[...] + _dot(za.astype(BF16), aup_ref[...]))
    g = _dot(jax.nn.sigmoid(zg).astype(BF16), gup_ref[...])
    kk = k * kk_ref[...]
    k2 = k * (1.0 + (a - 1.0) * ka_ref[...])
    kk = kk / jnp.maximum(jnp.sqrt(seg_sum(kk * kk)), 1e-12)
    bonus = seg_sum(r * k2 * rk_ref[...]) * v
    bb = kk * a

    cum_rows = [_split_dot_left(tri, ld[bi * chunk:(bi + 1) * chunk], 3) for bi in range(rows)]
    cum = jnp.concatenate(cum_rows, axis=0)
    cum_end = jnp.concatenate([jnp.broadcast_to(cr[chunk - 1:chunk, :], cr.shape) for cr in cum_rows], axis=0)
    e_inv = jnp.exp(-cum)
    e_end = jnp.exp(cum_end - cum)
    tok_all = dict(
        r_t=r * jnp.exp(cum), kk_t=kk * jnp.exp(cum - ld), b_h=bb * e_inv, k_h=k2 * e_inv,
        b_a=bb * e_end, k_a=k2 * e_end, v=v)
    tok = [{name: val[bi * chunk:(bi + 1) * chunk] for name, val in tok_all.items()} for bi in range(rows)]
    for bi in range(rows):
        tok[bi]["a_end"] = jnp.exp(cum_rows[bi][chunk - 1:chunk, :])

    units = [(bi, p) for bi in range(rows) for p in range(npair)]

    def pair(name):
        return [tok[bi][name][:, p * LANES:(p + 1) * LANES] for bi, p in units]

    kks = [stack2(x).astype(BF16) for x in pair("kk_t")]
    bs = [stack2(x).astype(BF16) for x in pair("b_h")]
    ks = [stack2(x).astype(BF16) for x in pair("k_h")]
    rs = [stack2(x).astype(BF16) for x in pair("r_t")]
    vs = [stack2(x).astype(BF16) for x in pair("v")]
    bat = [stack2(x).T.astype(BF16) for x in pair("b_a")]
    kat = [stack2(x).T.astype(BF16) for x in pair("k_a")]
    a_col = [jnp.broadcast_to(x, (LANES, LANES)).T for x in pair("a_end")]
    ids = range(len(units))

    tt = [_dot_nt(jnp.concatenate([kks[i], rs[i]], axis=0), jnp.concatenate([bs[i], ks[i]], axis=0)) for i in ids]
    l_b = [jnp.where(strict_lower, tt[i][0:LANES, 0:LANES], 0.0) for i in ids]
    l_k = [jnp.where(strict_lower, tt[i][0:LANES, LANES:2 * LANES], 0.0).astype(BF16) for i in ids]
    m_b = [jnp.where(lower, tt[i][LANES:2 * LANES, 0:LANES], 0.0).astype(BF16) for i in ids]
    m_k = [jnp.where(lower, tt[i][LANES:2 * LANES, LANES:2 * LANES], 0.0).astype(BF16) for i in ids]

    sel = ((row ^ col) < 2) & ((row & 1) != 0) & ((col & 1) == 0)
    t_inv = [eye - jnp.where(sel, l_b[i], 0.0) for i in ids]
    s = 2
    while s < HEAD_DIM:
        sel = ((row ^ col) < 2 * s) & ((row & s) != 0) & ((col & s) == 0)
        t_b = [t_inv[i].astype(BF16) for i in ids]
        inner = [_dot(jnp.where(sel, l_b[i], 0.0).astype(BF16), t_b[i]).astype(BF16) for i in ids]
        t_inv = [t_inv[i] - _dot(t_b[i], inner[i]) for i in ids]
        s *= 2
    t_b = [t_inv[i].astype(BF16) for i in ids]

    mlv = [_dot(jnp.concatenate([m_k[i], l_k[i]], axis=0), vs[i]) for i in ids]
    y0 = [mlv[i][0:LANES, :] for i in ids]
    lkv = [mlv[i][LANES:2 * LANES, :].astype(BF16) for i in ids]
    w_m = [_dot(t_b[i], kks[i]).astype(BF16) for i in ids]

    h = [h_s[i] for i in ids]
    h_b = [h[i].astype(BF16) for i in ids]
    u_b = [(-_dot(jnp.concatenate([w_m[i], t_b[i]], axis=1),
                  jnp.concatenate([h_b[i], lkv[i]], axis=0))).astype(BF16) for i in ids]
    y = [_dot(jnp.concatenate([rs[i], m_b[i]], axis=1), jnp.concatenate([h_b[i], u_b[i]], axis=0)) + y0[i]
         for i in ids]
    for i in ids:
        h_s[i] = a_col[i] * h[i] + _dot(jnp.concatenate([bat[i], kat[i]], axis=1),
                                        jnp.concatenate([u_b[i], vs[i]], axis=0))

    yb = jnp.concatenate(
        [jnp.concatenate([y[bi * npair + p][0:chunk, :] + y[bi * npair + p][chunk:2 * chunk, :]
                          for p in range(npair)], axis=1) for bi in range(rows)], axis=0)
    mean = seg_sum(yb) * (1.0 / HEAD_DIM)
    d = yb - mean
    var = seg_sum(d * d) * (1.0 / HEAD_DIM)
    yn = d * lax.rsqrt(var + RWKV_GN_EPS) * gnw_ref[...] + gnb_ref[...]
    out = ((yn + bonus) * g).astype(o_ref.dtype)
    for bi in range(rows):
        o_ref[bi] = out[bi * chunk:(bi + 1) * chunk]


def _split_dot_left(tri_b, x, terms):
    acc = None
    rem = x
    for _ in range(terms):
        piece = rem.astype(BF16)
        part = _dot(tri_b, piece)
        acc = part if acc is None else acc + part
        rem = rem - piece.astype(F32)
    return acc


BF16_SUBLANES = 16


def _rwkv_ones(chunk):
    t = jnp.arange(chunk)
    tri = (t[:, None] >= t[None, :]).astype(BF16)
    r = jnp.arange(LANES)[:, None]
    c = jnp.arange(LANES)[None, :]
    head_ones = ((r < HEAD_DIM) == (c < HEAD_DIM)).astype(BF16)
    return tri, head_ones


def _rwkv(z3, prm, *, rows, cast=()):
    b, s, _ = z3.shape
    chunk = RWKV_CHUNK
    nc = s // chunk
    nb = b // rows
    wr = 3 * D_RWKV
    rows8 = chunk // F32_SUBLANES
    npair = D_RWKV // LANES

    def vec(n):
        return pl.BlockSpec((1, n), lambda i, c: (0, 0))

    def full(a):
        return pl.BlockSpec(a.shape, lambda i, c: (0, 0))

    tri, head_ones = _rwkv_ones(chunk)

    n_steps = nb * nc
    cast_in_specs, cast_out_specs, cast_shapes, cast_steps = [], [], [], []
    for w, layer in cast:
        _, r, cc = w.shape
        assert r % BF16_SUBLANES == 0
        used = max(k for k in range(1, n_steps + 1) if (r // BF16_SUBLANES) % k == 0)
        tr = r // used
        row_blk = lambda i, c, last=used - 1: jnp.minimum(i * nc + c, last)
        cast_in_specs.append(pl.BlockSpec((1, tr, cc), lambda i, c, f=row_blk, l=layer: (l, f(i, c), 0)))
        cast_out_specs.append(pl.BlockSpec((1, tr, cc), lambda i, c, f=row_blk: (0, f(i, c), 0)))
        cast_shapes.append(jax.ShapeDtypeStruct((1, r, cc), BF16))
        cast_steps.append(used)
    kern = functools.partial(_rwkv_kernel, chunk=chunk, rows=rows, n_steps=n_steps, cast_steps=tuple(cast_steps))

    outs = pl.pallas_call(
        kern,
        grid=(nb, nc),
        in_specs=[
            pl.BlockSpec((rows, chunk, wr), lambda i, c: (i, c, COL_RKV // wr)),
            pl.BlockSpec((rows, F32_SUBLANES, wr),
                         lambda i, c: (i, jnp.maximum(c * rows8 - 1, 0), COL_RKV // wr)),
            pl.BlockSpec((rows, chunk, D_LORA_PACK), lambda i, c: (i, c, COL_LORA // D_LORA_PACK)),
            pl.BlockSpec((rows, F32_SUBLANES, D_LORA_PACK),
                         lambda i, c: (i, jnp.maximum(c * rows8 - 1, 0), COL_LORA // D_LORA_PACK)),
            vec(wr), vec(D_LORA_PACK),
            full(prm["w_up"]), vec(D_RWKV), full(prm["a_up"]), vec(D_RWKV), full(prm["g_up"]),
            vec(D_RWKV), vec(D_RWKV), vec(D_RWKV), vec(D_RWKV), vec(D_RWKV),
            full(tri), full(head_ones),
        ] + cast_in_specs,
        out_specs=[pl.BlockSpec((rows, chunk, D_RWKV), lambda i, c: (i, c, 0))] + cast_out_specs,
        out_shape=[jax.ShapeDtypeStruct((b, s, D_RWKV), BF16)] + cast_shapes,
        scratch_shapes=[
            pltpu.VMEM((rows * npair, LANES, LANES), F32),
            pltpu.VMEM((rows, chunk + F32_SUBLANES, wr + D_LORA_PACK), F32),
        ],
        compiler_params=pltpu.CompilerParams(
            dimension_semantics=("arbitrary", "arbitrary"), vmem_limit_bytes=VMEM_LIMIT),
        name="rwkv7",
    )(z3, z3, z3, z3, prm["mu_r"], prm["mu_l"], prm["w_up"], prm["w0"], prm["a_up"], prm["a0"],
      prm["g_up"], prm["k_k"], prm["k_a"], prm["r_k"], prm["gn_w"], prm["gn_b"],
      tri, head_ones, *[w for w, _ in cast])
    return outs[0], list(outs[1:])


def _pack_w_in_t(wt):
    o = 3 * D_ATT + 3 * D_RWKV
    o2 = o + RWKV_DECAY_LORA
    o3 = o2 + RWKV_AAA_LORA
    pad = ((0, 0), (0, LORA_PAD - RWKV_DECAY_LORA), (0, 0))
    return jnp.concatenate([wt[:, :o], jnp.pad(wt[:, o:o2], pad), jnp.pad(wt[:, o2:o3], pad), wt[:, o3:]], axis=1)


def _pack_mu(mu):
    o = 3 * D_RWKV
    mu_l = jnp.concatenate([
        jnp.pad(mu[o:o + RWKV_DECAY_LORA], (0, LORA_PAD - RWKV_DECAY_LORA)),
        jnp.pad(mu[o + RWKV_DECAY_LORA:o + RWKV_DECAY_LORA + RWKV_AAA_LORA], (0, LORA_PAD - RWKV_AAA_LORA)),
        mu[o + RWKV_DECAY_LORA + RWKV_AAA_LORA:],
    ])
    return mu[None, :o], mu_l[None, :]


def _pad_rows(w, n):
    return jnp.pad(w, ((0, n - w.shape[0]), (0, 0)))


def _rope_tables(seq):
    inv_freq = ROPE_THETA ** (-jnp.arange(0, HEAD_DIM, 2, dtype=F32) / HEAD_DIM)
    ang = jnp.arange(seq, dtype=F32)[:, None] * inv_freq[None, :]
    cos, sin = jnp.cos(ang), jnp.sin(ang)
    reps = LANES // HEAD_DIM
    cos_t = jnp.tile(jnp.concatenate([cos, cos], axis=1), (1, reps))
    sin_t = jnp.tile(jnp.concatenate([-sin, sin], axis=1), (1, reps))
    return cos_t, sin_t


def _tile(n, pref):
    t = min(n, pref)
    assert n % t == 0
    return t


def _row_tile(r, pref):
    return max(k for k in range(BF16_SUBLANES, pref + 1, BF16_SUBLANES) if r % k == 0)


def kernel(x, ffn1_norm_pre, ffn1_norm_post, ffn1_w_gate, ffn1_w_up, ffn1_w_down, mix_norm_pre, mix_norm_post, w_in, w_out, rwkv_mu, rwkv_w_up, rwkv_w0, rwkv_a_up, rwkv_a0, rwkv_g_up, rwkv_k_k, rwkv_k_a, rwkv_r_k, rwkv_gn_w, rwkv_gn_b, pool_w, pool_scale, ffn2_norm_pre, ffn2_norm_post, ffn2_w_gate, ffn2_w_up, ffn2_w_down):
    b, s, d = x.shape
    depth = w_in.shape[0]
    t = b * s
    d_ff = ffn1_w_gate.shape[2]
    assert s % RWKV_CHUNK == 0 and b % RWKV_ROWS == 0 and d == D_ATT + D_RWKV + D_POOL
    tm_ffn, tf = _tile(t, FFN_TM), _tile(d_ff, FFN_TF)
    tm_proj, tn = _tile(t, PROJ_TM), D_PACK // PROJ_COL_BLOCKS
    tm_mix = _tile(t, MIXOUT_TM)
    cos_t, sin_t = _rope_tables(s)
    x2 = x.reshape(t, d).astype(F32)

    w_in_t = jnp.swapaxes(w_in, 1, 2)
    big = (ffn1_w_gate, ffn1_w_up, ffn1_w_down, ffn2_w_gate, ffn2_w_up, ffn2_w_down, w_in_t, w_out)
    early = (0, 1, 2, 6)
    wb = {k: _cast_bf16(big[k], layer=0, tr=_row_tile(big[k].shape[1], CAST_ROWS)) for k in early}
    ffn1_g = (ffn1_norm_pre[:, None, :], ffn1_norm_post[:, None, :])
    ffn2_g = (ffn2_norm_pre[:, None, :], ffn2_norm_post[:, None, :])
    mix_g_pre, mix_g_post = mix_norm_pre[:, None, :], mix_norm_post[:, None, :]

    for i in range(depth):
        x2 = _ffn(x2, *ffn1_g, wb[0], wb[1], wb[2], layer=i, tm=tm_ffn, tf=tf)

        z = _proj(x2, mix_g_pre, _pack_w_in_t(wb[6]), layer=i, tm=tm_proj, tn=tn)
        z3 = z.reshape(b, s, D_PACK)
        y_att = _attention(z3, cos_t, sin_t)
        mu_r, mu_l = _pack_mu(rwkv_mu[i])
        prm = dict(
            mu_r=mu_r, mu_l=mu_l,
            w_up=_pad_rows(rwkv_w_up[i], LORA_PAD).astype(BF16), w0=rwkv_w0[i][None],
            a_up=_pad_rows(rwkv_a_up[i], LORA_PAD).astype(BF16), a0=rwkv_a0[i][None],
            g_up=rwkv_g_up[i].astype(BF16), k_k=rwkv_k_k[i][None], k_a=rwkv_k_a[i][None],
            r_k=rwkv_r_k[i].reshape(1, D_RWKV), gn_w=rwkv_gn_w[i][None], gn_b=rwkv_gn_b[i][None])
        late = [k for k in range(len(big)) if k not in wb]
        nxt = list(range(len(big))) if i + 1 < depth else []
        y_rwkv, cast_out = _rwkv(z3, prm, rows=RWKV_ROWS,
                                 cast=[(big[k], i) for k in late] + [(big[k], i + 1) for k in nxt])
        wb.update(zip(late, cast_out[:len(late)]))
        wb_next = dict(zip(nxt, cast_out[len(late):]))
        y_pool = _pool(z3, pool_w[i].astype(BF16), pool_scale[i][None])

        x2 = _mixout(x2, y_att.reshape(t, D_ATT), y_rwkv.reshape(t, D_RWKV), y_pool.reshape(t, D_POOL),
                     wb[7], mix_g_post, layer=i, tm=tm_mix)

        x2 = _ffn(x2, *ffn2_g, wb[3], wb[4], wb[5], layer=i, tm=tm_ffn, tf=tf)
        wb = wb_next
    return x2.reshape(b, s, d).astype(x.dtype)
```

```python
import functools
import math

import jax
import jax.numpy as jnp
from jax import lax
from jax.experimental import pallas as pl
from jax.experimental.pallas import tpu as pltpu

F32 = jnp.float32
BF16 = jnp.bfloat16

HEAD_DIM = 64
N_ATT_HEADS = 12
N_RWKV_HEADS = 12
D_ATT = N_ATT_HEADS * HEAD_DIM
D_RWKV = N_RWKV_HEADS * HEAD_DIM
D_POOL = 512
N_POOL_GROUPS = 4
POOL_GROUP = D_POOL // N_POOL_GROUPS
POOL_WINDOWS = (2, 4, 8, 16)
DILATED_BRANCHES = ((128, 1), (512, 4), (2048, 16))
ROPE_THETA = 10000.0
RWKV_DECAY_LORA = 96
RWKV_AAA_LORA = 96
RWKV_GATE_LORA = 256
NORM_EPS = 1e-6
RWKV_GN_EPS = 64e-5

LANES = 128
F32_SUBLANES = 8
LORA_PAD = 128
COL_ATT = 0
COL_RKV = 3 * D_ATT
COL_LORA = COL_RKV + 3 * D_RWKV
D_LORA_PACK = 2 * LORA_PAD + RWKV_GATE_LORA
COL_POOL = COL_LORA + D_LORA_PACK
D_PACK = COL_POOL + D_POOL

SEG_SUM_TERMS = 1
RWKV_ROWS = 4
RWKV_CHUNK = 64
ATT_KV_TILE = 512
ATT_GROUP = 2
FFN_TM, FFN_TF = 1024, 512
PROJ_TM, PROJ_COL_BLOCKS = 1024, 2
MIXOUT_TM = 512
CAST_ROWS = 256
V7X_VMEM_BYTES = 64 * 1024 * 1024
VMEM_LIMIT = V7X_VMEM_BYTES - 8 * 1024 * 1024
VMEM_LIMIT_BIG = V7X_VMEM_BYTES - 2 * 1024 * 1024

NEG_BIG = -1e30
LOG2_E = math.log2(math.e)


def _rms(xf, g):
    return xf * lax.rsqrt(jnp.mean(xf * xf, axis=-1, keepdims=True) + NORM_EPS) * g


def _dot(a, b):
    return jnp.dot(a, b, preferred_element_type=F32)


def _dot_nt(a, b):
    return lax.dot_general(a, b, (((1,), (1,)), ((), ())), preferred_element_type=F32)


def _ffn_kernel(x_ref, gpre_ref, gpost_ref, wg_ref, wu_ref, wd_ref, o_ref, h_ref, ssq_ref):
    j = pl.program_id(1)

    @pl.when(j == 0)
    def _():
        h_ref[...] = _rms(x_ref[...], gpre_ref[...]).astype(BF16)
        o_ref[...] = jnp.zeros_like(o_ref)

    h = h_ref[...]
    g = _dot(h, wg_ref[...])
    u = _dot(h, wu_ref[...])
    a = (g * jax.nn.sigmoid(g) * u).astype(BF16)
    acc = o_ref[...] + _dot(a, wd_ref[...])
    o_ref[...] = acc
    sq = acc * acc
    part = sq[:, 0:LANES]
    for t in range(1, sq.shape[1] // LANES):
        part = part + sq[:, t * LANES:(t + 1) * LANES]
    ssq_ref[...] = part

    @pl.when(j == pl.num_programs(1) - 1)
    def _():
        ssq = jnp.sum(ssq_ref[...], axis=-1, keepdims=True)
        inv = lax.rsqrt(ssq * (1.0 / o_ref.shape[1]) + NORM_EPS) * 0.5
        o_ref[...] = x_ref[...] + o_ref[...] * inv * gpost_ref[...]


def _ffn(x, g_pre, g_post, wg, wu, wd, *, layer, tm, tf):
    t, d = x.shape
    f = wg.shape[2]
    return pl.pallas_call(
        _ffn_kernel,
        grid=(t // tm, f // tf),
        in_specs=[
            pl.BlockSpec((tm, d), lambda i, j: (i, 0)),
            pl.BlockSpec((None, 1, d), lambda i, j: (layer, 0, 0)),
            pl.BlockSpec((None, 1, d), lambda i, j: (layer, 0, 0)),
            pl.BlockSpec((None, d, tf), lambda i, j: (0, 0, j)),
            pl.BlockSpec((None, d, tf), lambda i, j: (0, 0, j)),
            pl.BlockSpec((None, tf, d), lambda i, j: (0, j, 0)),
        ],
        out_specs=pl.BlockSpec((tm, d), lambda i, j: (i, 0)),
        out_shape=jax.ShapeDtypeStruct((t, d), F32),
        scratch_shapes=[pltpu.VMEM((tm, d), BF16), pltpu.VMEM((tm, LANES), F32)],
        compiler_params=pltpu.CompilerParams(
            dimension_semantics=("parallel", "arbitrary"), vmem_limit_bytes=VMEM_LIMIT_BIG),
        name="ffn",
    )(x, g_pre, g_post, wg, wu, wd)


def _proj_kernel(x_ref, g_ref, wt_ref, o_ref):
    o_ref[...] = _dot_nt(_rms(x_ref[...], g_ref[...]).astype(BF16), wt_ref[...])


def _proj(x, g, wt, *, layer, tm, tn):
    t, d = x.shape
    n = wt.shape[1]
    return pl.pallas_call(
        _proj_kernel,
        grid=(n // tn, t // tm),
        in_specs=[
            pl.BlockSpec((tm, d), lambda j, i: (i, 0)),
            pl.BlockSpec((None, 1, d), lambda j, i: (layer, 0, 0)),
            pl.BlockSpec((None, tn, d), lambda j, i: (0, j, 0), pipeline_mode=pl.Buffered(1)),
        ],
        out_specs=pl.BlockSpec((tm, tn), lambda j, i: (i, j)),
        out_shape=jax.ShapeDtypeStruct((t, n), F32),
        compiler_params=pltpu.CompilerParams(
            dimension_semantics=("arbitrary", "arbitrary"), vmem_limit_bytes=VMEM_LIMIT_BIG),
        name="proj_in",
    )(x, g, wt)


def _mixout_kernel(x_ref, ya_ref, yr_ref, yp_ref, wa_ref, wr_ref, wp_ref, g_ref, o_ref):
    mix = _dot(ya_ref[...], wa_ref[...]) + _dot(yr_ref[...], wr_ref[...]) + _dot(yp_ref[...], wp_ref[...])
    o_ref[...] = x_ref[...] + _rms(mix, g_ref[...])


def _mixout(x, ya, yr, yp, wo, g, *, layer, tm):
    t, d = x.shape
    da, dr, dp = ya.shape[1], yr.shape[1], yp.shape[1]
    assert da == dr and (da + dr) % dp == 0
    return pl.pallas_call(
        _mixout_kernel,
        grid=(t // tm,),
        in_specs=[
            pl.BlockSpec((tm, d), lambda i: (i, 0)),
            pl.BlockSpec((tm, da), lambda i: (i, 0)),
            pl.BlockSpec((tm, dr), lambda i: (i, 0)),
            pl.BlockSpec((tm, dp), lambda i: (i, 0)),
            pl.BlockSpec((None, da, d), lambda i: (0, 0, 0)),
            pl.BlockSpec((None, dr, d), lambda i: (0, 1, 0)),
            pl.BlockSpec((None, dp, d), lambda i: (0, (da + dr) // dp, 0)),
            pl.BlockSpec((None, 1, d), lambda i: (layer, 0, 0)),
        ],
        out_specs=pl.BlockSpec((tm, d), lambda i: (i, 0)),
        out_shape=jax.ShapeDtypeStruct((t, d), F32),
        compiler_params=pltpu.CompilerParams(
            dimension_semantics=("parallel",), vmem_limit_bytes=VMEM_LIMIT),
        name="proj_out",
    )(x, ya, yr, yp, wo, wo, wo, g)


def _cast_kernel(w_ref, o_ref):
    o_ref[...] = w_ref[...].astype(BF16)


def _cast_bf16(w, *, layer, tr):
    _, r, c = w.shape
    return pl.pallas_call(
        _cast_kernel,
        grid=(r // tr,),
        in_specs=[pl.BlockSpec((1, tr, c), lambda j: (layer, j, 0))],
        out_specs=pl.BlockSpec((1, tr, c), lambda j: (0, j, 0)),
        out_shape=jax.ShapeDtypeStruct((1, r, c), BF16),
        compiler_params=pltpu.CompilerParams(
            dimension_semantics=("parallel",), vmem_limit_bytes=VMEM_LIMIT),
        name="cast_bf16",
    )(w)


def _att_bias_tile(base, tq, tk):
    row = lax.broadcasted_iota(jnp.int32, (tq, tk), 0)
    col = lax.broadcasted_iota(jnp.int32, (tq, tk), 1)
    delta = base + row - col
    cnt = jnp.zeros((tq, tk), jnp.int32)
    for window, dil in DILATED_BRANCHES:
        hit = (delta >= 0) & (delta <= window) & ((delta & (dil - 1)) == 0)
        cnt = cnt + hit.astype(jnp.int32)
    return jnp.where(cnt == 3, math.log2(3.0),
                     jnp.where(cnt == 2, 1.0, jnp.where(cnt == 1, 0.0, NEG_BIG))).astype(BF16)


def _att_kernel(q_ref, k_ref, v_ref, cos_ref, sin_ref, o_ref,
                qa_s, qb_s, k_s, va_s, vb_s, bias_s, *, seq, tq, tk, group):
    nq = seq // tq
    ratio = tk // tq

    @pl.when((pl.program_id(0) == 0) & (pl.program_id(1) == 0))
    def _():
        for off in range(nq):
            bias_s[off] = _att_bias_tile(off * tq, tq, tk)

    lane = lax.broadcasted_iota(jnp.int32, (tq, LANES), 1)
    first_half = (lane & (HEAD_DIM - 1)) < (HEAD_DIM // 2)
    head_a = lane < HEAD_DIM

    def rope(x, cos, sin):
        swapped = jnp.where(first_half, pltpu.roll(x, LANES - HEAD_DIM // 2, 1), pltpu.roll(x, HEAD_DIM // 2, 1))
        return x * cos + swapped * sin

    staged_q, staged_kv = set(), set()

    def stage_q(i):
        if i in staged_q:
            return
        staged_q.add(i)
        rows = slice(i * tq, (i + 1) * tq)
        q = rope(q_ref[0, rows, :], cos_ref[rows, :], sin_ref[rows, :]) * (HEAD_DIM ** -0.5 * LOG2_E)
        qa_s[rows, :] = jnp.where(head_a, q, 0.0).astype(BF16)
        qb_s[rows, :] = jnp.where(head_a, 0.0, q).astype(BF16)

    def stage_kv(i):
        if i in staged_kv:
            return
        staged_kv.add(i)
        rows = slice(i * tq, (i + 1) * tq)
        k_s[rows, :] = rope(k_ref[0, rows, :], cos_ref[rows, :], sin_ref[rows, :]).astype(BF16)
        v = v_ref[0, rows, :]
        va_s[rows, :] = jnp.where(head_a, v, 1.0).astype(BF16)
        vb_s[rows, :] = jnp.where(head_a, 1.0, v).astype(BF16)

    def tile_max(s):
        m = s[:, 0:LANES]
        for t in range(1, s.shape[1] // LANES):
            m = jnp.maximum(m, s[:, t * LANES:(t + 1) * LANES])
        return m

    q_s = (qa_s, qb_s)
    v_s = (va_s, vb_s)
    m_run = {}
    acc = {}
    for kj in range(seq // tk):
        blocks = list(range(ratio * kj, nq))
        for g0 in range(0, len(blocks), group):
            chains = [(qi, hd) for qi in blocks[g0:g0 + group] for hd in range(2)]
            width = {qi: (tq if qi == ratio * kj else tk) for qi, _ in chains}
            for qi, _ in chains:
                stage_q(qi)
                for kb in range(ratio * kj, ratio * kj + width[qi] // tq):
                    stage_kv(kb)
            s = {}
            for qi, hd in chains:
                krows = slice(kj * tk, kj * tk + width[qi])
                s[qi, hd] = (_dot_nt(q_s[hd][qi * tq:(qi + 1) * tq, :], k_s[krows, :]).astype(BF16)
                             + bias_s[qi - ratio * kj, :, 0:width[qi]])
            p = {}
            for ch in chains:
                m_tile = jnp.max(tile_max(s[ch]).astype(F32), axis=-1, keepdims=True)
                if kj == 0:
                    m_new = m_tile
                else:
                    m_new = jnp.maximum(m_run[ch], m_tile)
                    acc[ch] = jnp.exp2(m_run[ch] - m_new) * acc[ch]
                m_run[ch] = m_new
                p[ch] = jnp.exp2(s[ch] - m_new.astype(BF16))
            for ch in chains:
                pv = _dot(p[ch], v_s[ch[1]][kj * tk:kj * tk + width[ch[0]], :])
                acc[ch] = pv if kj == 0 else acc[ch] + pv

    for qi in range(nq):
        o_a = acc[qi, 0] / pltpu.roll(acc[qi, 0], HEAD_DIM, 1)
        o_b = acc[qi, 1] / pltpu.roll(acc[qi, 1], HEAD_DIM, 1)
        o_ref[0, qi * tq:(qi + 1) * tq, :] = jnp.where(head_a, o_a, o_b).astype(o_ref.dtype)


def _attention(z3, cos_t, sin_t):
    b, s, _ = z3.shape
    tk = min(ATT_KV_TILE, s)
    tq = tk // 2
    npair = D_ATT // LANES
    kern = functools.partial(_att_kernel, seq=s, tq=tq, tk=tk, group=ATT_GROUP)
    return pl.pallas_call(
        kern,
        grid=(b, npair),
        in_specs=[
            pl.BlockSpec((1, s, LANES), lambda i, p: (i, 0, COL_ATT // LANES + p)),
            pl.BlockSpec((1, s, LANES), lambda i, p: (i, 0, (COL_ATT + D_ATT) // LANES + p)),
            pl.BlockSpec((1, s, LANES), lambda i, p: (i, 0, (COL_ATT + 2 * D_ATT) // LANES + p)),
            pl.BlockSpec((s, LANES), lambda i, p: (0, 0)),
            pl.BlockSpec((s, LANES), lambda i, p: (0, 0)),
        ],
        out_specs=pl.BlockSpec((1, s, LANES), lambda i, p: (i, 0, p)),
        out_shape=jax.ShapeDtypeStruct((b, s, D_ATT), BF16),
        scratch_shapes=[pltpu.VMEM((s, LANES), BF16)] * 5 + [pltpu.VMEM((s // tq, tq, tk), BF16)],
        compiler_params=pltpu.CompilerParams(
            dimension_semantics=("arbitrary", "arbitrary"), vmem_limit_bytes=VMEM_LIMIT),
        name="dilated_attention",
    )(z3, z3, z3, cos_t, sin_t)


POOL_PAD = max(POOL_WINDOWS)


def _pool_kernel(u_ref, w_ref, sc_ref, o_ref, pad_s, *, seq):
    t1 = (lax.broadcasted_iota(jnp.int32, (seq, POOL_GROUP), 0) + 1).astype(F32)
    pad_s[0:POOL_PAD, :] = jnp.zeros((POOL_PAD, POOL_GROUP), F32)
    for gi, w in enumerate(POOL_WINDOWS):
        cols = slice(gi * POOL_GROUP, (gi + 1) * POOL_GROUP)
        u = u_ref[0, :, cols]
        s = u
        k = 1
        while k < w:
            pad_s[POOL_PAD:POOL_PAD + seq, :] = s
            s = s + pad_s[POOL_PAD - k:POOL_PAD - k + seq, :]
            k *= 2
        p = s / jnp.minimum(t1, float(w)) - u
        y = _dot(p.astype(BF16), w_ref[gi])
        o_ref[0, :, cols] = (y * sc_ref[:, cols]).astype(o_ref.dtype)


def _pool(z3, pool_w, pool_scale):
    b, s, _ = z3.shape
    kern = functools.partial(_pool_kernel, seq=s)
    return pl.pallas_call(
        kern,
        grid=(b,),
        in_specs=[
            pl.BlockSpec((1, s, D_POOL), lambda i: (i, 0, COL_POOL // D_POOL)),
            pl.BlockSpec(pool_w.shape, lambda i: (0, 0, 0)),
            pl.BlockSpec((1, D_POOL), lambda i: (0, 0)),
        ],
        out_specs=pl.BlockSpec((1, s, D_POOL), lambda i: (i, 0, 0)),
        out_shape=jax.ShapeDtypeStruct((b, s, D_POOL), BF16),
        scratch_shapes=[pltpu.VMEM((s + POOL_PAD, POOL_GROUP), F32)],
        compiler_params=pltpu.CompilerParams(
            dimension_semantics=("parallel",), vmem_limit_bytes=VMEM_LIMIT),
        name="pool_mixer",
    )(z3, pool_w, pool_scale)


def _split_dot(x, ones_b, terms):
    acc = None
    rem = x
    for _ in range(terms):
        piece = rem.astype(BF16)
        part = _dot(piece, ones_b)
        acc = part if acc is None else acc + part
        rem = rem - piece.astype(F32)
    return acc


N_RWKV_IN = 18


def _rwkv_kernel(*refs, chunk, rows, n_steps, cast_steps):
    (zr_ref, zrp_ref, zl_ref, zlp_ref, mur_ref, mul_ref, wup_ref, w0_ref, aup_ref, a0_ref,
     gup_ref, kk_ref, ka_ref, rk_ref, gnw_ref, gnb_ref, tri_ref, ones_ref) = refs[:N_RWKV_IN]
    n_cast = len(cast_steps)
    cast_in = refs[N_RWKV_IN:N_RWKV_IN + n_cast]
    o_ref = refs[N_RWKV_IN + n_cast]
    cast_out = refs[N_RWKV_IN + n_cast + 1:N_RWKV_IN + 2 * n_cast + 1]
    h_s, sh_s = refs[N_RWKV_IN + 2 * n_cast + 1:]
    c = pl.program_id(1)
    npair = D_RWKV // LANES
    halo = F32_SUBLANES
    wr = 3 * D_RWKV

    step = pl.program_id(0) * pl.num_programs(1) + c
    for src, dst, used in zip(cast_in, cast_out, cast_steps):
        if used == n_steps:
            dst[...] = src[...].astype(BF16)
        else:
            @pl.when(step < used)
            def _(src=src, dst=dst):
                dst[...] = src[...].astype(BF16)

    @pl.when(c == 0)
    def _():
        h_s[...] = jnp.zeros_like(h_s)

    row = lax.broadcasted_iota(jnp.int32, (LANES, LANES), 0)
    col = lax.broadcasted_iota(jnp.int32, (LANES, LANES), 1)
    lane = lax.broadcasted_iota(jnp.int32, (chunk, LANES), 1)
    head_a = lane < HEAD_DIM
    strict_lower = row > col
    lower = row >= col
    eye = (row == col).astype(F32)
    tri = tri_ref[...]
    first = c == 0

    def seg_sum(x):
        m = x.shape[0]
        tall = jnp.concatenate([x[:, p * LANES:(p + 1) * LANES] for p in range(npair)], axis=0)
        summed = _split_dot(tall, ones_ref[...], SEG_SUM_TERMS)
        return jnp.concatenate([summed[p * m:(p + 1) * m] for p in range(npair)], axis=1)

    def stack2(x):
        return jnp.concatenate([jnp.where(head_a, x, 0.0), jnp.where(head_a, 0.0, x)], axis=0)

    def stack2_b(x):
        xb = x.astype(BF16)
        zero = jnp.zeros_like(xb)
        return jnp.concatenate([jnp.where(head_a, xb, zero), jnp.where(head_a, zero, xb)], axis=0)

    zr_rows, zl_rows = [], []
    for bi in range(rows):
        zr = zr_ref[bi]
        zl = zl_ref[bi]
        sh_s[bi, 0:halo, 0:wr] = jnp.where(first, 0.0, zrp_ref[bi])
        sh_s[bi, 0:halo, wr:wr + D_LORA_PACK] = jnp.where(first, 0.0, zlp_ref[bi])
        sh_s[bi, halo:halo + chunk, 0:wr] = zr
        sh_s[bi, halo:halo + chunk, wr:wr + D_LORA_PACK] = zl
        zr_prev = sh_s[bi, halo - 1:halo - 1 + chunk, 0:wr]
        zl_prev = sh_s[bi, halo - 1:halo - 1 + chunk, wr:wr + D_LORA_PACK]
        zr_rows.append(zr + (zr_prev - zr) * mur_ref[...])
        zl_rows.append(zl + (zl_prev - zl) * mul_ref[...])
    zr = jnp.concatenate(zr_rows, axis=0)
    zl = jnp.concatenate(zl_rows, axis=0)
    r = zr[:, 0:D_RWKV]
    k = zr[:, D_RWKV:2 * D_RWKV]
    v = zr[:, 2 * D_RWKV:3 * D_RWKV]
    zw = zl[:, 0:LORA_PAD]
    za = zl[:, LORA_PAD:2 * LORA_PAD]
    zg = zl[:, 2 * LORA_PAD:D_LORA_PACK]

    lw = w0_ref[...] + _dot(jnp.tanh(zw).astype(BF16), wup_ref[...])
    ld = -math.exp(-0.5) * jax.nn.sigmoid(lw)
    a = jax.nn.sigmoid(a0_ref[...] + _dot(za.astype(BF16), aup_ref[...]))
    g = _dot(jax.nn.sigmoid(zg).astype(BF16), gup_ref[...])
    kk = k * kk_ref[...]
    k2 = k * (1.0 + (a - 1.0) * ka_ref[...])
    kk = kk / jnp.maximum(jnp.sqrt(seg_sum(kk * kk)), 1e-12)
    bonus = seg_sum(r * k2 * rk_ref[...]) * v
    bb = kk * a

    cum_rows = [_split_dot_left(tri, ld[bi * chunk:(bi + 1) * chunk], 3) for bi in range(rows)]
    cum = jnp.concatenate(cum_rows, axis=0)
    cum_end = jnp.concatenate([jnp.broadcast_to(cr[chunk - 1:chunk, :], cr.shape) for cr in cum_rows], axis=0)
    e_inv = jnp.exp(-cum)
    e_end = jnp.exp(cum_end - cum)
    tok_all = dict(
        r_t=r * jnp.exp(cum), kk_t=kk * jnp.exp(cum - ld), b_h=bb * e_inv, k_h=k2 * e_inv,
        b_a=bb * e_end, k_a=k2 * e_end, v=v)
    tok = [{name: val[bi * chunk:(bi + 1) * chunk] for name, val in tok_all.items()} for bi in range(rows)]
    for bi in range(rows):
        tok[bi]["a_end"] = jnp.exp(cum_rows[bi][chunk - 1:chunk, :])

    units = [(bi, p) for bi in range(rows) for p in range(npair)]

    def pair(name):
        return [tok[bi][name][:, p * LANES:(p + 1) * LANES] for bi, p in units]

    kks = [stack2_b(x) for x in pair("kk_t")]
    bs = [stack2_b(x) for x in pair("b_h")]
    ks = [stack2_b(x) for x in pair("k_h")]
    rs = [stack2_b(x) for x in pair("r_t")]
    vs = [stack2_b(x) for x in pair("v")]
    bat = [stack2(x).T.astype(BF16) for x in pair("b_a")]
    kat = [stack2(x).T.astype(BF16) for x in pair("k_a")]
    a_col = [jnp.broadcast_to(x, (LANES, LANES)).T for x in pair("a_end")]
    ids = range(len(units))

    tt = [_dot_nt(jnp.concatenate([kks[i], rs[i]], axis=0), jnp.concatenate([bs[i], ks[i]], axis=0)) for i in ids]
    l_b = [jnp.where(strict_lower, tt[i][0:LANES, 0:LANES], 0.0) for i in ids]
    l_k = [jnp.where(strict_lower, tt[i][0:LANES, LANES:2 * LANES], 0.0).astype(BF16) for i in ids]
    m_b = [jnp.where(lower, tt[i][LANES:2 * LANES, 0:LANES], 0.0).astype(BF16) for i in ids]
    m_k = [jnp.where(lower, tt[i][LANES:2 * LANES, LANES:2 * LANES], 0.0).astype(BF16) for i in ids]

    sel = ((row ^ col) < 2) & ((row & 1) != 0) & ((col & 1) == 0)
    t_inv = [eye - jnp.where(sel, l_b[i], 0.0) for i in ids]
    s = 2
    while s < HEAD_DIM:
        sel = ((row ^ col) < 2 * s) & ((row & s) != 0) & ((col & s) == 0)
        t_b = [t_inv[i].astype(BF16) for i in ids]
        inner = [_dot(jnp.where(sel, l_b[i], 0.0).astype(BF16), t_b[i]).astype(BF16) for i in ids]
        t_inv = [t_inv[i] - _dot(t_b[i], inner[i]) for i in ids]
        s *= 2
    t_b = [t_inv[i].astype(BF16) for i in ids]

    mlv = [_dot(jnp.concatenate([m_k[i], l_k[i]], axis=0), vs[i]) for i in ids]
    y0 = [mlv[i][0:LANES, :] for i in ids]
    lkv = [mlv[i][LANES:2 * LANES, :].astype(BF16) for i in ids]
    w_m = [_dot(t_b[i], kks[i]).astype(BF16) for i in ids]

    h = [h_s[i] for i in ids]
    h_b = [h[i].astype(BF16) for i in ids]
    u_b = [(-_dot(jnp.concatenate([w_m[i], t_b[i]], axis=1),
                  jnp.concatenate([h_b[i], lkv[i]], axis=0))).astype(BF16) for i in ids]
    y = [_dot(jnp.concatenate([rs[i], m_b[i]], axis=1), jnp.concatenate([h_b[i], u_b[i]], axis=0)) + y0[i]
         for i in ids]
    for i in ids:
        h_s[i] = a_col[i] * h[i] + _dot(jnp.concatenate([bat[i], kat[i]], axis=1),
                                        jnp.concatenate([u_b[i], vs[i]], axis=0))

    yb = jnp.concatenate(
        [jnp.concatenate([y[bi * npair + p][0:chunk, :] + y[bi * npair + p][chunk:2 * chunk, :]
                          for p in range(npair)], axis=1) for bi in range(rows)], axis=0)
    mean = seg_sum(yb) * (1.0 / HEAD_DIM)
    d = yb - mean
    var = seg_sum(d * d) * (1.0 / HEAD_DIM)
    yn = d * lax.rsqrt(var + RWKV_GN_EPS) * gnw_ref[...] + gnb_ref[...]
    out = ((yn + bonus) * g).astype(o_ref.dtype)
    for bi in range(rows):
        o_ref[bi] = out[bi * chunk:(bi + 1) * chunk]


def _split_dot_left(tri_b, x, terms):
    acc = None
    rem = x
    for _ in range(terms):
        piece = rem.astype(BF16)
        part = _dot(tri_b, piece)
        acc = part if acc is None else acc + part
        rem = rem - piece.astype(F32)
    return acc


BF16_SUBLANES = 16


def _rwkv_ones(chunk):
    t = jnp.arange(chunk)
    tri = (t[:, None] >= t[None, :]).astype(BF16)
    r = jnp.arange(LANES)[:, None]
    c = jnp.arange(LANES)[None, :]
    head_ones = ((r < HEAD_DIM) == (c < HEAD_DIM)).astype(BF16)
    return tri, head_ones


def _rwkv(z3, prm, *, rows, cast=()):
    b, s, _ = z3.shape
    chunk = RWKV_CHUNK
    nc = s // chunk
    nb = b // rows
    wr = 3 * D_RWKV
    rows8 = chunk // F32_SUBLANES
    npair = D_RWKV // LANES

    def vec(n):
        return pl.BlockSpec((1, n), lambda i, c: (0, 0))

    def full(a):
        return pl.BlockSpec(a.shape, lambda i, c: (0, 0))

    tri, head_ones = _rwkv_ones(chunk)

    n_steps = nb * nc
    cast_in_specs, cast_out_specs, cast_shapes, cast_steps = [], [], [], []
    for w, layer in cast:
        _, r, cc = w.shape
        assert r % BF16_SUBLANES == 0
        used = max(k for k in range(1, n_steps + 1) if (r // BF16_SUBLANES) % k == 0)
        tr = r // used
        row_blk = lambda i, c, last=used - 1: jnp.minimum(i * nc + c, last)
        cast_in_specs.append(pl.BlockSpec((1, tr, cc), lambda i, c, f=row_blk, l=layer: (l, f(i, c), 0)))
        cast_out_specs.append(pl.BlockSpec((1, tr, cc), lambda i, c, f=row_blk: (0, f(i, c), 0)))
        cast_shapes.append(jax.ShapeDtypeStruct((1, r, cc), BF16))
        cast_steps.append(used)
    kern = functools.partial(_rwkv_kernel, chunk=chunk, rows=rows, n_steps=n_steps, cast_steps=tuple(cast_steps))

    outs = pl.pallas_call(
        kern,
        grid=(nb, nc),
        in_specs=[
            pl.BlockSpec((rows, chunk, wr), lambda i, c: (i, c, COL_RKV // wr)),
            pl.BlockSpec((rows, F32_SUBLANES, wr),
                         lambda i, c: (i, jnp.maximum(c * rows8 - 1, 0), COL_RKV // wr)),
            pl.BlockSpec((rows, chunk, D_LORA_PACK), lambda i, c: (i, c, COL_LORA // D_LORA_PACK)),
            pl.BlockSpec((rows, F32_SUBLANES, D_LORA_PACK),
                         lambda i, c: (i, jnp.maximum(c * rows8 - 1, 0), COL_LORA // D_LORA_PACK)),
            vec(wr), vec(D_LORA_PACK),
            full(prm["w_up"]), vec(D_RWKV), full(prm["a_up"]), vec(D_RWKV), full(prm["g_up"]),
            vec(D_RWKV), vec(D_RWKV), vec(D_RWKV), vec(D_RWKV), vec(D_RWKV),
            full(tri), full(head_ones),
        ] + cast_in_specs,
        out_specs=[pl.BlockSpec((rows, chunk, D_RWKV), lambda i, c: (i, c, 0))] + cast_out_specs,
        out_shape=[jax.ShapeDtypeStruct((b, s, D_RWKV), BF16)] + cast_shapes,
        scratch_shapes=[
            pltpu.VMEM((rows * npair, LANES, LANES), F32),
            pltpu.VMEM((rows, chunk + F32_SUBLANES, wr + D_LORA_PACK), F32),
        ],
        compiler_params=pltpu.CompilerParams(
            dimension_semantics=("arbitrary", "arbitrary"), vmem_limit_bytes=VMEM_LIMIT),
        name="rwkv7",
    )(z3, z3, z3, z3, prm["mu_r"], prm["mu_l"], prm["w_up"], prm["w0"], prm["a_up"], prm["a0"],
      prm["g_up"], prm["k_k"], prm["k_a"], prm["r_k"], prm["gn_w"], prm["gn_b"],
      tri, head_ones, *[w for w, _ in cast])
    return outs[0], list(outs[1:])


def _pack_w_in_t(wt):
    o = 3 * D_ATT + 3 * D_RWKV
    o2 = o + RWKV_DECAY_LORA
    o3 = o2 + RWKV_AAA_LORA
    pad = ((0, 0), (0, LORA_PAD - RWKV_DECAY_LORA), (0, 0))
    return jnp.concatenate([wt[:, :o], jnp.pad(wt[:, o:o2], pad), jnp.pad(wt[:, o2:o3], pad), wt[:, o3:]], axis=1)


def _pack_mu(mu):
    o = 3 * D_RWKV
    mu_l = jnp.concatenate([
        jnp.pad(mu[o:o + RWKV_DECAY_LORA], (0, LORA_PAD - RWKV_DECAY_LORA)),
        jnp.pad(mu[o + RWKV_DECAY_LORA:o + RWKV_DECAY_LORA + RWKV_AAA_LORA], (0, LORA_PAD - RWKV_AAA_LORA)),
        mu[o + RWKV_DECAY_LORA + RWKV_AAA_LORA:],
    ])
    return mu[None, :o], mu_l[None, :]


def _pad_rows(w, n):
    return jnp.pad(w, ((0, n - w.shape[0]), (0, 0)))


def _rope_tables(seq):
    inv_freq = ROPE_THETA ** (-jnp.arange(0, HEAD_DIM, 2, dtype=F32) / HEAD_DIM)
    ang = jnp.arange(seq, dtype=F32)[:, None] * inv_freq[None, :]
    cos, sin = jnp.cos(ang), jnp.sin(ang)
    reps = LANES // HEAD_DIM
    cos_t = jnp.tile(jnp.concatenate([cos, cos], axis=1), (1, reps))
    sin_t = jnp.tile(jnp.concatenate([-sin, sin], axis=1), (1, reps))
    return cos_t, sin_t


def _tile(n, pref):
    t = min(n, pref)
    assert n % t == 0
    return t


def _row_tile(r, pref):
    return max(k for k in range(BF16_SUBLANES, pref + 1, BF16_SUBLANES) if r % k == 0)


def kernel(x, ffn1_norm_pre, ffn1_norm_post, ffn1_w_gate, ffn1_w_up, ffn1_w_down, mix_norm_pre, mix_norm_post, w_in, w_out, rwkv_mu, rwkv_w_up, rwkv_w0, rwkv_a_up, rwkv_a0, rwkv_g_up, rwkv_k_k, rwkv_k_a, rwkv_r_k, rwkv_gn_w, rwkv_gn_b, pool_w, pool_scale, ffn2_norm_pre, ffn2_norm_post, ffn2_w_gate, ffn2_w_up, ffn2_w_down):
    b, s, d = x.shape
    depth = w_in.shape[0]
    t = b * s
    d_ff = ffn1_w_gate.shape[2]
    assert s % RWKV_CHUNK == 0 and b % RWKV_ROWS == 0 and d == D_ATT + D_RWKV + D_POOL
    tm_ffn, tf = _tile(t, FFN_TM), _tile(d_ff, FFN_TF)
    tm_proj, tn = _tile(t, PROJ_TM), D_PACK // PROJ_COL_BLOCKS
    tm_mix = _tile(t, MIXOUT_TM)
    cos_t, sin_t = _rope_tables(s)
    x2 = x.reshape(t, d).astype(F32)

    w_in_t = jnp.swapaxes(w_in, 1, 2)
    big = (ffn1_w_gate, ffn1_w_up, ffn1_w_down, ffn2_w_gate, ffn2_w_up, ffn2_w_down, w_in_t, w_out)
    early = (0, 1, 2, 6)
    wb = {k: _cast_bf16(big[k], layer=0, tr=_row_tile(big[k].shape[1], CAST_ROWS)) for k in early}
    ffn1_g = (ffn1_norm_pre[:, None, :], ffn1_norm_post[:, None, :])
    ffn2_g = (ffn2_norm_pre[:, None, :], ffn2_norm_post[:, None, :])
    mix_g_pre, mix_g_post = mix_norm_pre[:, None, :], mix_norm_post[:, None, :]

    for i in range(depth):
        x2 = _ffn(x2, *ffn1_g, wb[0], wb[1], wb[2], layer=i, tm=tm_ffn, tf=tf)

        z = _proj(x2, mix_g_pre, _pack_w_in_t(wb[6]), layer=i, tm=tm_proj, tn=tn)
        z3 = z.reshape(b, s, D_PACK)
        y_att = _attention(z3, cos_t, sin_t)
        mu_r, mu_l = _pack_mu(rwkv_mu[i])
        prm = dict(
            mu_r=mu_r, mu_l=mu_l,
            w_up=_pad_rows(rwkv_w_up[i], LORA_PAD).astype(BF16), w0=rwkv_w0[i][None],
            a_up=_pad_rows(rwkv_a_up[i], LORA_PAD).astype(BF16), a0=rwkv_a0[i][None],
            g_up=rwkv_g_up[i].astype(BF16), k_k=rwkv_k_k[i][None], k_a=rwkv_k_a[i][None],
            r_k=rwkv_r_k[i].reshape(1, D_RWKV), gn_w=rwkv_gn_w[i][None], gn_b=rwkv_gn_b[i][None])
        late = [k for k in range(len(big)) if k not in wb]
        nxt = list(range(len(big))) if i + 1 < depth else []
        y_rwkv, cast_out = _rwkv(z3, prm, rows=RWKV_ROWS,
                                 cast=[(big[k], i) for k in late] + [(big[k], i + 1) for k in nxt])
        wb.update(zip(late, cast_out[:len(late)]))
        wb_next = dict(zip(nxt, cast_out[len(late):]))
        y_pool = _pool(z3, pool_w[i].astype(BF16), pool_scale[i][None])

        x2 = _mixout(x2, y_att.reshape(t, D_ATT), y_rwkv.reshape(t, D_RWKV), y_pool.reshape(t, D_POOL),
                     wb[7], mix_g_post, layer=i, tm=tm_mix)

        x2 = _ffn(x2, *ffn2_g, wb[3], wb[4], wb[5], layer=i, tm=tm_ffn, tf=tf)
        wb = wb_next
    return x2.reshape(b, s, d).astype(x.dtype)
```

```python
import functools
import math

import jax
import jax.numpy as jnp
from jax import lax
from jax.experimental import pallas as pl
from jax.experimental.pallas import tpu as pltpu

F32 = jnp.float32
BF16 = jnp.bfloat16

HEAD_DIM = 64
N_ATT_HEADS = 12
N_RWKV_HEADS = 12
D_ATT = N_ATT_HEADS * HEAD_DIM
D_RWKV = N_RWKV_HEADS * HEAD_DIM
D_POOL = 512
N_POOL_GROUPS = 4
POOL_GROUP = D_POOL // N_POOL_GROUPS
POOL_WINDOWS = (2, 4, 8, 16)
DILATED_BRANCHES = ((128, 1), (512, 4), (2048, 16))
ROPE_THETA = 10000.0
RWKV_DECAY_LORA = 96
RWKV_AAA_LORA = 96
RWKV_GATE_LORA = 256
NORM_EPS = 1e-6
RWKV_GN_EPS = 64e-5

LANES = 128
F32_SUBLANES = 8
LORA_PAD = 128
COL_ATT = 0
COL_RKV = 3 * D_ATT
COL_LORA = COL_RKV + 3 * D_RWKV
D_LORA_PACK = 2 * LORA_PAD + RWKV_GATE_LORA
COL_POOL = COL_LORA + D_LORA_PACK
D_PACK = COL_POOL + D_POOL

SEG_SUM_TERMS = 1
RWKV_ROWS = 4
RWKV_CHUNK = 64
ATT_KV_TILE = 512
ATT_GROUP = 2
FFN_TM, FFN_TF = 1024, 512
PROJ_TM, PROJ_COL_BLOCKS = 1024, 2
MIXOUT_TM = 512
CAST_ROWS = 256
V7X_VMEM_BYTES = 64 * 1024 * 1024
VMEM_LIMIT = V7X_VMEM_BYTES - 8 * 1024 * 1024
VMEM_LIMIT_BIG = V7X_VMEM_BYTES - 2 * 1024 * 1024

NEG_BIG = -1e30
LOG2_E = math.log2(math.e)


def _rms(xf, g):
    return xf * lax.rsqrt(jnp.mean(xf * xf, axis=-1, keepdims=True) + NORM_EPS) * g


def _dot(a, b):
    return jnp.dot(a, b, preferred_element_type=F32)


def _dot_nt(a, b):
    return lax.dot_general(a, b, (((1,), (1,)), ((), ())), preferred_element_type=F32)


def _ffn_kernel(x_ref, gpre_ref, gpost_ref, wg_ref, wu_ref, wd_ref, o_ref, h_ref, ssq_ref):
    j = pl.program_id(1)

    @pl.when(j == 0)
    def _():
        h_ref[...] = _rms(x_ref[...], gpre_ref[...]).astype(BF16)
        o_ref[...] = jnp.zeros_like(o_ref)

    h = h_ref[...]
    g = _dot(h, wg_ref[...])
    u = _dot(h, wu_ref[...])
    a = (g * jax.nn.sigmoid(g) * u).astype(BF16)
    acc = o_ref[...] + _dot(a, wd_ref[...])
    o_ref[...] = acc
    sq = acc * acc
    part = sq[:, 0:LANES]
    for t in range(1, sq.shape[1] // LANES):
        part = part + sq[:, t * LANES:(t + 1) * LANES]
    ssq_ref[...] = part

    @pl.when(j == pl.num_programs(1) - 1)
    def _():
        ssq = jnp.sum(ssq_ref[...], axis=-1, keepdims=True)
        inv = lax.rsqrt(ssq * (1.0 / o_ref.shape[1]) + NORM_EPS) * 0.5
        o_ref[...] = x_ref[...] + o_ref[...] * inv * gpost_ref[...]


def _ffn(x, g_pre, g_post, wg, wu, wd, *, layer, tm, tf):
    t, d = x.shape
    f = wg.shape[2]
    return pl.pallas_call(
        _ffn_kernel,
        grid=(t // tm, f // tf),
        in_specs=[
            pl.BlockSpec((tm, d), lambda i, j: (i, 0)),
            pl.BlockSpec((None, 1, d), lambda i, j: (layer, 0, 0)),
            pl.BlockSpec((None, 1, d), lambda i, j: (layer, 0, 0)),
            pl.BlockSpec((None, d, tf), lambda i, j: (0, 0, j)),
            pl.BlockSpec((None, d, tf), lambda i, j: (0, 0, j)),
            pl.BlockSpec((None, tf, d), lambda i, j: (0, j, 0)),
        ],
        out_specs=pl.BlockSpec((tm, d), lambda i, j: (i, 0)),
        out_shape=jax.ShapeDtypeStruct((t, d), F32),
        scratch_shapes=[pltpu.VMEM((tm, d), BF16), pltpu.VMEM((tm, LANES), F32)],
        compiler_params=pltpu.CompilerParams(
            dimension_semantics=("parallel", "arbitrary"), vmem_limit_bytes=VMEM_LIMIT_BIG),
        name="ffn",
    )(x, g_pre, g_post, wg, wu, wd)


def _proj_kernel(x_ref, g_ref, wt_ref, o_ref):
    o_ref[...] = _dot_nt(_rms(x_ref[...], g_ref[...]).astype(BF16), wt_ref[...])


def _proj(x, g, wt, *, layer, tm, tn):
    t, d = x.shape
    n = wt.shape[1]
    return pl.pallas_call(
        _proj_kernel,
        grid=(n // tn, t // tm),
        in_specs=[
            pl.BlockSpec((tm, d), lambda j, i: (i, 0)),
            pl.BlockSpec((None, 1, d), lambda j, i: (layer, 0, 0)),
            pl.BlockSpec((None, tn, d), lambda j, i: (0, j, 0), pipeline_mode=pl.Buffered(1)),
        ],
        out_specs=pl.BlockSpec((tm, tn), lambda j, i: (i, j)),
        out_shape=jax.ShapeDtypeStruct((t, n), F32),
        compiler_params=pltpu.CompilerParams(
            dimension_semantics=("arbitrary", "arbitrary"), vmem_limit_bytes=VMEM_LIMIT_BIG),
        name="proj_in",
    )(x, g, wt)


def _mixout_kernel(x_ref, ya_ref, yr_ref, yp_ref, wa_ref, wr_ref, wp_ref, g_ref, o_ref):
    mix = _dot(ya_ref[...], wa_ref[...]) + _dot(yr_ref[...], wr_ref[...]) + _dot(yp_ref[...], wp_ref[...])
    o_ref[...] = x_ref[...] + _rms(mix, g_ref[...])


def _mixout(x, ya, yr, yp, wo, g, *, layer, tm):
    t, d = x.shape
    da, dr, dp = ya.shape[1], yr.shape[1], yp.shape[1]
    assert da == dr and (da + dr) % dp == 0
    return pl.pallas_call(
        _mixout_kernel,
        grid=(t // tm,),
        in_specs=[
            pl.BlockSpec((tm, d), lambda i: (i, 0)),
            pl.BlockSpec((tm, da), lambda i: (i, 0)),
            pl.BlockSpec((tm, dr), lambda i: (i, 0)),
            pl.BlockSpec((tm, dp), lambda i: (i, 0)),
            pl.BlockSpec((None, da, d), lambda i: (0, 0, 0)),
            pl.BlockSpec((None, dr, d), lambda i: (0, 1, 0)),
            pl.BlockSpec((None, dp, d), lambda i: (0, (da + dr) // dp, 0)),
            pl.BlockSpec((None, 1, d), lambda i: (layer, 0, 0)),
        ],
        out_specs=pl.BlockSpec((tm, d), lambda i: (i, 0)),
        out_shape=jax.ShapeDtypeStruct((t, d), F32),
        compiler_params=pltpu.CompilerParams(
            dimension_semantics=("parallel",), vmem_limit_bytes=VMEM_LIMIT),
        name="proj_out",
    )(x, ya, yr, yp, wo, wo, wo, g)


def _cast_kernel(w_ref, o_ref):
    o_ref[...] = w_ref[...].astype(BF16)


def _cast_bf16(w, *, layer, tr):
    _, r, c = w.shape
    return pl.pallas_call(
        _cast_kernel,
        grid=(r // tr,),
        in_specs=[pl.BlockSpec((1, tr, c), lambda j: (layer, j, 0))],
        out_specs=pl.BlockSpec((1, tr, c), lambda j: (0, j, 0)),
        out_shape=jax.ShapeDtypeStruct((1, r, c), BF16),
        compiler_params=pltpu.CompilerParams(
            dimension_semantics=("parallel",), vmem_limit_bytes=VMEM_LIMIT),
        name="cast_bf16",
    )(w)


def _att_bias_tile(base, tq, tk):
    row = lax.broadcasted_iota(jnp.int32, (tq, tk), 0)
    col = lax.broadcasted_iota(jnp.int32, (tq, tk), 1)
    delta = base + row - col
    cnt = jnp.zeros((tq, tk), jnp.int32)
    for window, dil in DILATED_BRANCHES:
        hit = (delta >= 0) & (delta <= window) & ((delta & (dil - 1)) == 0)
        cnt = cnt + hit.astype(jnp.int32)
    return jnp.where(cnt == 3, math.log2(3.0),
                     jnp.where(cnt == 2, 1.0, jnp.where(cnt == 1, 0.0, NEG_BIG))).astype(BF16)


def _att_kernel(q_ref, k_ref, v_ref, cos_ref, sin_ref, o_ref,
                qa_s, qb_s, k_s, va_s, vb_s, bias_s, *, seq, tq, tk, group):
    nq = seq // tq
    ratio = tk // tq

    @pl.when((pl.program_id(0) == 0) & (pl.program_id(1) == 0))
    def _():
        for off in range(nq):
            bias_s[off] = _att_bias_tile(off * tq, tq, tk)

    lane = lax.broadcasted_iota(jnp.int32, (tq, LANES), 1)
    first_half = (lane & (HEAD_DIM - 1)) < (HEAD_DIM // 2)
    head_a = lane < HEAD_DIM

    def rope(x, cos, sin):
        swapped = jnp.where(first_half, pltpu.roll(x, LANES - HEAD_DIM // 2, 1), pltpu.roll(x, HEAD_DIM // 2, 1))
        return x * cos + swapped * sin

    staged_q, staged_kv = set(), set()

    def stage_q(i):
        if i in staged_q:
            return
        staged_q.add(i)
        rows = slice(i * tq, (i + 1) * tq)
        q = rope(q_ref[0, rows, :], cos_ref[rows, :], sin_ref[rows, :]) * (HEAD_DIM ** -0.5 * LOG2_E)
        qa_s[rows, :] = jnp.where(head_a, q, 0.0).astype(BF16)
        qb_s[rows, :] = jnp.where(head_a, 0.0, q).astype(BF16)

    def stage_kv(i):
        if i in staged_kv:
            return
        staged_kv.add(i)
        rows = slice(i * tq, (i + 1) * tq)
        k_s[rows, :] = rope(k_ref[0, rows, :], cos_ref[rows, :], sin_ref[rows, :]).astype(BF16)
        v = v_ref[0, rows, :]
        va_s[rows, :] = jnp.where(head_a, v, 1.0).astype(BF16)
        vb_s[rows, :] = jnp.where(head_a, 1.0, v).astype(BF16)

    def tile_max(s):
        m = s[:, 0:LANES]
        for t in range(1, s.shape[1] // LANES):
            m = jnp.maximum(m, s[:, t * LANES:(t + 1) * LANES])
        return m

    q_s = (qa_s, qb_s)
    v_s = (va_s, vb_s)
    m_run = {}
    acc = {}
    for kj in range(seq // tk):
        blocks = list(range(ratio * kj, nq))
        for g0 in range(0, len(blocks), group):
            chains = [(qi, hd) for qi in blocks[g0:g0 + group] for hd in range(2)]
            width = {qi: (tq if qi == ratio * kj else tk) for qi, _ in chains}
            for qi, _ in chains:
                stage_q(qi)
                for kb in range(ratio * kj, ratio * kj + width[qi] // tq):
                    stage_kv(kb)
            s = {}
            for qi, hd in chains:
                krows = slice(kj * tk, kj * tk + width[qi])
                s[qi, hd] = (_dot_nt(q_s[hd][qi * tq:(qi + 1) * tq, :], k_s[krows, :]).astype(BF16)
                             + bias_s[qi - ratio * kj, :, 0:width[qi]])
            p = {}
            for ch in chains:
                m_tile = jnp.max(tile_max(s[ch]).astype(F32), axis=-1, keepdims=True)
                if kj == 0:
                    m_new = m_tile
                else:
                    m_new = jnp.maximum(m_run[ch], m_tile)
                    acc[ch] = jnp.exp2(m_run[ch] - m_new) * acc[ch]
                m_run[ch] = m_new
                p[ch] = jnp.exp2(s[ch] - m_new.astype(BF16))
            for ch in chains:
                pv = _dot(p[ch], v_s[ch[1]][kj * tk:kj * tk + width[ch[0]], :])
                acc[ch] = pv if kj == 0 else acc[ch] + pv

    for qi in range(nq):
        o_a = acc[qi, 0] / pltpu.roll(acc[qi, 0], HEAD_DIM, 1)
        o_b = acc[qi, 1] / pltpu.roll(acc[qi, 1], HEAD_DIM, 1)
        o_ref[0, qi * tq:(qi + 1) * tq, :] = jnp.where(head_a, o_a, o_b).astype(o_ref.dtype)


def _attention(z3, cos_t, sin_t):
    b, s, _ = z3.shape
    tk = min(ATT_KV_TILE, s)
    tq = tk // 2
    npair = D_ATT // LANES
    kern = functools.partial(_att_kernel, seq=s, tq=tq, tk=tk, group=ATT_GROUP)
    return pl.pallas_call(
        kern,
        grid=(b, npair),
        in_specs=[
            pl.BlockSpec((1, s, LANES), lambda i, p: (i, 0, COL_ATT // LANES + p)),
            pl.BlockSpec((1, s, LANES), lambda i, p: (i, 0, (COL_ATT + D_ATT) // LANES + p)),
            pl.BlockSpec((1, s, LANES), lambda i, p: (i, 0, (COL_ATT + 2 * D_ATT) // LANES + p)),
            pl.BlockSpec((s, LANES), lambda i, p: (0, 0)),
            pl.BlockSpec((s, LANES), lambda i, p: (0, 0)),
        ],
        out_specs=pl.BlockSpec((1, s, LANES), lambda i, p: (i, 0, p)),
        out_shape=jax.ShapeDtypeStruct((b, s, D_ATT), BF16),
        scratch_shapes=[pltpu.VMEM((s, LANES), BF16)] * 5 + [pltpu.VMEM((s // tq, tq, tk), BF16)],
        compiler_params=pltpu.CompilerParams(
            dimension_semantics=("arbitrary", "arbitrary"), vmem_limit_bytes=VMEM_LIMIT),
        name="dilated_attention",
    )(z3, z3, z3, cos_t, sin_t)


POOL_PAD = max(POOL_WINDOWS)


def _pool_kernel(u_ref, w_ref, sc_ref, o_ref, pad_s, *, seq):
    t1 = (lax.broadcasted_iota(jnp.int32, (seq, POOL_GROUP), 0) + 1).astype(F32)
    pad_s[0:POOL_PAD, :] = jnp.zeros((POOL_PAD, POOL_GROUP), F32)
    for gi, w in enumerate(POOL_WINDOWS):
        cols = slice(gi * POOL_GROUP, (gi + 1) * POOL_GROUP)
        u = u_ref[0, :, cols]
        s = u
        k = 1
        while k < w:
            pad_s[POOL_PAD:POOL_PAD + seq, :] = s
            s = s + pad_s[POOL_PAD - k:POOL_PAD - k + seq, :]
            k *= 2
        p = s / jnp.minimum(t1, float(w)) - u
        y = _dot(p.astype(BF16), w_ref[gi])
        o_ref[0, :, cols] = (y * sc_ref[:, cols]).astype(o_ref.dtype)


def _pool(z3, pool_w, pool_scale):
    b, s, _ = z3.shape
    kern = functools.partial(_pool_kernel, seq=s)
    return pl.pallas_call(
        kern,
        grid=(b,),
        in_specs=[
            pl.BlockSpec((1, s, D_POOL), lambda i: (i, 0, COL_POOL // D_POOL)),
            pl.BlockSpec(pool_w.shape, lambda i: (0, 0, 0)),
            pl.BlockSpec((1, D_POOL), lambda i: (0, 0)),
        ],
        out_specs=pl.BlockSpec((1, s, D_POOL), lambda i: (i, 0, 0)),
        out_shape=jax.ShapeDtypeStruct((b, s, D_POOL), BF16),
        scratch_shapes=[pltpu.VMEM((s + POOL_PAD, POOL_GROUP), F32)],
        compiler_params=pltpu.CompilerParams(
            dimension_semantics=("parallel",), vmem_limit_bytes=VMEM_LIMIT),
        name="pool_mixer",
    )(z3, pool_w, pool_scale)


def _split_dot(x, ones_b, terms):
    acc = None
    rem = x
    for _ in range(terms):
        piece = rem.astype(BF16)
        part = _dot(piece, ones_b)
        acc = part if acc is None else acc + part
        rem = rem - piece.astype(F32)
    return acc


N_RWKV_IN = 18


def _rwkv_kernel(*refs, chunk, rows, n_steps, cast_steps):
    (zr_ref, zrp_ref, zl_ref, zlp_ref, mur_ref, mul_ref, wup_ref, w0_ref, aup_ref, a0_ref,
     gup_ref, kk_ref, ka_ref, rk_ref, gnw_ref, gnb_ref, tri_ref, ones_ref) = refs[:N_RWKV_IN]
    n_cast = len(cast_steps)
    cast_in = refs[N_RWKV_IN:N_RWKV_IN + n_cast]
    o_ref = refs[N_RWKV_IN + n_cast]
    cast_out = refs[N_RWKV_IN + n_cast + 1:N_RWKV_IN + 2 * n_cast + 1]
    h_s, sh_s = refs[N_RWKV_IN + 2 * n_cast + 1:]
    c = pl.program_id(1)
    npair = D_RWKV // LANES
    halo = F32_SUBLANES
    wr = 3 * D_RWKV

    step = pl.program_id(0) * pl.num_programs(1) + c
    for src, dst, used in zip(cast_in, cast_out, cast_steps):
        if used == n_steps:
            dst[...] = src[...].astype(BF16)
        else:
            @pl.when(step < used)
            def _(src=src, dst=dst):
                dst[...] = src[...].astype(BF16)

    @pl.when(c == 0)
    def _():
        h_s[...] = jnp.zeros_like(h_s)

    row = lax.broadcasted_iota(jnp.int32, (LANES, LANES), 0)
    col = lax.broadcasted_iota(jnp.int32, (LANES, LANES), 1)
    lane = lax.broadcasted_iota(jnp.int32, (chunk, LANES), 1)
    head_a = lane < HEAD_DIM
    strict_lower = row > col
    lower = row >= col
    eye = (row == col).astype(F32)
    tri = tri_ref[...]
    first = c == 0

    def seg_sum(x):
        m = x.shape[0]
        tall = jnp.concatenate([x[:, p * LANES:(p + 1) * LANES] for p in range(npair)], axis=0)
        summed = _split_dot(tall, ones_ref[...], SEG_SUM_TERMS)
        return jnp.concatenate([summed[p * m:(p + 1) * m] for p in range(npair)], axis=1)

    def stack2_b(x):
        xb = x.astype(BF16)
        zero = jnp.zeros_like(xb)
        return jnp.concatenate([jnp.where(head_a, xb, zero), jnp.where(head_a, zero, xb)], axis=0)

    zr_rows, zl_rows = [], []
    for bi in range(rows):
        zr = zr_ref[bi]
        zl = zl_ref[bi]
        sh_s[bi, 0:halo, 0:wr] = jnp.where(first, 0.0, zrp_ref[bi])
        sh_s[bi, 0:halo, wr:wr + D_LORA_PACK] = jnp.where(first, 0.0, zlp_ref[bi])
        sh_s[bi, halo:halo + chunk, 0:wr] = zr
        sh_s[bi, halo:halo + chunk, wr:wr + D_LORA_PACK] = zl
        zr_prev = sh_s[bi, halo - 1:halo - 1 + chunk, 0:wr]
        zl_prev = sh_s[bi, halo - 1:halo - 1 + chunk, wr:wr + D_LORA_PACK]
        zr_rows.append(zr + (zr_prev - zr) * mur_ref[...])
        zl_rows.append(zl + (zl_prev - zl) * mul_ref[...])
    zr = jnp.concatenate(zr_rows, axis=0)
    zl = jnp.concatenate(zl_rows, axis=0)
    r = zr[:, 0:D_RWKV]
    k = zr[:, D_RWKV:2 * D_RWKV]
    v = zr[:, 2 * D_RWKV:3 * D_RWKV]
    zw = zl[:, 0:LORA_PAD]
    za = zl[:, LORA_PAD:2 * LORA_PAD]
    zg = zl[:, 2 * LORA_PAD:D_LORA_PACK]

    lw = w0_ref[...] + _dot(jnp.tanh(zw).astype(BF16), wup_ref[...])
    ld = -math.exp(-0.5) * jax.nn.sigmoid(lw)
    a = jax.nn.sigmoid(a0_ref[...] + _dot(za.astype(BF16), aup_ref[...]))
    g = _dot(jax.nn.sigmoid(zg).astype(BF16), gup_ref[...])
    kk = k * kk_ref[...]
    k2 = k * (1.0 + (a - 1.0) * ka_ref[...])
    kk = kk / jnp.maximum(jnp.sqrt(seg_sum(kk * kk)), 1e-12)
    bonus = seg_sum(r * k2 * rk_ref[...]) * v
    bb = kk * a

    cum_rows = [_split_dot_left(tri, ld[bi * chunk:(bi + 1) * chunk], 3) for bi in range(rows)]
    cum = jnp.concatenate(cum_rows, axis=0)
    cum_end = jnp.concatenate([jnp.broadcast_to(cr[chunk - 1:chunk, :], cr.shape) for cr in cum_rows], axis=0)
    e_inv = jnp.exp(-cum)
    e_end = jnp.exp(cum_end - cum)
    tok_all = dict(
        r_t=r * jnp.exp(cum), kk_t=kk * jnp.exp(cum - ld), b_h=bb * e_inv, k_h=k2 * e_inv,
        b_a=bb * e_end, k_a=k2 * e_end, v=v)
    tok = [{name: val[bi * chunk:(bi + 1) * chunk] for name, val in tok_all.items()} for bi in range(rows)]
    for bi in range(rows):
        tok[bi]["a_end"] = jnp.exp(cum_rows[bi][chunk - 1:chunk, :])

    units = [(bi, p) for bi in range(rows) for p in range(npair)]

    def pair(name):
        return [tok[bi][name][:, p * LANES:(p + 1) * LANES] for bi, p in units]

    kks = [stack2_b(x) for x in pair("kk_t")]
    bs = [stack2_b(x) for x in pair("b_h")]
    ks = [stack2_b(x) for x in pair("k_h")]
    rs = [stack2_b(x) for x in pair("r_t")]
    vs = [stack2_b(x) for x in pair("v")]
    bat = [stack2_b(x).T for x in pair("b_a")]
    kat = [stack2_b(x).T for x in pair("k_a")]
    a_col = [jnp.broadcast_to(x, (LANES, LANES)).T for x in pair("a_end")]
    ids = range(len(units))

    tt = [_dot_nt(jnp.concatenate([kks[i], rs[i]], axis=0), jnp.concatenate([bs[i], ks[i]], axis=0)) for i in ids]
    l_b = [jnp.where(strict_lower, tt[i][0:LANES, 0:LANES], 0.0) for i in ids]
    l_k = [jnp.where(strict_lower, tt[i][0:LANES, LANES:2 * LANES], 0.0).astype(BF16) for i in ids]
    m_b = [jnp.where(lower, tt[i][LANES:2 * LANES, 0:LANES], 0.0).astype(BF16) for i in ids]
    m_k = [jnp.where(lower, tt[i][LANES:2 * LANES, LANES:2 * LANES], 0.0).astype(BF16) for i in ids]

    sel = ((row ^ col) < 2) & ((row & 1) != 0) & ((col & 1) == 0)
    t_inv = [eye - jnp.where(sel, l_b[i], 0.0) for i in ids]
    s = 2
    while s < HEAD_DIM:
        sel = ((row ^ col) < 2 * s) & ((row & s) != 0) & ((col & s) == 0)
        t_b = [t_inv[i].astype(BF16) for i in ids]
        inner = [_dot(jnp.where(sel, l_b[i], 0.0).astype(BF16), t_b[i]).astype(BF16) for i in ids]
        t_inv = [t_inv[i] - _dot(t_b[i], inner[i]) for i in ids]
        s *= 2
    t_b = [t_inv[i].astype(BF16) for i in ids]

    mlv = [_dot(jnp.concatenate([m_k[i], l_k[i]], axis=0), vs[i]) for i in ids]
    y0 = [mlv[i][0:LANES, :] for i in ids]
    lkv = [mlv[i][LANES:2 * LANES, :].astype(BF16) for i in ids]
    w_m = [_dot(t_b[i], kks[i]).astype(BF16) for i in ids]

    h = [h_s[i] for i in ids]
    h_b = [h[i].astype(BF16) for i in ids]
    u_b = [(-_dot(jnp.concatenate([w_m[i], t_b[i]], axis=1),
                  jnp.concatenate([h_b[i], lkv[i]], axis=0))).astype(BF16) for i in ids]
    y = [_dot(jnp.concatenate([rs[i], m_b[i]], axis=1), jnp.concatenate([h_b[i], u_b[i]], axis=0)) + y0[i]
         for i in ids]
    for i in ids:
        h_s[i] = a_col[i] * h[i] + _dot(jnp.concatenate([bat[i], kat[i]], axis=1),
                                        jnp.concatenate([u_b[i], vs[i]], axis=0))

    yb = jnp.concatenate(
        [jnp.concatenate([y[bi * npair + p][0:chunk, :] + y[bi * npair + p][chunk:2 * chunk, :]
                          for p in range(npair)], axis=1) for bi in range(rows)], axis=0)
    mean = seg_sum(yb) * (1.0 / HEAD_DIM)
    d = yb - mean
    var = seg_sum(d * d) * (1.0 / HEAD_DIM)
    yn = d * lax.rsqrt(var + RWKV_GN_EPS) * gnw_ref[...] + gnb_ref[...]
    out = ((yn + bonus) * g).astype(o_ref.dtype)
    for bi in range(rows):
        o_ref[bi] = out[bi * chunk:(bi + 1) * chunk]


def _split_dot_left(tri_b, x, terms):
    acc = None
    rem = x
    for _ in range(terms):
        piece = rem.astype(BF16)
        part = _dot(tri_b, piece)
        acc = part if acc is None else acc + part
        rem = rem - piece.astype(F32)
    return acc


BF16_SUBLANES = 16


def _rwkv_ones(chunk):
    t = jnp.arange(chunk)
    tri = (t[:, None] >= t[None, :]).astype(BF16)
    r = jnp.arange(LANES)[:, None]
    c = jnp.arange(LANES)[None, :]
    head_ones = ((r < HEAD_DIM) == (c < HEAD_DIM)).astype(BF16)
    return tri, head_ones


def _rwkv(z3, prm, *, rows, cast=()):
    b, s, _ = z3.shape
    chunk = RWKV_CHUNK
    nc = s // chunk
    nb = b // rows
    wr = 3 * D_RWKV
    rows8 = chunk // F32_SUBLANES
    npair = D_RWKV // LANES

    def vec(n):
        return pl.BlockSpec((1, n), lambda i, c: (0, 0))

    def full(a):
        return pl.BlockSpec(a.shape, lambda i, c: (0, 0))

    tri, head_ones = _rwkv_ones(chunk)

    n_steps = nb * nc
    cast_in_specs, cast_out_specs, cast_shapes, cast_steps = [], [], [], []
    for w, layer in cast:
        _, r, cc = w.shape
        assert r % BF16_SUBLANES == 0
        used = max(k for k in range(1, n_steps + 1) if (r // BF16_SUBLANES) % k == 0)
        tr = r // used
        row_blk = lambda i, c, last=used - 1: jnp.minimum(i * nc + c, last)
        cast_in_specs.append(pl.BlockSpec((1, tr, cc), lambda i, c, f=row_blk, l=layer: (l, f(i, c), 0)))
        cast_out_specs.append(pl.BlockSpec((1, tr, cc), lambda i, c, f=row_blk: (0, f(i, c), 0)))
        cast_shapes.append(jax.ShapeDtypeStruct((1, r, cc), BF16))
        cast_steps.append(used)
    kern = functools.partial(_rwkv_kernel, chunk=chunk, rows=rows, n_steps=n_steps, cast_steps=tuple(cast_steps))

    outs = pl.pallas_call(
        kern,
        grid=(nb, nc),
        in_specs=[
            pl.BlockSpec((rows, chunk, wr), lambda i, c: (i, c, COL_RKV // wr)),
            pl.BlockSpec((rows, F32_SUBLANES, wr),
                         lambda i, c: (i, jnp.maximum(c * rows8 - 1, 0), COL_RKV // wr)),
            pl.BlockSpec((rows, chunk, D_LORA_PACK), lambda i, c: (i, c, COL_LORA // D_LORA_PACK)),
            pl.BlockSpec((rows, F32_SUBLANES, D_LORA_PACK),
                         lambda i, c: (i, jnp.maximum(c * rows8 - 1, 0), COL_LORA // D_LORA_PACK)),
            vec(wr), vec(D_LORA_PACK),
            full(prm["w_up"]), vec(D_RWKV), full(prm["a_up"]), vec(D_RWKV), full(prm["g_up"]),
            vec(D_RWKV), vec(D_RWKV), vec(D_RWKV), vec(D_RWKV), vec(D_RWKV),
            full(tri), full(head_ones),
        ] + cast_in_specs,
        out_specs=[pl.BlockSpec((rows, chunk, D_RWKV), lambda i, c: (i, c, 0))] + cast_out_specs,
        out_shape=[jax.ShapeDtypeStruct((b, s, D_RWKV), BF16)] + cast_shapes,
        scratch_shapes=[
            pltpu.VMEM((rows * npair, LANES, LANES), F32),
            pltpu.VMEM((rows, chunk + F32_SUBLANES, wr + D_LORA_PACK), F32),
        ],
        compiler_params=pltpu.CompilerParams(
            dimension_semantics=("arbitrary", "arbitrary"), vmem_limit_bytes=VMEM_LIMIT),
        name="rwkv7",
    )(z3, z3, z3, z3, prm["mu_r"], prm["mu_l"], prm["w_up"], prm["w0"], prm["a_up"], prm["a0"],
      prm["g_up"], prm["k_k"], prm["k_a"], prm["r_k"], prm["gn_w"], prm["gn_b"],
      tri, head_ones, *[w for w, _ in cast])
    return outs[0], list(outs[1:])


def _pack_w_in_t(wt):
    o = 3 * D_ATT + 3 * D_RWKV
    o2 = o + RWKV_DECAY_LORA
    o3 = o2 + RWKV_AAA_LORA
    pad = ((0, 0), (0, LORA_PAD - RWKV_DECAY_LORA), (0, 0))
    return jnp.concatenate([wt[:, :o], jnp.pad(wt[:, o:o2], pad), jnp.pad(wt[:, o2:o3], pad), wt[:, o3:]], axis=1)


def _pack_mu(mu):
    o = 3 * D_RWKV
    mu_l = jnp.concatenate([
        jnp.pad(mu[o:o + RWKV_DECAY_LORA], (0, LORA_PAD - RWKV_DECAY_LORA)),
        jnp.pad(mu[o + RWKV_DECAY_LORA:o + RWKV_DECAY_LORA + RWKV_AAA_LORA], (0, LORA_PAD - RWKV_AAA_LORA)),
        mu[o + RWKV_DECAY_LORA + RWKV_AAA_LORA:],
    ])
    return mu[None, :o], mu_l[None, :]


def _pad_rows(w, n):
    return jnp.pad(w, ((0, n - w.shape[0]), (0, 0)))


def _rope_tables(seq):
    inv_freq = ROPE_THETA ** (-jnp.arange(0, HEAD_DIM, 2, dtype=F32) / HEAD_DIM)
    ang = jnp.arange(seq, dtype=F32)[:, None] * inv_freq[None, :]
    cos, sin = jnp.cos(ang), jnp.sin(ang)
    reps = LANES // HEAD_DIM
    cos_t = jnp.tile(jnp.concatenate([cos, cos], axis=1), (1, reps))
    sin_t = jnp.tile(jnp.concatenate([-sin, sin], axis=1), (1, reps))
    return cos_t, sin_t


def _tile(n, pref):
    t = min(n, pref)
    assert n % t == 0
    return t


def _row_tile(r, pref):
    return max(k for k in range(BF16_SUBLANES, pref + 1, BF16_SUBLANES) if r % k == 0)


def kernel(x, ffn1_norm_pre, ffn1_norm_post, ffn1_w_gate, ffn1_w_up, ffn1_w_down, mix_norm_pre, mix_norm_post, w_in, w_out, rwkv_mu, rwkv_w_up, rwkv_w0, rwkv_a_up, rwkv_a0, rwkv_g_up, rwkv_k_k, rwkv_k_a, rwkv_r_k, rwkv_gn_w, rwkv_gn_b, pool_w, pool_scale, ffn2_norm_pre, ffn2_norm_post, ffn2_w_gate, ffn2_w_up, ffn2_w_down):
    b, s, d = x.shape
    depth = w_in.shape[0]
    t = b * s
    d_ff = ffn1_w_gate.shape[2]
    assert s % RWKV_CHUNK == 0 and b % RWKV_ROWS == 0 and d == D_ATT + D_RWKV + D_POOL
    tm_ffn, tf = _tile(t, FFN_TM), _tile(d_ff, FFN_TF)
    tm_proj, tn = _tile(t, PROJ_TM), D_PACK // PROJ_COL_BLOCKS
    tm_mix = _tile(t, MIXOUT_TM)
    cos_t, sin_t = _rope_tables(s)
    x2 = x.reshape(t, d).astype(F32)

    w_in_t = jnp.swapaxes(w_in, 1, 2)
    big = (ffn1_w_gate, ffn1_w_up, ffn1_w_down, ffn2_w_gate, ffn2_w_up, ffn2_w_down, w_in_t, w_out)
    early = (0, 1, 2, 6)
    wb = {k: _cast_bf16(big[k], layer=0, tr=_row_tile(big[k].shape[1], CAST_ROWS)) for k in early}
    ffn1_g = (ffn1_norm_pre[:, None, :], ffn1_norm_post[:, None, :])
    ffn2_g = (ffn2_norm_pre[:, None, :], ffn2_norm_post[:, None, :])
    mix_g_pre, mix_g_post = mix_norm_pre[:, None, :], mix_norm_post[:, None, :]

    for i in range(depth):
        x2 = _ffn(x2, *ffn1_g, wb[0], wb[1], wb[2], layer=i, tm=tm_ffn, tf=tf)

        z = _proj(x2, mix_g_pre, _pack_w_in_t(wb[6]), layer=i, tm=tm_proj, tn=tn)
        z3 = z.reshape(b, s, D_PACK)
        y_att = _attention(z3, cos_t, sin_t)
        mu_r, mu_l = _pack_mu(rwkv_mu[i])
        prm = dict(
            mu_r=mu_r, mu_l=mu_l,
            w_up=_pad_rows(rwkv_w_up[i], LORA_PAD).astype(BF16), w0=rwkv_w0[i][None],
            a_up=_pad_rows(rwkv_a_up[i], LORA_PAD).astype(BF16), a0=rwkv_a0[i][None],
            g_up=rwkv_g_up[i].astype(BF16), k_k=rwkv_k_k[i][None], k_a=rwkv_k_a[i][None],
            r_k=rwkv_r_k[i].reshape(1, D_RWKV), gn_w=rwkv_gn_w[i][None], gn_b=rwkv_gn_b[i][None])
        late = [k for k in range(len(big)) if k not in wb]
        nxt = list(range(len(big))) if i + 1 < depth else []
        y_rwkv, cast_out = _rwkv(z3, prm, rows=RWKV_ROWS,
                                 cast=[(big[k], i) for k in late] + [(big[k], i + 1) for k in nxt])
        wb.update(zip(late, cast_out[:len(late)]))
        wb_next = dict(zip(nxt, cast_out[len(late):]))
        y_pool = _pool(z3, pool_w[i].astype(BF16), pool_scale[i][None])

        x2 = _mixout(x2, y_att.reshape(t, D_ATT), y_rwkv.reshape(t, D_RWKV), y_pool.reshape(t, D_POOL),
                     wb[7], mix_g_post, layer=i, tm=tm_mix)

        x2 = _ffn(x2, *ffn2_g, wb[3], wb[4], wb[5], layer=i, tm=tm_ffn, tf=tf)
        wb = wb_next
    return x2.reshape(b, s, d).astype(x.dtype)
```
